```python
import jax, jax.numpy as jnp
from jax import lax
import numpy as np

D_MODEL = 1024
BATCH = 4
SEQ = 8192
DEPTH = 4

GRID_W = 64
CTX_LEN = 256
N_EVEN = (DEPTH + 1) // 2
N_ODD = DEPTH // 2
N_SUB = 3
N_MOD = 3 * N_SUB
D_FF = ((8 * D_MODEL // 3) + 255) // 256 * 256
FFN_RES = 0.5
LRU_WIDTH = D_MODEL // 2
LRU_BW = 64
LRU_BLOCKS = LRU_WIDTH // LRU_BW
LRU_C = 8.0
CONV_W = 4
WIN_HD = 64
WIN_HEADS = (D_MODEL // 2) // WIN_HD
WIN_KV = WIN_HEADS // 4
WIN_G = WIN_HEADS // WIN_KV
WINDOW = 128
GLB_HD = 128
GLB_HEADS = D_MODEL // GLB_HD
GLB_KV = GLB_HEADS // 4
GLB_G = GLB_HEADS // GLB_KV
Q_BLOCK = 128
ROPE_THETA = 10000.0
EPS = 1e-6
EVEN_SPLITS = (LRU_WIDTH, LRU_WIDTH, WIN_HEADS * WIN_HD, WIN_KV * WIN_HD, WIN_KV * WIN_HD)
ODD_SPLITS = (GLB_HEADS * GLB_HD, GLB_KV * GLB_HD, GLB_KV * GLB_HD)
EVEN_IN = sum(EVEN_SPLITS)
ODD_IN = sum(ODD_SPLITS)
EVEN_OUT = LRU_WIDTH + WIN_HEADS * WIN_HD
ODD_OUT = GLB_HEADS * GLB_HD

kernel_name = 'hybrid_dit_rglru_window_axial_block'


def _rmsnorm(x, g):
    xf = x.astype(jnp.float32)
    y = xf * lax.rsqrt(jnp.mean(xf * xf, axis=-1, keepdims=True) + EPS)
    return (y * g.astype(jnp.float32)).astype(x.dtype)


def _modulate(h, shift, scale):
    return h * (1 + scale) + shift


def _split(p, sizes):
    out, o = [], 0
    for s in sizes:
        out.append(p[..., o:o + s])
        o += s
    return out


def _axial_rope(row, col, hd):
    n_freq = hd // 4
    freq = ROPE_THETA ** (-jnp.arange(n_freq, dtype=jnp.float32) / n_freq)
    ang = jnp.concatenate([row.astype(jnp.float32)[:, None] * freq,
                           col.astype(jnp.float32)[:, None] * freq], axis=-1)
    return jnp.cos(ang), jnp.sin(ang)


def _apply_rope(x, rope):
    cos, sin = rope
    hd = x.shape[-1]
    xf = x.astype(jnp.float32).reshape(x.shape[:-1] + (hd // 2, 2))
    shape = (x.shape[1],) + (1,) * (x.ndim - 3) + (hd // 2,)
    cos = cos.reshape(shape)
    sin = sin.reshape(shape)
    x1, x2 = xf[..., 0], xf[..., 1]
    out = jnp.stack([x1 * cos - x2 * sin, x1 * sin + x2 * cos], axis=-1).reshape(x.shape)
    return out.astype(x.dtype)


def _gqa_attend(q, k, v, mask, sink):
    scale = q.shape[-1] ** -0.5
    s = jnp.einsum('bqkgd,bskd->bkgqs', q, k).astype(jnp.float32) * scale
    if mask is not None:
        s = jnp.where(mask, s, -jnp.inf)
    if sink is not None:
        snk = jnp.broadcast_to(sink.astype(jnp.float32)[None, :, :, None, None], s.shape[:-1] + (1,))
        p = jax.nn.softmax(jnp.concatenate([s, snk], axis=-1), axis=-1)[..., :-1]
    else:
        p = jax.nn.softmax(s, axis=-1)
    return jnp.einsum('bkgqs,bskd->bqkgd', p.astype(v.dtype), v)


def _window_attention(q, k, v, kc, vc, sink):
    B_, T = q.shape[0], q.shape[1]
    nb = T // Q_BLOCK
    band = Q_BLOCK + 2 * WINDOW
    pad = ((0, 0), (WINDOW, WINDOW), (0, 0), (0, 0))
    kp = jnp.pad(k, pad)
    vp = jnp.pad(v, pad)
    qb = q.reshape((B_, nb, Q_BLOCK) + q.shape[2:]).swapaxes(0, 1)
    ctx_valid = jnp.ones((Q_BLOCK, kc.shape[1]), dtype=bool)

    def block(args):
        n, q_blk = args
        start = n * Q_BLOCK
        k_loc = lax.dynamic_slice_in_dim(kp, start, band, axis=1)
        v_loc = lax.dynamic_slice_in_dim(vp, start, band, axis=1)
        qpos = start + jnp.arange(Q_BLOCK)
        kpos = start - WINDOW + jnp.arange(band)
        valid = (jnp.abs(kpos[None, :] - qpos[:, None]) <= WINDOW) & (kpos[None, :] >= 0) & (kpos[None, :] < T)
        mask = jnp.concatenate([valid, ctx_valid], axis=1)
        return _gqa_attend(q_blk, jnp.concatenate([k_loc, kc], axis=1),
                           jnp.concatenate([v_loc, vc], axis=1), mask, sink)

    out = lax.map(block, (jnp.arange(nb), qb))
    return out.swapaxes(0, 1).reshape(B_, T, -1)


def _dense_attention(q, k_all, v_all):
    B_, T = q.shape[0], q.shape[1]
    nb = T // Q_BLOCK
    qb = q.reshape((B_, nb, Q_BLOCK) + q.shape[2:]).swapaxes(0, 1)
    out = lax.map(lambda q_blk: _gqa_attend(q_blk, k_all, v_all, None, None), qb)
    return out.swapaxes(0, 1).reshape(B_, T, -1)


def _centred_dwconv(x, w, b):
    left = CONV_W // 2
    right = CONV_W - 1 - left
    L = x.shape[1]
    xp = jnp.pad(x, ((0, 0), (left, right), (0, 0)))
    out = b
    for t in range(CONV_W):
        out = out + xp[:, t:t + L] * w[t]
    return out


def _rglru_coeffs(xin, w_a, b_a, w_x, b_x, lam):
    B_, L, C = xin.shape
    xb = xin.reshape(B_, L, LRU_BLOCKS, LRU_BW)
    gate_a = jnp.einsum('blhi,hij->blhj', xb, w_a).reshape(B_, L, C) + b_a
    gate_x = jnp.einsum('blhi,hij->blhj', xb, w_x).reshape(B_, L, C) + b_x
    r = jax.nn.sigmoid(gate_a.astype(jnp.float32))
    i = jax.nn.sigmoid(gate_x.astype(jnp.float32))
    log_a = -LRU_C * r * jax.nn.softplus(-lam.astype(jnp.float32))
    a = jnp.exp(log_a)
    b = jnp.sqrt(-jnp.expm1(2.0 * log_a)) * i * xin.astype(jnp.float32)
    return a, b


def _combine(e1, e2):
    a1, b1 = e1
    a2, b2 = e2
    return a1 * a2, a2 * b1 + b2


def _linear_scan(a, b, h0, reverse):
    if h0 is not None:
        if reverse:
            b = b.at[:, -1].add(a[:, -1] * h0)
        else:
            b = b.at[:, 0].add(a[:, 0] * h0)
    _, h = lax.associative_scan(_combine, (a, b), reverse=reverse, axis=1)
    return h


def _ffn_sublayer(h, shift, scale, gate, g_pre, g_post, w_gate, w_up, w_down):
    u = _modulate(_rmsnorm(h, g_pre), shift, scale)
    y = (jax.nn.silu(u @ w_gate) * (u @ w_up)) @ w_down
    return h + FFN_RES * gate * _rmsnorm(y, g_post)


def _even_mixer(u, uc, w_in, conv_w, conv_b, w_a, b_a, w_x, b_x, lam, sink, w_out, rope, with_ctx_out):
    B_, T, _ = u.shape
    Lc = uc.shape[1]
    xa, ga, q, k, v = _split(u @ w_in, EVEN_SPLITS)
    xac, gac, qc, kc, vc = _split(uc @ w_in, EVEN_SPLITS)
    xa = _centred_dwconv(xa, conv_w, conv_b)
    xac = _centred_dwconv(xac, conv_w, conv_b)
    a_cf, b_cf = _rglru_coeffs(xac, w_a[0], b_a[0], w_x[0], b_x[0], lam[0])
    a_cb, b_cb = _rglru_coeffs(xac, w_a[1], b_a[1], w_x[1], b_x[1], lam[1])
    h_cf = _linear_scan(a_cf, b_cf, None, False)
    h_cb = _linear_scan(a_cb, b_cb, None, True)
    a_f, b_f = _rglru_coeffs(xa, w_a[0], b_a[0], w_x[0], b_x[0], lam[0])
    a_b, b_b = _rglru_coeffs(xa, w_a[1], b_a[1], w_x[1], b_x[1], lam[1])
    h_f = _linear_scan(a_f, b_f, h_cf[:, -1], False)
    h_b = _linear_scan(a_b, b_b, h_cb[:, 0], True)
    y_a = (h_f + h_b).astype(u.dtype) * jax.nn.gelu(ga)
    q = _apply_rope(q.reshape(B_, T, WIN_KV, WIN_G, WIN_HD), rope)
    k = _apply_rope(k.reshape(B_, T, WIN_KV, WIN_HD), rope)
    v = v.reshape(B_, T, WIN_KV, WIN_HD)
    kc = kc.reshape(B_, Lc, WIN_KV, WIN_HD)
    vc = vc.reshape(B_, Lc, WIN_KV, WIN_HD)
    sink_kg = sink.reshape(WIN_KV, WIN_G)
    o_b = _window_attention(q, k, v, kc, vc, sink_kg)
    y = jnp.concatenate([y_a, o_b], axis=-1) @ w_out
    if not with_ctx_out:
        return y, None
    y_ac = (h_cf + h_cb).astype(uc.dtype) * jax.nn.gelu(gac)
    o_bc = _gqa_attend(qc.reshape(B_, Lc, WIN_KV, WIN_G, WIN_HD), kc, vc, None, sink_kg).reshape(B_, Lc, -1)
    yc = jnp.concatenate([y_ac, o_bc], axis=-1) @ w_out
    return y, yc


def _odd_mixer(u, uc, w_in, q_gain, k_gain, w_out, rope, with_ctx_out):
    B_, T, _ = u.shape
    Lc = uc.shape[1]
    q, k, v = _split(u @ w_in, ODD_SPLITS)
    qc, kc, vc = _split(uc @ w_in, ODD_SPLITS)
    q = _apply_rope(_rmsnorm(q.reshape(B_, T, GLB_KV, GLB_G, GLB_HD), q_gain), rope)
    k = _apply_rope(_rmsnorm(k.reshape(B_, T, GLB_KV, GLB_HD), k_gain), rope)
    v = v.reshape(B_, T, GLB_KV, GLB_HD)
    kc = _rmsnorm(kc.reshape(B_, Lc, GLB_KV, GLB_HD), k_gain)
    vc = vc.reshape(B_, Lc, GLB_KV, GLB_HD)
    k_all = jnp.concatenate([kc, k], axis=1)
    v_all = jnp.concatenate([vc, v], axis=1)
    y = _dense_attention(q, k_all, v_all) @ w_out
    if not with_ctx_out:
        return y, None
    qc = _rmsnorm(qc.reshape(B_, Lc, GLB_KV, GLB_G, GLB_HD), q_gain)
    yc = _gqa_attend(qc, kc, vc, None, None).reshape(B_, Lc, -1) @ w_out
    return y, yc


def setup_inputs(seed: int = 0) -> dict:
    key = jax.random.key(seed)
    ks = jax.random.split(key, 32)
    D = D_MODEL

    def nrm(k, shape, scale):
        return jax.random.normal(k, shape, jnp.float32) * scale

    a0 = jax.random.uniform(ks[18], (N_EVEN, 2, LRU_WIDTH), jnp.float32, 0.9, 0.999)
    sig = a0 ** (1.0 / LRU_C)
    return {
        'x': nrm(ks[0], (BATCH, SEQ, D), 1.0),
        'c': nrm(ks[1], (BATCH, D), 1.0),
        'ctx': nrm(ks[2], (BATCH, CTX_LEN, D), 1.0),
        'c_ctx': nrm(ks[3], (D,), 1.0),
        'w_ada': nrm(ks[4], (DEPTH, D, N_MOD * D), 0.5 * D ** -0.5),
        'b_ada': nrm(ks[5], (DEPTH, N_MOD * D), 0.02),
        'norm_pre': 1.0 + nrm(ks[6], (DEPTH, N_SUB, D), 0.02),
        'norm_post': 1.0 + nrm(ks[7], (DEPTH, N_SUB, D), 0.02),
        'ffn_w_gate': nrm(ks[8], (DEPTH, 2, D, D_FF), D ** -0.5),
        'ffn_w_up': nrm(ks[9], (DEPTH, 2, D, D_FF), D ** -0.5),
        'ffn_w_down': nrm(ks[10], (DEPTH, 2, D_FF, D), D_FF ** -0.5),
        'even_w_in': nrm(ks[11], (N_EVEN, D, EVEN_IN), D ** -0.5),
        'even_conv_w': nrm(ks[12], (N_EVEN, CONV_W, LRU_WIDTH), CONV_W ** -0.5),
        'even_conv_b': nrm(ks[13], (N_EVEN, LRU_WIDTH), 0.02),
        'lru_w_a': nrm(ks[14], (N_EVEN, 2, LRU_BLOCKS, LRU_BW, LRU_BW), LRU_BW ** -0.5),
        'lru_b_a': nrm(ks[15], (N_EVEN, 2, LRU_WIDTH), 0.02),
        'lru_w_x': nrm(ks[16], (N_EVEN, 2, LRU_BLOCKS, LRU_BW, LRU_BW), LRU_BW ** -0.5),
        'lru_b_x': nrm(ks[17], (N_EVEN, 2, LRU_WIDTH), 0.02),
        'lru_lambda': jnp.log(sig) - jnp.log1p(-sig),
        'attn_sink': nrm(ks[19], (N_EVEN, WIN_HEADS), 0.5),
        'even_w_out': nrm(ks[20], (N_EVEN, EVEN_OUT, D), EVEN_OUT ** -0.5),
        'odd_w_in': nrm(ks[21], (N_ODD, D, ODD_IN), D ** -0.5),
        'odd_q_norm': 1.0 + nrm(ks[22], (N_ODD, GLB_HD), 0.02),
        'odd_k_norm': 1.0 + nrm(ks[23], (N_ODD, GLB_HD), 0.02),
        'odd_w_out': nrm(ks[24], (N_ODD, ODD_OUT, D), ODD_OUT ** -0.5),
    }


def reference(x, c, ctx, c_ctx, w_ada, b_ada, norm_pre, norm_post, ffn_w_gate, ffn_w_up, ffn_w_down,
              even_w_in, even_conv_w, even_conv_b, lru_w_a, lru_b_a, lru_w_x, lru_b_x, lru_lambda,
              attn_sink, even_w_out, odd_w_in, odd_q_norm, odd_k_norm, odd_w_out):
    B_, T, D = x.shape
    rows = T // GRID_W
    pos = jnp.arange(T)
    row = jnp.repeat(jnp.arange(rows), GRID_W)
    col = pos % GRID_W
    rope_win = _axial_rope(row, col, WIN_HD)
    rope_glb = _axial_rope(row, col, GLB_HD)
    xc = ctx
    sc = jax.nn.silu(c)
    scc = jax.nn.silu(c_ctx)
    for l in range(DEPTH):
        last = l == DEPTH - 1
        mod = (sc @ w_ada[l] + b_ada[l]).reshape(B_, N_MOD, 1, D)
        mod_c = (scc @ w_ada[l] + b_ada[l]).reshape(N_MOD, 1, D)
        x = _ffn_sublayer(x, mod[:, 0], mod[:, 1], mod[:, 2], norm_pre[l, 0], norm_post[l, 0],
                          ffn_w_gate[l, 0], ffn_w_up[l, 0], ffn_w_down[l, 0])
        xc = _ffn_sublayer(xc, mod_c[0], mod_c[1], mod_c[2], norm_pre[l, 0], norm_post[l, 0],
                           ffn_w_gate[l, 0], ffn_w_up[l, 0], ffn_w_down[l, 0])
        u = _modulate(_rmsnorm(x, norm_pre[l, 1]), mod[:, 3], mod[:, 4])
        uc = _modulate(_rmsnorm(xc, norm_pre[l, 1]), mod_c[3], mod_c[4])
        i = l // 2
        if l % 2 == 0:
            y, yc = _even_mixer(u, uc, even_w_in[i], even_conv_w[i], even_conv_b[i], lru_w_a[i], lru_b_a[i],
                                lru_w_x[i], lru_b_x[i], lru_lambda[i], attn_sink[i], even_w_out[i],
                                rope_win, not last)
        else:
            y, yc = _odd_mixer(u, uc, odd_w_in[i], odd_q_norm[i], odd_k_norm[i], odd_w_out[i],
                               rope_glb, not last)
        x = x + mod[:, 5] * _rmsnorm(y, norm_post[l, 1])
        x = _ffn_sublayer(x, mod[:, 6], mod[:, 7], mod[:, 8], norm_pre[l, 2], norm_post[l, 2],
                          ffn_w_gate[l, 1], ffn_w_up[l, 1], ffn_w_down[l, 1])
        if not last:
            xc = xc + mod_c[5] * _rmsnorm(yc, norm_post[l, 1])
            xc = _ffn_sublayer(xc, mod_c[6], mod_c[7], mod_c[8], norm_pre[l, 2], norm_post[l, 2],
                               ffn_w_gate[l, 1], ffn_w_up[l, 1], ffn_w_down[l, 1])
    return x
```

```python
import functools

import jax
import jax.numpy as jnp
import numpy as np
from jax import lax
from jax.experimental import pallas as pl
from jax.experimental.pallas import tpu as pltpu

F32 = jnp.float32
BF16 = jnp.bfloat16

EPS = 1e-6
FFN_RES = 0.5
LRU_C = 8.0
WINDOW = 128
GRID_W = 64
ROPE_THETA = 10000.0
GQA_GROUP = 4
N_MOD = 9
MOD_ROWS = 16

V7X_VMEM_LIMIT = 56 * 1024 * 1024
Q_TILE_WIN = 128
LRU_CHUNK = 256
HALO = 8


def _const_spec(shape):
    zeros = (0,) * len(shape)
    return pl.BlockSpec(shape, lambda *_: zeros)


def _resident_spec(shape):
    zeros = (0,) * len(shape)
    return pl.BlockSpec(shape, lambda *_: zeros, pipeline_mode=pl.Buffered(1))


def _dot(a, b):
    return jnp.dot(a, b, preferred_element_type=F32)


def _dot_t(a, b):
    return lax.dot_general(a, b, (((1,), (1,)), ((), ())), preferred_element_type=F32)


def _rms(x, g):
    return x * lax.rsqrt(jnp.mean(x * x, axis=-1, keepdims=True) + EPS) * g


def _pre(x, g, shift, scale):
    return _rms(x, g) * (1.0 + scale) + shift


def _ada_kernel(c_ref, w_ref, b_ref, o_ref):
    c = c_ref[...]
    s = c * jax.nn.sigmoid(c)
    o_ref[...] = _dot(s.astype(BF16), w_ref[...].astype(BF16)) + b_ref[...]


def ada_mod(c_rows, w_ada, b_ada):
    depth, d, n = w_ada.shape
    tn = 1024
    return pl.pallas_call(
        _ada_kernel,
        out_shape=jax.ShapeDtypeStruct((depth, MOD_ROWS, n), F32),
        grid=(depth, n // tn),
        in_specs=[
            pl.BlockSpec((MOD_ROWS, d), lambda l, j: (0, 0)),
            pl.BlockSpec((None, d, tn), lambda l, j: (l, 0, j)),
            pl.BlockSpec((None, 1, tn), lambda l, j: (l, 0, j)),
        ],
        out_specs=pl.BlockSpec((None, MOD_ROWS, tn), lambda l, j: (l, 0, j)),
        name="ada_mod",
    )(c_rows, w_ada, b_ada.reshape(depth, 1, n))


class _Layout:
    def __init__(self, b, t, lc):
        self.b, self.t, self.lc = b, t, lc
        self.n_lat = b * t
        self.nt = b * (t + lc)

    def tile(self, candidates):
        for tm in candidates:
            if self.t % tm == 0 and (self.b * self.lc) % tm == 0:
                return tm
        raise ValueError("no token tile divides both the latent and the context stream")

    def mod_row(self, tm):
        n_lat_tiles, per_b, b = self.n_lat // tm, self.t // tm, self.b
        return lambda i: jnp.where(i < n_lat_tiles, i // per_b, b)

    def rope_block(self, tm):
        n_lat_tiles, per_b = self.n_lat // tm, self.t // tm
        return lambda i: jnp.where(i < n_lat_tiles, i % per_b, per_b)


def _mod_spec(lay, tm):
    row = lay.mod_row(tm)
    return pl.BlockSpec((None, N_MOD, lay_d(lay)), lambda i: (row(i), 0, 0))


def lay_d(lay):
    return lay.d


def _ffn_kernel(x_ref, mod_ref, gpre_ref, gpost_ref, wg_ref, wu_ref, wd_ref, o_ref, h_ref, *, m0, ff_chunk):
    x = x_ref[...]
    u = _pre(x, gpre_ref[...], mod_ref[m0:m0 + 1, :], mod_ref[m0 + 1:m0 + 2, :]).astype(BF16)
    d_ff = wg_ref.shape[1]
    for c0 in range(0, d_ff, ff_chunk):
        g = _dot(u, wg_ref[:, c0:c0 + ff_chunk])
        up = _dot(u, wu_ref[:, c0:c0 + ff_chunk])
        h_ref[:, c0:c0 + ff_chunk] = (g * jax.nn.sigmoid(g) * up).astype(BF16)
    y = _dot(h_ref[...], wd_ref[...])
    o_ref[...] = x + FFN_RES * mod_ref[m0 + 2:m0 + 3, :] * _rms(y, gpost_ref[...])


def ffn_sublayer(lay, x_all, n_rows, mod_l, m0, g_pre, g_post, wg, wu, wd):
    d = lay.d
    d_ff = wg.shape[1]
    tm = lay.tile((512, 256, 128))
    return pl.pallas_call(
        functools.partial(_ffn_kernel, m0=m0, ff_chunk=256),
        out_shape=jax.ShapeDtypeStruct((n_rows, d), F32),
        grid=(n_rows // tm,),
        in_specs=[
            pl.BlockSpec((tm, d), lambda i: (i, 0)),
            _mod_spec(lay, tm),
            _const_spec((1, d)),
            _const_spec((1, d)),
            _resident_spec((d, d_ff)),
            _resident_spec((d, d_ff)),
            _resident_spec((d_ff, d)),
        ],
        out_specs=pl.BlockSpec((tm, d), lambda i: (i, 0)),
        scratch_shapes=[pltpu.VMEM((tm, d_ff), BF16)],
        compiler_params=pltpu.CompilerParams(vmem_limit_bytes=V7X_VMEM_LIMIT),
        name="ffn_sublayer",
    )(x_all, mod_l, g_pre.reshape(1, d), g_post.reshape(1, d), wg, wu, wd)


def _even_in_kernel(x_ref, mod_ref, g_ref, w_ref, cos_ref, sin_ref, xa_ref, gg_ref, q_ref, k_ref, v_ref, *, lw, qw, kw):
    u = _pre(x_ref[...], g_ref[...], mod_ref[3:4, :], mod_ref[4:5, :]).astype(BF16)
    cos, sin = cos_ref[...], sin_ref[...]
    o = 0
    xa_ref[...] = _dot(u, w_ref[:, o:o + lw])
    o += lw
    gg_ref[...] = jax.nn.gelu(_dot(u, w_ref[:, o:o + lw]))
    o += lw
    q = _dot(u, w_ref[:, o:o + qw])
    qr = _dot(u, w_ref[:, o + qw:o + 2 * qw])
    o += 2 * qw
    reps = qw // cos.shape[1]
    q_ref[...] = (q * jnp.concatenate([cos] * reps, axis=1) + qr * jnp.concatenate([sin] * reps, axis=1)).astype(BF16)
    k = _dot(u, w_ref[:, o:o + kw])
    kr = _dot(u, w_ref[:, o + kw:o + 2 * kw])
    o += 2 * kw
    reps = kw // cos.shape[1]
    k_ref[...] = (k * jnp.concatenate([cos] * reps, axis=1) + kr * jnp.concatenate([sin] * reps, axis=1)).astype(BF16)
    v_ref[...] = _dot(u, w_ref[:, o:o + kw]).astype(BF16)


def even_in(lay, x_all, mod_l, g_pre, w_cat, cos_t, sin_t, lw, qw, kw):
    d = lay.d
    tm = lay.tile((512, 256, 128))
    rb = lay.rope_block(tm)
    nt = lay.nt
    row = lambda i: (i, 0)
    return pl.pallas_call(
        functools.partial(_even_in_kernel, lw=lw, qw=qw, kw=kw),
        out_shape=(
            jax.ShapeDtypeStruct((nt, lw), F32),
            jax.ShapeDtypeStruct((nt, lw), F32),
            jax.ShapeDtypeStruct((nt, qw), BF16),
            jax.ShapeDtypeStruct((nt, kw), BF16),
            jax.ShapeDtypeStruct((nt, kw), BF16),
        ),
        grid=(nt // tm,),
        in_specs=[
            pl.BlockSpec((tm, d), row),
            _mod_spec(lay, tm),
            _const_spec((1, d)),
            _resident_spec(w_cat.shape),
            pl.BlockSpec((tm, cos_t.shape[1]), lambda i: (rb(i), 0)),
            pl.BlockSpec((tm, cos_t.shape[1]), lambda i: (rb(i), 0)),
        ],
        out_specs=(
            pl.BlockSpec((tm, lw), row),
            pl.BlockSpec((tm, lw), row),
            pl.BlockSpec((tm, qw), row),
            pl.BlockSpec((tm, kw), row),
            pl.BlockSpec((tm, kw), row),
        ),
        compiler_params=pltpu.CompilerParams(vmem_limit_bytes=V7X_VMEM_LIMIT),
        name="even_in",
    )(x_all, mod_l, g_pre.reshape(1, d), w_cat, cos_t, sin_t)


def _shift_rows(x, dist, fill, reverse):
    n = x.shape[0]
    rows = lax.broadcasted_iota(jnp.int32, x.shape, 0)
    if reverse:
        return jnp.where(rows < n - dist, pltpu.roll(x, n - dist, 0), fill)
    return jnp.where(rows >= dist, pltpu.roll(x, dist, 0), fill)


def _lru_kernel(xm_ref, xl_ref, xr_ref, cw_ref, cb_ref, wg_ref, bg_ref, lam_ref, *rest,
                reverse, n_ctx_chunks, n_lat_chunks):
    if reverse:
        hf_ref, gg_ref, o_ref, xbuf, carry = rest
    else:
        o_ref, xbuf, carry = rest
    s = pl.program_id(1)
    cl, lw = xm_ref.shape
    in_ctx = s < n_ctx_chunks
    pos = jnp.where(in_ctx, s, s - n_ctx_chunks)
    n_seq = jnp.where(in_ctx, n_ctx_chunks, n_lat_chunks)
    chunk = (n_seq - 1 - pos) if reverse else pos

    @pl.when(s == 0)
    def _():
        carry[...] = jnp.zeros_like(carry)

    xbuf[0:HALO, :] = jnp.where(chunk > 0, xl_ref[...], 0.0)
    xbuf[HALO:HALO + cl, :] = xm_ref[...]
    xbuf[HALO + cl:, :] = jnp.where(chunk < n_seq - 1, xr_ref[...], 0.0)
    taps = cw_ref.shape[0]
    left = taps // 2
    xin = cb_ref[...] + xbuf[pl.ds(HALO - left, cl), :] * cw_ref[0:1, :]
    for t in range(1, taps):
        xin = xin + xbuf[pl.ds(HALO - left + t, cl), :] * cw_ref[t:t + 1, :]

    gates = _dot(xin.astype(BF16), wg_ref[...]) + bg_ref[...]
    r = jax.nn.sigmoid(gates[:, :lw])
    gate_i = jax.nn.sigmoid(gates[:, lw:])
    log_a = (-LRU_C) * r * jax.nn.softplus(-lam_ref[...])
    a = jnp.exp(log_a)
    b = jnp.sqrt(-jnp.tanh(log_a) * (a * a + 1.0)) * gate_i * xin

    dist = 1
    while dist < cl:
        b = b + a * _shift_rows(b, dist, 0.0, reverse)
        a = a * _shift_rows(a, dist, 1.0, reverse)
        dist *= 2
    h = b + a * carry[...]
    carry[...] = h[0:1, :] if reverse else h[cl - 1:cl, :]
    if reverse:
        o_ref[...] = ((hf_ref[...] + h) * gg_ref[...]).astype(o_ref.dtype)
    else:
        o_ref[...] = h


def lru_scan(lay, xa, conv_w, conv_b, w_gate, b_gate, lam, reverse, hf=None, gg=None):
    nt, lw = xa.shape
    cl = LRU_CHUNK
    assert lay.t % cl == 0 and lay.lc % cl == 0
    ncc, nlc = lay.lc // cl, lay.t // cl
    n_lat_blocks = lay.n_lat // cl

    def chunk_block(b, s):
        in_ctx = s < ncc
        pos = jnp.where(in_ctx, s, s - ncc)
        if reverse:
            pos = jnp.where(in_ctx, ncc - 1 - pos, nlc - 1 - pos)
        return jnp.where(in_ctx, n_lat_blocks + b * ncc + pos, b * nlc + pos)

    per = cl // HALO
    last_halo = nt // HALO - 1
    main = pl.BlockSpec((cl, lw), lambda b, s: (chunk_block(b, s), 0))
    in_specs = [
        main,
        pl.BlockSpec((HALO, lw), lambda b, s: (jnp.maximum(chunk_block(b, s) * per - 1, 0), 0)),
        pl.BlockSpec((HALO, lw), lambda b, s: (jnp.minimum((chunk_block(b, s) + 1) * per, last_halo), 0)),
        _const_spec(conv_w.shape),
        _const_spec((1, lw)),
        _const_spec(w_gate.shape),
        _const_spec((1, 2 * lw)),
        _const_spec((1, lw)),
    ]
    args = [xa, xa, xa, conv_w, conv_b.reshape(1, lw), w_gate, b_gate.reshape(1, 2 * lw), lam.reshape(1, lw)]
    if reverse:
        in_specs += [main, main]
        args += [hf, gg]
    return pl.pallas_call(
        functools.partial(_lru_kernel, reverse=reverse, n_ctx_chunks=ncc, n_lat_chunks=nlc),
        out_shape=jax.ShapeDtypeStruct((nt, lw), BF16 if reverse else F32),
        grid=(lay.b, ncc + nlc),
        in_specs=in_specs,
        out_specs=main,
        scratch_shapes=[pltpu.VMEM((cl + 2 * HALO, lw), F32), pltpu.VMEM((1, lw), F32)],
        compiler_params=pltpu.CompilerParams(dimension_semantics=("arbitrary", "arbitrary")),
        name="lru_bwd" if reverse else "lru_fwd",
    )(*args)


def _win_attn_kernel(q_ref, kp_ref, kc_ref, kn_ref, kx_ref, vp_ref, vc_ref, vn_ref, vx_ref, sink_ref, o_ref,
                     *, n_q_lat, t, n_heads, hd):
    j = pl.program_id(1)
    tq = q_ref.shape[0]
    lc = kx_ref.shape[0]
    is_ctx = j >= n_q_lat
    qpos = j * tq + lax.broadcasted_iota(jnp.int32, (tq, 3 * tq), 0)
    kpos = (j - 1) * tq + lax.broadcasted_iota(jnp.int32, (tq, 3 * tq), 1)
    dist = jnp.abs(kpos - qpos)
    bad = (dist > WINDOW).astype(jnp.int32) + (kpos < 0).astype(jnp.int32) + (kpos >= t).astype(jnp.int32)
    bad = bad + is_ctx.astype(jnp.int32)
    bias = jnp.concatenate([jnp.where(bad > 0, -jnp.inf, 0.0).astype(F32), jnp.zeros((tq, lc), F32)], axis=1)
    lane = lax.broadcasted_iota(jnp.int32, (tq, 2 * hd), 1)
    low = lane < hd
    pair_out = []
    for h in range(n_heads):
        kv = h // GQA_GROUP
        pair = h // 2
        cs = slice(2 * hd * kv, 2 * hd * (kv + 1))
        k_all = jnp.concatenate([kp_ref[:, cs], kc_ref[:, cs], kn_ref[:, cs], kx_ref[:, cs]], axis=0)
        v_all = jnp.concatenate([vp_ref[:, cs], vc_ref[:, cs], vn_ref[:, cs], vx_ref[:, cs]], axis=0)
        qp = q_ref[:, 2 * hd * pair:2 * hd * (pair + 1)]
        qm = jnp.where(low if h % 2 == 0 else jnp.logical_not(low), qp, jnp.zeros_like(qp))
        sc = _dot_t(qm, k_all) + bias
        snk = sink_ref[h:h + 1, 0:1]
        m = jnp.maximum(jnp.max(sc, axis=-1, keepdims=True), snk)
        p = jnp.exp(sc - m)
        den = jnp.sum(p, axis=-1, keepdims=True) + jnp.exp(snk - m)
        o = _dot(p.astype(BF16), v_all) / den
        if h % 2 == 0:
            o_even = o
        else:
            pair_out.append(jnp.where(low, o_even, o))
    o_ref[...] = jnp.concatenate(pair_out, axis=1).astype(o_ref.dtype)


def win_attn(lay, q, kd, vd, sink_rows, n_heads, hd):
    nt, qw = q.shape
    kw = kd.shape[1]
    tq = Q_TILE_WIN
    assert tq == WINDOW and lay.t % tq == 0 and lay.lc % tq == 0 and lay.n_lat % lay.lc == 0
    nb, ncq = lay.t // tq, lay.lc // tq
    ctx_q0 = lay.n_lat // tq
    ctx_k0 = lay.n_lat // lay.lc

    def qblock(b, j):
        return jnp.where(j < nb, b * nb + j, ctx_q0 + b * ncq + (j - nb))

    def band(off):
        def index(b, j):
            jj = jnp.clip(jnp.minimum(j, nb - 1) + off, 0, nb - 1)
            return (b * nb + jj, 0)
        return pl.BlockSpec((tq, kw), index)

    ctx = pl.BlockSpec((lay.lc, kw), lambda b, j: (ctx_k0 + b, 0))
    qspec = pl.BlockSpec((tq, qw), lambda b, j: (qblock(b, j), 0))
    return pl.pallas_call(
        functools.partial(_win_attn_kernel, n_q_lat=nb, t=lay.t, n_heads=n_heads, hd=hd),
        out_shape=jax.ShapeDtypeStruct((nt, qw), BF16),
        grid=(lay.b, nb + ncq),
        in_specs=[qspec, band(-1), band(0), band(1), ctx, band(-1), band(0), band(1), ctx,
                  _const_spec(sink_rows.shape)],
        out_specs=qspec,
        name="win_attn",
    )(q, kd, kd, kd, kd, vd, vd, vd, vd, sink_rows)


def _mix_out_kernel(x_ref, mod_ref, g_ref, w_ref, *rest):
    o_ref = rest[-1]
    parts = [r[...] for r in rest[:-1]]
    y = _dot(jnp.concatenate(parts, axis=1) if len(parts) > 1 else parts[0], w_ref[...])
    o_ref[...] = x_ref[...] + mod_ref[5:6, :] * _rms(y, g_ref[...])


def mix_out(lay, x_all, n_rows, mod_l, g_post, w_out, parts):
    d = lay.d
    tm = lay.tile((512, 256, 128))
    row = lambda i: (i, 0)
    return pl.pallas_call(
        _mix_out_kernel,
        out_shape=jax.ShapeDtypeStruct((n_rows, d), F32),
        grid=(n_rows // tm,),
        in_specs=[pl.BlockSpec((tm, d), row), _mod_spec(lay, tm), _const_spec((1, d)), _resident_spec(w_out.shape)]
        + [pl.BlockSpec((tm, p.shape[1]), row) for p in parts],
        out_specs=pl.BlockSpec((tm, d), row),
        name="mix_out",
    )(x_all, mod_l, g_post.reshape(1, d), w_out, *parts)


def _odd_in_kernel(x_ref, mod_ref, g_ref, w_ref, cos_ref, sin_ref, gq_ref, gk_ref, q_ref, k_ref, v_ref, *, qw, kw, hd):
    u = _pre(x_ref[...], g_ref[...], mod_ref[3:4, :], mod_ref[4:5, :]).astype(BF16)
    cos, sin = cos_ref[...], sin_ref[...]

    def normed_rope(o, width, gains, out_ref):
        cg = cos * gains[0:1, :]
        sg = sin * gains[1:2, :]
        for h0 in range(0, width, hd):
            x = _dot(u, w_ref[:, o + h0:o + h0 + hd])
            xr = _dot(u, w_ref[:, o + width + h0:o + width + h0 + hd])
            r = lax.rsqrt(jnp.mean(x * x, axis=-1, keepdims=True) + EPS)
            out_ref[:, h0:h0 + hd] = (r * (x * cg + xr * sg)).astype(out_ref.dtype)

    normed_rope(0, qw, gq_ref[...], q_ref)
    normed_rope(2 * qw, kw, gk_ref[...], k_ref)
    o = 2 * qw + 2 * kw
    v_ref[...] = _dot(u, w_ref[:, o:o + kw]).astype(BF16)


def odd_in(lay, x_all, mod_l, g_pre, w_cat, cos_t, sin_t, gq, gk, qw, kw, hd):
    d = lay.d
    tm = lay.tile((512, 256, 128))
    rb = lay.rope_block(tm)
    nt = lay.nt
    row = lambda i: (i, 0)
    return pl.pallas_call(
        functools.partial(_odd_in_kernel, qw=qw, kw=kw, hd=hd),
        out_shape=(
            jax.ShapeDtypeStruct((nt, qw), BF16),
            jax.ShapeDtypeStruct((nt, kw), BF16),
            jax.ShapeDtypeStruct((nt, kw), BF16),
        ),
        grid=(nt // tm,),
        in_specs=[
            pl.BlockSpec((tm, d), row),
            _mod_spec(lay, tm),
            _const_spec((1, d)),
            _resident_spec(w_cat.shape),
            pl.BlockSpec((tm, hd), lambda i: (rb(i), 0)),
            pl.BlockSpec((tm, hd), lambda i: (rb(i), 0)),
            _const_spec(gq.shape),
            _const_spec(gk.shape),
        ],
        out_specs=(pl.BlockSpec((tm, qw), row), pl.BlockSpec((tm, kw), row), pl.BlockSpec((tm, kw), row)),
        compiler_params=pltpu.CompilerParams(vmem_limit_bytes=V7X_VMEM_LIMIT),
        name="odd_in",
    )(x_all, mod_l, g_pre.reshape(1, d), w_cat, cos_t, sin_t, gq, gk)


def _dense_attn_kernel(q_ref, kl_ref, vl_ref, kx_ref, vx_ref, o_ref, m_ref, l_ref, acc_ref, *, n_q_lat, tk, scale):
    j = pl.program_id(2)
    tq = q_ref.shape[0]
    hd = kl_ref.shape[1]
    qs = jnp.concatenate([q_ref[:, g * hd:(g + 1) * hd] for g in range(GQA_GROUP)], axis=0)

    s = _dot_t(qs, kx_ref[...]) * scale
    m0 = jnp.max(s, axis=-1, keepdims=True)
    p = jnp.exp(s - m0)
    m_ref[...] = m0
    l_ref[...] = jnp.sum(p, axis=-1, keepdims=True)
    acc_ref[...] = _dot(p.astype(BF16), vx_ref[...])

    def body(i, c):
        r0 = pl.multiple_of(i * tk, tk)
        s = _dot_t(qs, kl_ref[pl.ds(r0, tk), :]) * scale
        m_prev = m_ref[...]
        m_new = jnp.maximum(m_prev, jnp.max(s, axis=-1, keepdims=True))
        alpha = jnp.exp(m_prev - m_new)
        p = jnp.exp(s - m_new)
        l_ref[...] = alpha * l_ref[...] + jnp.sum(p, axis=-1, keepdims=True)
        acc_ref[...] = alpha * acc_ref[...] + _dot(p.astype(BF16), vl_ref[pl.ds(r0, tk), :])
        m_ref[...] = m_new
        return c

    lax.fori_loop(0, jnp.where(j < n_q_lat, kl_ref.shape[0] // tk, 0), body, 0)
    o = acc_ref[...] / l_ref[...]
    o_ref[...] = jnp.concatenate([o[g * tq:(g + 1) * tq, :] for g in range(GQA_GROUP)], axis=1).astype(o_ref.dtype)


def dense_attn(lay, q, k, v, hd, with_ctx_queries):
    nt, qw = q.shape
    tq = 256
    tk = 512 if lay.t % 512 == 0 else lay.t
    assert lay.t % tq == 0 and lay.lc % tq == 0 and lay.n_lat % lay.lc == 0
    n_kv = k.shape[1] // hd
    gw = GQA_GROUP * hd
    nql, nqc = lay.t // tq, lay.lc // tq
    ctx_q0 = lay.n_lat // tq
    ctx_k0 = lay.n_lat // lay.lc
    n_rows = nt if with_ctx_queries else lay.n_lat

    def qblock(b, kv, j):
        return (jnp.where(j < nql, b * nql + j, ctx_q0 + b * nqc + (j - nql)), kv)

    lat = pl.BlockSpec((lay.t, hd), lambda b, kv, j: (b, kv))
    ctx = pl.BlockSpec((lay.lc, hd), lambda b, kv, j: (ctx_k0 + b, kv))
    qspec = pl.BlockSpec((tq, gw), qblock)
    m_rows = GQA_GROUP * tq
    return pl.pallas_call(
        functools.partial(_dense_attn_kernel, n_q_lat=nql, tk=tk, scale=hd ** -0.5),
        out_shape=jax.ShapeDtypeStruct((n_rows, qw), BF16),
        grid=(lay.b, n_kv, nql + (nqc if with_ctx_queries else 0)),
        in_specs=[qspec, lat, lat, ctx, ctx],
        out_specs=qspec,
        scratch_shapes=[pltpu.VMEM((m_rows, 1), F32), pltpu.VMEM((m_rows, 1), F32), pltpu.VMEM((m_rows, hd), F32)],
        compiler_params=pltpu.CompilerParams(vmem_limit_bytes=V7X_VMEM_LIMIT),
        name="dense_attn",
    )(q, k, v, k, v)


def _rot_cols(w):
    pairs = w.reshape(w.shape[0], -1, 2)
    return jnp.stack([-pairs[..., 1], pairs[..., 0]], axis=-1).reshape(w.shape)


def _dup_heads(w, hd):
    heads = w.reshape(w.shape[0], -1, 1, hd)
    return jnp.concatenate([heads, heads], axis=2).reshape(w.shape[0], -1)


def _block_diag(w):
    h, n, _ = w.shape
    eye = jnp.eye(h, dtype=w.dtype)
    return (eye[:, None, :, None] * w[:, :, None, :]).reshape(h * n, h * n)


def _rope_tables(t, hd, lanes, ident_rows):
    n_freq = hd // 4
    freq = ROPE_THETA ** (-jnp.arange(n_freq, dtype=F32) / n_freq)
    pos = jnp.arange(t)
    row = (pos // GRID_W).astype(F32)
    col = (pos % GRID_W).astype(F32)
    ang = jnp.concatenate([row[:, None] * freq, col[:, None] * freq], axis=-1)
    cos = jnp.repeat(jnp.cos(ang), 2, axis=-1)
    sin = jnp.repeat(jnp.sin(ang), 2, axis=-1)
    cos = jnp.concatenate([cos, jnp.ones((ident_rows, hd), F32)], axis=0)
    sin = jnp.concatenate([sin, jnp.zeros((ident_rows, hd), F32)], axis=0)
    reps = lanes // hd
    return jnp.tile(cos, (1, reps)), jnp.tile(sin, (1, reps))


def _swap_pairs(g):
    return g.reshape(-1, 2)[:, ::-1].reshape(g.shape)


def kernel(x, c, ctx, c_ctx, w_ada, b_ada, norm_pre, norm_post, ffn_w_gate, ffn_w_up, ffn_w_down, even_w_in,
           even_conv_w, even_conv_b, lru_w_a, lru_b_a, lru_w_x, lru_b_x, lru_lambda, attn_sink, even_w_out, odd_w_in,
           odd_q_norm, odd_k_norm, odd_w_out):
    b, t, d = x.shape
    lc = ctx.shape[1]
    depth = w_ada.shape[0]
    lay = _Layout(b, t, lc)
    lay.d = d
    assert b + 1 <= MOD_ROWS and w_ada.shape[2] == N_MOD * d

    lw = even_conv_w.shape[2]
    win_heads = attn_sink.shape[1]
    win_hd = (d - lw) // win_heads
    win_kv = win_heads // GQA_GROUP
    glb_hd = odd_q_norm.shape[1]
    glb_heads = d // glb_hd
    glb_kv = glb_heads // GQA_GROUP
    qw_e, kw_e = win_heads * win_hd, win_kv * win_hd
    qw_o, kw_o = glb_heads * glb_hd, glb_kv * glb_hd

    x_all = jnp.concatenate([x.reshape(b * t, d), ctx.reshape(b * lc, d)], axis=0)
    c_rows = jnp.concatenate([c, c_ctx[None, :], jnp.zeros((MOD_ROWS - b - 1, d), F32)], axis=0)
    mod = ada_mod(c_rows, w_ada, b_ada).reshape(depth, MOD_ROWS, N_MOD, d)

    tm = lay.tile((512, 256, 128))
    cos_e, sin_e = _rope_tables(t, win_hd, 2 * win_hd, tm)
    cos_o, sin_o = _rope_tables(t, glb_hd, glb_hd, tm)

    wg16, wu16, wd16 = ffn_w_gate.astype(BF16), ffn_w_up.astype(BF16), ffn_w_down.astype(BF16)

    for l in range(depth):
        last = l == depth - 1
        i = l // 2
        mod_l = mod[l]
        x_all = ffn_sublayer(lay, x_all, lay.nt, mod_l, 0, norm_pre[l, 0], norm_post[l, 0], wg16[l, 0], wu16[l, 0], wd16[l, 0])
        rows_out = lay.n_lat if last else lay.nt
        if l % 2 == 0:
            w = even_w_in[i]
            o1, o2, o3, o4 = 2 * lw, 2 * lw + qw_e, 2 * lw + qw_e + kw_e, 2 * lw + qw_e + 2 * kw_e
            wq = w[:, o1:o2] * (win_hd ** -0.5)
            wk = w[:, o2:o3]
            w_cat = jnp.concatenate([
                w[:, :o1], wq, _rot_cols(wq), _dup_heads(wk, win_hd), _dup_heads(_rot_cols(wk), win_hd),
                _dup_heads(w[:, o3:o4], win_hd)], axis=1).astype(BF16)
            xa, gg, q, kd, vd = even_in(lay, x_all, mod_l, norm_pre[l, 1], w_cat, cos_e, sin_e, lw, qw_e, 2 * kw_e)
            spl_args = []
            for direction in range(2):
                w_gate = jnp.concatenate([_block_diag(lru_w_a[i, direction]), _block_diag(lru_w_x[i, direction])], axis=1).astype(BF16)
                b_gate = jnp.concatenate([lru_b_a[i, direction], lru_b_x[i, direction]])
                spl_args.append((w_gate, b_gate, lru_lambda[i, direction]))
            hf = lru_scan(lay, xa, even_conv_w[i], even_conv_b[i], *spl_args[0], reverse=False)
            ya = lru_scan(lay, xa, even_conv_w[i], even_conv_b[i], *spl_args[1], reverse=True, hf=hf, gg=gg)
            sink_rows = jnp.broadcast_to(attn_sink[i][:, None], (win_heads, 128)).astype(F32)
            ob = win_attn(lay, q, kd, vd, sink_rows, win_heads, win_hd)
            x_all = mix_out(lay, x_all, rows_out, mod_l, norm_post[l, 1], even_w_out[i].astype(BF16), [ya, ob])
        else:
            w = odd_w_in[i]
            wq, wk, wv = w[:, :qw_o], w[:, qw_o:qw_o + kw_o], w[:, qw_o + kw_o:]
            w_cat = jnp.concatenate([wq, _rot_cols(wq), wk, _rot_cols(wk), wv], axis=1).astype(BF16)
            gq = jnp.stack([odd_q_norm[i], _swap_pairs(odd_q_norm[i])])
            gk = jnp.stack([odd_k_norm[i], _swap_pairs(odd_k_norm[i])])
            q, k, v = odd_in(lay, x_all, mod_l, norm_pre[l, 1], w_cat, cos_o, sin_o, gq, gk, qw_o, kw_o, glb_hd)
            o = dense_attn(lay, q, k, v, glb_hd, with_ctx_queries=not last)
            x_all = mix_out(lay, x_all, rows_out, mod_l, norm_post[l, 1], odd_w_out[i].astype(BF16), [o])
        x_all = ffn_sublayer(lay, x_all, rows_out, mod_l, 6, norm_pre[l, 2], norm_post[l, 2], wg16[l, 1], wu16[l, 1], wd16[l, 1])
    return x_all[:b * t].reshape(b, t, d)
```

```python
import functools

import jax
import jax.numpy as jnp
import numpy as np
from jax import lax
from jax.experimental import pallas as pl
from jax.experimental.pallas import tpu as pltpu

F32 = jnp.float32
BF16 = jnp.bfloat16

EPS = 1e-6
FFN_RES = 0.5
LRU_C = 8.0
WINDOW = 128
GRID_W = 64
ROPE_THETA = 10000.0
GQA_GROUP = 4
N_MOD = 9
MOD_ROWS = 16

V7X_VMEM_LIMIT = 56 * 1024 * 1024
Q_TILE_WIN = 128
LRU_CHUNK = 256
HALO = 8
KEY_CHUNK = 256
LOG2_E = 1.4426950408889634


def _const_spec(shape):
    zeros = (0,) * len(shape)
    return pl.BlockSpec(shape, lambda *_: zeros)


def _resident_spec(shape):
    zeros = (0,) * len(shape)
    return pl.BlockSpec(shape, lambda *_: zeros, pipeline_mode=pl.Buffered(1))


def _dot(a, b):
    return jnp.dot(a, b, preferred_element_type=F32)


def _dot_t(a, b):
    return lax.dot_general(a, b, (((1,), (1,)), ((), ())), preferred_element_type=F32)


def _rms(x, g):
    return x * lax.rsqrt(jnp.mean(x * x, axis=-1, keepdims=True) + EPS) * g


def _pre(x, g, shift, scale):
    return _rms(x, g) * (1.0 + scale) + shift


def _ada_kernel(c_ref, w_ref, b_ref, o_ref):
    c = c_ref[...]
    s = c * jax.nn.sigmoid(c)
    o_ref[...] = _dot(s.astype(BF16), w_ref[...].astype(BF16)) + b_ref[...]


def ada_mod(c_rows, w_ada, b_ada):
    depth, d, n = w_ada.shape
    tn = 1024
    return pl.pallas_call(
        _ada_kernel,
        out_shape=jax.ShapeDtypeStruct((depth, MOD_ROWS, n), F32),
        grid=(depth, n // tn),
        in_specs=[
            pl.BlockSpec((MOD_ROWS, d), lambda l, j: (0, 0)),
            pl.BlockSpec((None, d, tn), lambda l, j: (l, 0, j)),
            pl.BlockSpec((None, 1, tn), lambda l, j: (l, 0, j)),
        ],
        out_specs=pl.BlockSpec((None, MOD_ROWS, tn), lambda l, j: (l, 0, j)),
        name="ada_mod",
    )(c_rows, w_ada, b_ada.reshape(depth, 1, n))


class _Layout:
    def __init__(self, b, t, lc):
        self.b, self.t, self.lc = b, t, lc
        self.n_lat = b * t
        self.nt = b * (t + lc)

    def tile(self, candidates):
        for tm in candidates:
            if self.t % tm == 0 and (self.b * self.lc) % tm == 0:
                return tm
        raise ValueError("no token tile divides both the latent and the context stream")

    def mod_row(self, tm):
        n_lat_tiles, per_b, b = self.n_lat // tm, self.t // tm, self.b
        return lambda i: jnp.where(i < n_lat_tiles, i // per_b, b)

    def rope_block(self, tm):
        n_lat_tiles, per_b = self.n_lat // tm, self.t // tm
        return lambda i: jnp.where(i < n_lat_tiles, i % per_b, per_b)


def _mod_spec(lay, tm):
    row = lay.mod_row(tm)
    return pl.BlockSpec((None, N_MOD, lay_d(lay)), lambda i: (row(i), 0, 0))


def lay_d(lay):
    return lay.d


def _ffn_kernel(x_ref, mod_ref, gpre_ref, gpost_ref, wg_ref, wu_ref, wd_ref, o_ref, h_ref, *, m0, ff_chunk):
    x = x_ref[...]
    u = _pre(x, gpre_ref[...], mod_ref[m0:m0 + 1, :], mod_ref[m0 + 1:m0 + 2, :]).astype(BF16)
    d_ff = wg_ref.shape[1]
    for c0 in range(0, d_ff, ff_chunk):
        g = _dot(u, wg_ref[:, c0:c0 + ff_chunk])
        up = _dot(u, wu_ref[:, c0:c0 + ff_chunk])
        h_ref[:, c0:c0 + ff_chunk] = (g * jax.nn.sigmoid(g) * up).astype(BF16)
    y = _dot(h_ref[...], wd_ref[...])
    o_ref[...] = x + FFN_RES * mod_ref[m0 + 2:m0 + 3, :] * _rms(y, gpost_ref[...])


def ffn_sublayer(lay, x_all, n_rows, mod_l, m0, g_pre, g_post, wg, wu, wd):
    d = lay.d
    d_ff = wg.shape[1]
    tm = lay.tile((512, 256, 128))
    return pl.pallas_call(
        functools.partial(_ffn_kernel, m0=m0, ff_chunk=256),
        out_shape=jax.ShapeDtypeStruct((n_rows, d), F32),
        grid=(n_rows // tm,),
        in_specs=[
            pl.BlockSpec((tm, d), lambda i: (i, 0)),
            _mod_spec(lay, tm),
            _const_spec((1, d)),
            _const_spec((1, d)),
            _resident_spec((d, d_ff)),
            _resident_spec((d, d_ff)),
            _resident_spec((d_ff, d)),
        ],
        out_specs=pl.BlockSpec((tm, d), lambda i: (i, 0)),
        scratch_shapes=[pltpu.VMEM((tm, d_ff), BF16)],
        compiler_params=pltpu.CompilerParams(vmem_limit_bytes=V7X_VMEM_LIMIT),
        name="ffn_sublayer",
    )(x_all, mod_l, g_pre.reshape(1, d), g_post.reshape(1, d), wg, wu, wd)


def _even_in_kernel(x_ref, mod_ref, g_ref, w_ref, cos_ref, sin_ref, xa_ref, gg_ref, q_ref, k_ref, v_ref, *, lw, qw, kw):
    u = _pre(x_ref[...], g_ref[...], mod_ref[3:4, :], mod_ref[4:5, :]).astype(BF16)
    cos, sin = cos_ref[...], sin_ref[...]
    o = 0
    xa_ref[...] = _dot(u, w_ref[:, o:o + lw])
    o += lw
    gg_ref[...] = jax.nn.gelu(_dot(u, w_ref[:, o:o + lw]))
    o += lw
    q = _dot(u, w_ref[:, o:o + qw])
    qr = _dot(u, w_ref[:, o + qw:o + 2 * qw])
    o += 2 * qw
    reps = qw // cos.shape[1]
    q_ref[...] = (q * jnp.concatenate([cos] * reps, axis=1) + qr * jnp.concatenate([sin] * reps, axis=1)).astype(BF16)
    k = _dot(u, w_ref[:, o:o + kw])
    kr = _dot(u, w_ref[:, o + kw:o + 2 * kw])
    o += 2 * kw
    reps = kw // cos.shape[1]
    k_ref[...] = (k * jnp.concatenate([cos] * reps, axis=1) + kr * jnp.concatenate([sin] * reps, axis=1)).astype(BF16)
    v_ref[...] = _dot(u, w_ref[:, o:o + kw]).astype(BF16)


def even_in(lay, x_all, mod_l, g_pre, w_cat, cos_t, sin_t, lw, qw, kw):
    d = lay.d
    tm = lay.tile((512, 256, 128))
    rb = lay.rope_block(tm)
    nt = lay.nt
    row = lambda i: (i, 0)
    return pl.pallas_call(
        functools.partial(_even_in_kernel, lw=lw, qw=qw, kw=kw),
        out_shape=(
            jax.ShapeDtypeStruct((nt, lw), F32),
            jax.ShapeDtypeStruct((nt, lw), F32),
            jax.ShapeDtypeStruct((nt, qw), BF16),
            jax.ShapeDtypeStruct((nt, kw), BF16),
            jax.ShapeDtypeStruct((nt, kw), BF16),
        ),
        grid=(nt // tm,),
        in_specs=[
            pl.BlockSpec((tm, d), row),
            _mod_spec(lay, tm),
            _const_spec((1, d)),
            _resident_spec(w_cat.shape),
            pl.BlockSpec((tm, cos_t.shape[1]), lambda i: (rb(i), 0)),
            pl.BlockSpec((tm, cos_t.shape[1]), lambda i: (rb(i), 0)),
        ],
        out_specs=(
            pl.BlockSpec((tm, lw), row),
            pl.BlockSpec((tm, lw), row),
            pl.BlockSpec((tm, qw), row),
            pl.BlockSpec((tm, kw), row),
            pl.BlockSpec((tm, kw), row),
        ),
        compiler_params=pltpu.CompilerParams(vmem_limit_bytes=V7X_VMEM_LIMIT),
        name="even_in",
    )(x_all, mod_l, g_pre.reshape(1, d), w_cat, cos_t, sin_t)


def _shift_rows(x, dist, fill, reverse):
    n = x.shape[0]
    rows = lax.broadcasted_iota(jnp.int32, x.shape, 0)
    if reverse:
        return jnp.where(rows < n - dist, pltpu.roll(x, n - dist, 0), fill)
    return jnp.where(rows >= dist, pltpu.roll(x, dist, 0), fill)


def _lru_kernel(xm_ref, xl_ref, xr_ref, cw_ref, cb_ref, wg_ref, bg_ref, lam_ref, *rest,
                reverse, n_ctx_chunks, n_lat_chunks):
    if reverse:
        hf_ref, gg_ref, o_ref, xbuf, carry = rest
    else:
        o_ref, xbuf, carry = rest
    s = pl.program_id(1)
    cl, lw = xm_ref.shape
    in_ctx = s < n_ctx_chunks
    pos = jnp.where(in_ctx, s, s - n_ctx_chunks)
    n_seq = jnp.where(in_ctx, n_ctx_chunks, n_lat_chunks)
    chunk = (n_seq - 1 - pos) if reverse else pos

    @pl.when(s == 0)
    def _():
        carry[...] = jnp.zeros_like(carry)

    xbuf[0:HALO, :] = jnp.where(chunk > 0, xl_ref[...], 0.0)
    xbuf[HALO:HALO + cl, :] = xm_ref[...]
    xbuf[HALO + cl:, :] = jnp.where(chunk < n_seq - 1, xr_ref[...], 0.0)
    taps = cw_ref.shape[0]
    left = taps // 2
    xin = cb_ref[...] + xbuf[pl.ds(HALO - left, cl), :] * cw_ref[0:1, :]
    for t in range(1, taps):
        xin = xin + xbuf[pl.ds(HALO - left + t, cl), :] * cw_ref[t:t + 1, :]

    gates = _dot(xin.astype(BF16), wg_ref[...]) + bg_ref[...]
    r = jax.nn.sigmoid(gates[:, :lw])
    gate_i = jax.nn.sigmoid(gates[:, lw:])
    log_a = (-LRU_C) * r * jax.nn.softplus(-lam_ref[...])
    a = jnp.exp(log_a)
    b = jnp.sqrt(-jnp.tanh(log_a) * (a * a + 1.0)) * gate_i * xin

    dist = 1
    while dist < cl:
        b = b + a * _shift_rows(b, dist, 0.0, reverse)
        a = a * _shift_rows(a, dist, 1.0, reverse)
        dist *= 2
    h = b + a * carry[...]
    carry[...] = h[0:1, :] if reverse else h[cl - 1:cl, :]
    if reverse:
        o_ref[...] = ((hf_ref[...] + h) * gg_ref[...]).astype(o_ref.dtype)
    else:
        o_ref[...] = h


def lru_scan(lay, xa, conv_w, conv_b, w_gate, b_gate, lam, reverse, hf=None, gg=None):
    nt, lw = xa.shape
    cl = LRU_CHUNK
    assert lay.t % cl == 0 and lay.lc % cl == 0
    ncc, nlc = lay.lc // cl, lay.t // cl
    n_lat_blocks = lay.n_lat // cl

    def chunk_block(b, s):
        in_ctx = s < ncc
        pos = jnp.where(in_ctx, s, s - ncc)
        if reverse:
            pos = jnp.where(in_ctx, ncc - 1 - pos, nlc - 1 - pos)
        return jnp.where(in_ctx, n_lat_blocks + b * ncc + pos, b * nlc + pos)

    per = cl // HALO
    last_halo = nt // HALO - 1
    main = pl.BlockSpec((cl, lw), lambda b, s: (chunk_block(b, s), 0))
    in_specs = [
        main,
        pl.BlockSpec((HALO, lw), lambda b, s: (jnp.maximum(chunk_block(b, s) * per - 1, 0), 0)),
        pl.BlockSpec((HALO, lw), lambda b, s: (jnp.minimum((chunk_block(b, s) + 1) * per, last_halo), 0)),
        _const_spec(conv_w.shape),
        _const_spec((1, lw)),
        _const_spec(w_gate.shape),
        _const_spec((1, 2 * lw)),
        _const_spec((1, lw)),
    ]
    args = [xa, xa, xa, conv_w, conv_b.reshape(1, lw), w_gate, b_gate.reshape(1, 2 * lw), lam.reshape(1, lw)]
    if reverse:
        in_specs += [main, main]
        args += [hf, gg]
    return pl.pallas_call(
        functools.partial(_lru_kernel, reverse=reverse, n_ctx_chunks=ncc, n_lat_chunks=nlc),
        out_shape=jax.ShapeDtypeStruct((nt, lw), BF16 if reverse else F32),
        grid=(lay.b, ncc + nlc),
        in_specs=in_specs,
        out_specs=main,
        scratch_shapes=[pltpu.VMEM((cl + 2 * HALO, lw), F32), pltpu.VMEM((1, lw), F32)],
        compiler_params=pltpu.CompilerParams(dimension_semantics=("arbitrary", "arbitrary")),
        name="lru_bwd" if reverse else "lru_fwd",
    )(*args)


def _win_attn_kernel(q_ref, kp_ref, kc_ref, kn_ref, kx_ref, vp_ref, vc_ref, vn_ref, vx_ref, sink_ref, o_ref,
                     *, n_q_lat, t, n_heads, hd):
    j = pl.program_id(1)
    tq = q_ref.shape[0]
    lc = kx_ref.shape[0]
    is_ctx = j >= n_q_lat
    qpos = j * tq + lax.broadcasted_iota(jnp.int32, (tq, 3 * tq), 0)
    kpos = (j - 1) * tq + lax.broadcasted_iota(jnp.int32, (tq, 3 * tq), 1)
    dist = jnp.abs(kpos - qpos)
    bad = (dist > WINDOW).astype(jnp.int32) + (kpos < 0).astype(jnp.int32) + (kpos >= t).astype(jnp.int32)
    bad = bad + is_ctx.astype(jnp.int32)
    bias = jnp.concatenate([jnp.where(bad > 0, -jnp.inf, 0.0).astype(F32), jnp.zeros((tq, lc), F32)], axis=1)
    lane = lax.broadcasted_iota(jnp.int32, (tq, 2 * hd), 1)
    low = lane < hd
    pair_out = []
    for h in range(n_heads):
        kv = h // GQA_GROUP
        pair = h // 2
        cs = slice(2 * hd * kv, 2 * hd * (kv + 1))
        k_all = jnp.concatenate([kp_ref[:, cs], kc_ref[:, cs], kn_ref[:, cs], kx_ref[:, cs]], axis=0)
        v_all = jnp.concatenate([vp_ref[:, cs], vc_ref[:, cs], vn_ref[:, cs], vx_ref[:, cs]], axis=0)
        qp = q_ref[:, 2 * hd * pair:2 * hd * (pair + 1)]
        qm = jnp.where(low if h % 2 == 0 else jnp.logical_not(low), qp, jnp.zeros_like(qp))
        sc = _dot_t(qm, k_all) + bias
        snk = sink_ref[h:h + 1, 0:1]
        m = jnp.maximum(jnp.max(sc, axis=-1, keepdims=True), snk)
        p = jnp.exp(sc - m)
        den = jnp.sum(p, axis=-1, keepdims=True) + jnp.exp(snk - m)
        o = _dot(p.astype(BF16), v_all) / den
        if h % 2 == 0:
            o_even = o
        else:
            pair_out.append(jnp.where(low, o_even, o))
    o_ref[...] = jnp.concatenate(pair_out, axis=1).astype(o_ref.dtype)


def win_attn(lay, q, kd, vd, sink_rows, n_heads, hd):
    nt, qw = q.shape
    kw = kd.shape[1]
    tq = Q_TILE_WIN
    assert tq == WINDOW and lay.t % tq == 0 and lay.lc % tq == 0 and lay.n_lat % lay.lc == 0
    nb, ncq = lay.t // tq, lay.lc // tq
    ctx_q0 = lay.n_lat // tq
    ctx_k0 = lay.n_lat // lay.lc

    def qblock(b, j):
        return jnp.where(j < nb, b * nb + j, ctx_q0 + b * ncq + (j - nb))

    def band(off):
        def index(b, j):
            jj = jnp.clip(jnp.minimum(j, nb - 1) + off, 0, nb - 1)
            return (b * nb + jj, 0)
        return pl.BlockSpec((tq, kw), index)

    ctx = pl.BlockSpec((lay.lc, kw), lambda b, j: (ctx_k0 + b, 0))
    qspec = pl.BlockSpec((tq, qw), lambda b, j: (qblock(b, j), 0))
    return pl.pallas_call(
        functools.partial(_win_attn_kernel, n_q_lat=nb, t=lay.t, n_heads=n_heads, hd=hd),
        out_shape=jax.ShapeDtypeStruct((nt, qw), BF16),
        grid=(lay.b, nb + ncq),
        in_specs=[qspec, band(-1), band(0), band(1), ctx, band(-1), band(0), band(1), ctx,
                  _const_spec(sink_rows.shape)],
        out_specs=qspec,
        name="win_attn",
    )(q, kd, kd, kd, kd, vd, vd, vd, vd, sink_rows)


def _mix_out_kernel(x_ref, mod_ref, g_ref, w_ref, *rest):
    o_ref = rest[-1]
    parts = [r[...] for r in rest[:-1]]
    y = _dot(jnp.concatenate(parts, axis=1) if len(parts) > 1 else parts[0], w_ref[...])
    o_ref[...] = x_ref[...] + mod_ref[5:6, :] * _rms(y, g_ref[...])


def mix_out(lay, x_all, n_rows, mod_l, g_post, w_out, parts):
    d = lay.d
    tm = lay.tile((512, 256, 128))
    row = lambda i: (i, 0)
    return pl.pallas_call(
        _mix_out_kernel,
        out_shape=jax.ShapeDtypeStruct((n_rows, d), F32),
        grid=(n_rows // tm,),
        in_specs=[pl.BlockSpec((tm, d), row), _mod_spec(lay, tm), _const_spec((1, d)), _resident_spec(w_out.shape)]
        + [pl.BlockSpec((tm, p.shape[1]), row) for p in parts],
        out_specs=pl.BlockSpec((tm, d), row),
        name="mix_out",
    )(x_all, mod_l, g_post.reshape(1, d), w_out, *parts)


def _odd_in_kernel(x_ref, mod_ref, g_ref, w_ref, cos_ref, sin_ref, gq_ref, gk_ref, q_ref, k_ref, vt_ref, *, qw, kw, hd):
    u = _pre(x_ref[...], g_ref[...], mod_ref[3:4, :], mod_ref[4:5, :]).astype(BF16)
    cos, sin = cos_ref[...], sin_ref[...]

    def normed_rope(o, width, gains, out_ref, post):
        cg = cos * gains[0:1, :]
        sg = sin * gains[1:2, :]
        for h0 in range(0, width, hd):
            x = _dot(u, w_ref[:, o + h0:o + h0 + hd])
            xr = _dot(u, w_ref[:, o + width + h0:o + width + h0 + hd])
            r = lax.rsqrt(jnp.mean(x * x, axis=-1, keepdims=True) + EPS) * post
            out_ref[:, h0:h0 + hd] = (r * (x * cg + xr * sg)).astype(out_ref.dtype)

    normed_rope(0, qw, gq_ref[...], q_ref, hd ** -0.5 * LOG2_E)
    normed_rope(2 * qw, kw, gk_ref[...], k_ref, 1.0)
    o = 2 * qw + 2 * kw
    v = _dot(u, w_ref[:, o:o + kw])
    ch = vt_ref.shape[2]
    for c in range(vt_ref.shape[0]):
        vt_ref[c] = v[c * ch:(c + 1) * ch, :].T.astype(BF16)


def odd_in(lay, x_all, mod_l, g_pre, w_cat, cos_t, sin_t, gq, gk, qw, kw, hd):
    d = lay.d
    tm = lay.tile((512, 256, 128))
    rb = lay.rope_block(tm)
    nt = lay.nt
    row = lambda i: (i, 0)
    ch = KEY_CHUNK
    assert tm % ch == 0
    return pl.pallas_call(
        functools.partial(_odd_in_kernel, qw=qw, kw=kw, hd=hd),
        out_shape=(
            jax.ShapeDtypeStruct((nt, qw), BF16),
            jax.ShapeDtypeStruct((nt, kw), BF16),
            jax.ShapeDtypeStruct((nt // ch, kw, ch), BF16),
        ),
        grid=(nt // tm,),
        in_specs=[
            pl.BlockSpec((tm, d), row),
            _mod_spec(lay, tm),
            _const_spec((1, d)),
            _resident_spec(w_cat.shape),
            pl.BlockSpec((tm, hd), lambda i: (rb(i), 0)),
            pl.BlockSpec((tm, hd), lambda i: (rb(i), 0)),
            _const_spec(gq.shape),
            _const_spec(gk.shape),
        ],
        out_specs=(pl.BlockSpec((tm, qw), row), pl.BlockSpec((tm, kw), row),
                   pl.BlockSpec((tm // ch, kw, ch), lambda i: (i, 0, 0))),
        compiler_params=pltpu.CompilerParams(vmem_limit_bytes=V7X_VMEM_LIMIT),
        name="odd_in",
    )(x_all, mod_l, g_pre.reshape(1, d), w_cat, cos_t, sin_t, gq, gk)


def _dense_attn_kernel(q_ref, kl_ref, kx_ref, vl_ref, vx_ref, o_ref, s_ref, m_ref, l_ref, acc_ref, *, n_q_lat, nb):
    j = pl.program_id(2)
    tq = q_ref.shape[0]
    t, hd = kl_ref.shape
    lc = kx_ref.shape[0]
    ch = vl_ref.shape[2]
    cols = GQA_GROUP * tq
    qs = jnp.concatenate([q_ref[:, g * hd:(g + 1) * hd] for g in range(GQA_GROUP)], axis=0)

    def sum8(x):
        return jnp.sum(x.reshape(-1, 8, cols), axis=0)

    def colmax(x):
        return jnp.max(jnp.max(x.reshape(-1, 8, cols), axis=0), axis=0, keepdims=True)

    sx = _dot_t(kx_ref[...], qs)
    m0 = colmax(sx)
    p = jnp.exp2(sx - m0)
    m_ref[...] = m0
    l_ref[...] = sum8(p)
    acc_ref[...] = _dot(jnp.concatenate([vx_ref[c] for c in range(lc // ch)], axis=1), p.astype(BF16))

    @pl.when(j < n_q_lat)
    def _():
        n_blocks = t // nb

        def produce(i):
            s = _dot_t(kl_ref[i * nb:(i + 1) * nb, :], qs)
            s_ref[i % 2] = s
            return colmax(s)

        m = m_ref[...]
        block_max = produce(0)
        for i in range(n_blocks):
            m_new = jnp.maximum(m, block_max)
            if i + 1 < n_blocks:
                block_max = produce(i + 1)
            alpha = jnp.exp2(m - m_new)
            p = jnp.exp2(s_ref[i % 2] - m_new)
            vt = jnp.concatenate([vl_ref[i * (nb // ch) + c] for c in range(nb // ch)], axis=1)
            l_ref[...] = l_ref[...] * alpha + sum8(p)
            acc_ref[...] = acc_ref[...] * alpha + _dot(vt, p.astype(BF16))
            m = m_new

    o_t = acc_ref[...] * (1.0 / jnp.sum(l_ref[...], axis=0, keepdims=True))
    for g in range(GQA_GROUP):
        o_ref[:, g * hd:(g + 1) * hd] = o_t[:, g * tq:(g + 1) * tq].T.astype(o_ref.dtype)


def dense_attn(lay, q, k, vt, hd, with_ctx_queries):
    nt, qw = q.shape
    tq = 128
    ch = vt.shape[2]
    assert lay.t % tq == 0 and lay.lc % tq == 0 and lay.n_lat % lay.lc == 0 and lay.t % ch == 0 and lay.lc % ch == 0
    n_kv = k.shape[1] // hd
    gw = GQA_GROUP * hd
    nql, nqc = lay.t // tq, lay.lc // tq
    ctx_q0 = lay.n_lat // tq
    ctx_k0 = lay.n_lat // lay.lc
    n_rows = nt if with_ctx_queries else lay.n_lat

    def qblock(b, kv, j):
        return (jnp.where(j < nql, b * nql + j, ctx_q0 + b * nqc + (j - nql)), kv)

    qspec = pl.BlockSpec((tq, gw), qblock)
    cols = GQA_GROUP * tq
    nb = 1024 if lay.t % 1024 == 0 else ch
    assert nb % ch == 0 and lay.t % nb == 0
    return pl.pallas_call(
        functools.partial(_dense_attn_kernel, n_q_lat=nql, nb=nb),
        out_shape=jax.ShapeDtypeStruct((n_rows, qw), BF16),
        grid=(lay.b, n_kv, nql + (nqc if with_ctx_queries else 0)),
        in_specs=[
            qspec,
            pl.BlockSpec((lay.t, hd), lambda b, kv, j: (b, kv)),
            pl.BlockSpec((lay.lc, hd), lambda b, kv, j: (ctx_k0 + b, kv)),
            pl.BlockSpec((lay.t // ch, hd, ch), lambda b, kv, j: (b, kv, 0)),
            pl.BlockSpec((lay.lc // ch, hd, ch), lambda b, kv, j: (ctx_k0 + b, kv, 0)),
        ],
        out_specs=qspec,
        scratch_shapes=[pltpu.VMEM((2, nb, cols), F32), pltpu.VMEM((1, cols), F32), pltpu.VMEM((8, cols), F32),
                        pltpu.VMEM((hd, cols), F32)],
        compiler_params=pltpu.CompilerParams(vmem_limit_bytes=V7X_VMEM_LIMIT),
        name="dense_attn",
    )(q, k, k, vt, vt)


def _rot_cols(w):
    pairs = w.reshape(w.shape[0], -1, 2)
    return jnp.stack([-pairs[..., 1], pairs[..., 0]], axis=-1).reshape(w.shape)


def _dup_heads(w, hd):
    heads = w.reshape(w.shape[0], -1, 1, hd)
    return jnp.concatenate([heads, heads], axis=2).reshape(w.shape[0], -1)


def _block_diag(w):
    h, n, _ = w.shape
    eye = jnp.eye(h, dtype=w.dtype)
    return (eye[:, None, :, None] * w[:, :, None, :]).reshape(h * n, h * n)


def _rope_tables(t, hd, lanes, ident_rows):
    n_freq = hd // 4
    freq = ROPE_THETA ** (-jnp.arange(n_freq, dtype=F32) / n_freq)
    pos = jnp.arange(t)
    row = (pos // GRID_W).astype(F32)
    col = (pos % GRID_W).astype(F32)
    ang = jnp.concatenate([row[:, None] * freq, col[:, None] * freq], axis=-1)
    cos = jnp.repeat(jnp.cos(ang), 2, axis=-1)
    sin = jnp.repeat(jnp.sin(ang), 2, axis=-1)
    cos = jnp.concatenate([cos, jnp.ones((ident_rows, hd), F32)], axis=0)
    sin = jnp.concatenate([sin, jnp.zeros((ident_rows, hd), F32)], axis=0)
    reps = lanes // hd
    return jnp.tile(cos, (1, reps)), jnp.tile(sin, (1, reps))


def _swap_pairs(g):
    return g.reshape(-1, 2)[:, ::-1].reshape(g.shape)


def kernel(x, c, ctx, c_ctx, w_ada, b_ada, norm_pre, norm_post, ffn_w_gate, ffn_w_up, ffn_w_down, even_w_in,
           even_conv_w, even_conv_b, lru_w_a, lru_b_a, lru_w_x, lru_b_x, lru_lambda, attn_sink, even_w_out, odd_w_in,
           odd_q_norm, odd_k_norm, odd_w_out):
    b, t, d = x.shape
    lc = ctx.shape[1]
    depth = w_ada.shape[0]
    lay = _Layout(b, t, lc)
    lay.d = d
    assert b + 1 <= MOD_ROWS and w_ada.shape[2] == N_MOD * d

    lw = even_conv_w.shape[2]
    win_heads = attn_sink.shape[1]
    win_hd = (d - lw) // win_heads
    win_kv = win_heads // GQA_GROUP
    glb_hd = odd_q_norm.shape[1]
    glb_heads = d // glb_hd
    glb_kv = glb_heads // GQA_GROUP
    qw_e, kw_e = win_heads * win_hd, win_kv * win_hd
    qw_o, kw_o = glb_heads * glb_hd, glb_kv * glb_hd

    x_all = jnp.concatenate([x.reshape(b * t, d), ctx.reshape(b * lc, d)], axis=0)
    c_rows = jnp.concatenate([c, c_ctx[None, :], jnp.zeros((MOD_ROWS - b - 1, d), F32)], axis=0)
    mod = ada_mod(c_rows, w_ada, b_ada).reshape(depth, MOD_ROWS, N_MOD, d)

    tm = lay.tile((512, 256, 128))
    cos_e, sin_e = _rope_tables(t, win_hd, 2 * win_hd, tm)
    cos_o, sin_o = _rope_tables(t, glb_hd, glb_hd, tm)

    wg16, wu16, wd16 = ffn_w_gate.astype(BF16), ffn_w_up.astype(BF16), ffn_w_down.astype(BF16)

    for l in range(depth):
        last = l == depth - 1
        i = l // 2
        mod_l = mod[l]
        x_all = ffn_sublayer(lay, x_all, lay.nt, mod_l, 0, norm_pre[l, 0], norm_post[l, 0], wg16[l, 0], wu16[l, 0], wd16[l, 0])
        rows_out = lay.n_lat if last else lay.nt
        if l % 2 == 0:
            w = even_w_in[i]
            o1, o2, o3, o4 = 2 * lw, 2 * lw + qw_e, 2 * lw + qw_e + kw_e, 2 * lw + qw_e + 2 * kw_e
            wq = w[:, o1:o2] * (win_hd ** -0.5)
            wk = w[:, o2:o3]
            w_cat = jnp.concatenate([
                w[:, :o1], wq, _rot_cols(wq), _dup_heads(wk, win_hd), _dup_heads(_rot_cols(wk), win_hd),
                _dup_heads(w[:, o3:o4], win_hd)], axis=1).astype(BF16)
            xa, gg, q, kd, vd = even_in(lay, x_all, mod_l, norm_pre[l, 1], w_cat, cos_e, sin_e, lw, qw_e, 2 * kw_e)
            spl_args = []
            for direction in range(2):
                w_gate = jnp.concatenate([_block_diag(lru_w_a[i, direction]), _block_diag(lru_w_x[i, direction])], axis=1).astype(BF16)
                b_gate = jnp.concatenate([lru_b_a[i, direction], lru_b_x[i, direction]])
                spl_args.append((w_gate, b_gate, lru_lambda[i, direction]))
            hf = lru_scan(lay, xa, even_conv_w[i], even_conv_b[i], *spl_args[0], reverse=False)
            ya = lru_scan(lay, xa, even_conv_w[i], even_conv_b[i], *spl_args[1], reverse=True, hf=hf, gg=gg)
            sink_rows = jnp.broadcast_to(attn_sink[i][:, None], (win_heads, 128)).astype(F32)
            ob = win_attn(lay, q, kd, vd, sink_rows, win_heads, win_hd)
            x_all = mix_out(lay, x_all, rows_out, mod_l, norm_post[l, 1], even_w_out[i].astype(BF16), [ya, ob])
        else:
            w = odd_w_in[i]
            wq, wk, wv = w[:, :qw_o], w[:, qw_o:qw_o + kw_o], w[:, qw_o + kw_o:]
            w_cat = jnp.concatenate([wq, _rot_cols(wq), wk, _rot_cols(wk), wv], axis=1).astype(BF16)
            gq = jnp.stack([odd_q_norm[i], _swap_pairs(odd_q_norm[i])])
            gk = jnp.stack([odd_k_norm[i], _swap_pairs(odd_k_norm[i])])
            q, k, vt = odd_in(lay, x_all, mod_l, norm_pre[l, 1], w_cat, cos_o, sin_o, gq, gk, qw_o, kw_o, glb_hd)
            o = dense_attn(lay, q, k, vt, glb_hd, with_ctx_queries=not last)
            x_all = mix_out(lay, x_all, rows_out, mod_l, norm_post[l, 1], odd_w_out[i].astype(BF16), [o])
        x_all = ffn_sublayer(lay, x_all, rows_out, mod_l, 6, norm_pre[l, 2], norm_post[l, 2], wg16[l, 1], wu16[l, 1], wd16[l, 1])
    return x_all[:b * t].reshape(b, t, d)
```

```python
import functools

import jax
import jax.numpy as jnp
import numpy as np
from jax import lax
from jax.experimental import pallas as pl
from jax.experimental.pallas import tpu as pltpu

F32 = jnp.float32
BF16 = jnp.bfloat16

EPS = 1e-6
FFN_RES = 0.5
LRU_C = 8.0
WINDOW = 128
GRID_W = 64
ROPE_THETA = 10000.0
GQA_GROUP = 4
N_MOD = 9
MOD_ROWS = 16

V7X_VMEM_LIMIT = 56 * 1024 * 1024
Q_TILE_WIN = 256
VT_CHUNK = 128
LRU_CHUNK = 256
HALO = 8
KEY_CHUNK = 256
LOG2_E = 1.4426950408889634


def _const_spec(shape):
    zeros = (0,) * len(shape)
    return pl.BlockSpec(shape, lambda *_: zeros)


def _resident_spec(shape):
    zeros = (0,) * len(shape)
    return pl.BlockSpec(shape, lambda *_: zeros, pipeline_mode=pl.Buffered(1))


def _dot(a, b):
    return jnp.dot(a, b, preferred_element_type=F32)


def _dot_t(a, b):
    return lax.dot_general(a, b, (((1,), (1,)), ((), ())), preferred_element_type=F32)


def _rms(x, g):
    return x * lax.rsqrt(jnp.mean(x * x, axis=-1, keepdims=True) + EPS) * g


def _pre(x, g, shift, scale):
    return _rms(x, g) * (1.0 + scale) + shift


def _ada_kernel(c_ref, w_ref, b_ref, o_ref):
    c = c_ref[...]
    s = c * jax.nn.sigmoid(c)
    o_ref[...] = _dot(s.astype(BF16), w_ref[...].astype(BF16)) + b_ref[...]


def ada_mod(c_rows, w_ada, b_ada):
    depth, d, n = w_ada.shape
    tn = 1024
    return pl.pallas_call(
        _ada_kernel,
        out_shape=jax.ShapeDtypeStruct((depth, MOD_ROWS, n), F32),
        grid=(depth, n // tn),
        in_specs=[
            pl.BlockSpec((MOD_ROWS, d), lambda l, j: (0, 0)),
            pl.BlockSpec((None, d, tn), lambda l, j: (l, 0, j)),
            pl.BlockSpec((None, 1, tn), lambda l, j: (l, 0, j)),
        ],
        out_specs=pl.BlockSpec((None, MOD_ROWS, tn), lambda l, j: (l, 0, j)),
        name="ada_mod",
    )(c_rows, w_ada, b_ada.reshape(depth, 1, n))


class _Layout:
    def __init__(self, b, t, lc):
        self.b, self.t, self.lc = b, t, lc
        self.n_lat = b * t
        self.nt = b * (t + lc)

    def tile(self, candidates):
        for tm in candidates:
            if self.t % tm == 0 and (self.b * self.lc) % tm == 0:
                return tm
        raise ValueError("no token tile divides both the latent and the context stream")

    def mod_row(self, tm):
        n_lat_tiles, per_b, b = self.n_lat // tm, self.t // tm, self.b
        return lambda i: jnp.where(i < n_lat_tiles, i // per_b, b)

    def rope_block(self, tm):
        n_lat_tiles, per_b = self.n_lat // tm, self.t // tm
        return lambda i: jnp.where(i < n_lat_tiles, i % per_b, per_b)


def _mod_spec(lay, tm):
    row = lay.mod_row(tm)
    return pl.BlockSpec((None, N_MOD, lay_d(lay)), lambda i: (row(i), 0, 0))


def lay_d(lay):
    return lay.d


def _ffn_kernel(x_ref, mod_ref, gpre_ref, gpost_ref, wg_ref, wu_ref, wd_ref, o_ref, h_ref, *, m0, ff_chunk):
    x = x_ref[...]
    u = _pre(x, gpre_ref[...], mod_ref[m0:m0 + 1, :], mod_ref[m0 + 1:m0 + 2, :]).astype(BF16)
    d_ff = wg_ref.shape[1]
    for c0 in range(0, d_ff, ff_chunk):
        g = _dot(u, wg_ref[:, c0:c0 + ff_chunk])
        up = _dot(u, wu_ref[:, c0:c0 + ff_chunk])
        h_ref[:, c0:c0 + ff_chunk] = (g * jax.nn.sigmoid(g) * up).astype(BF16)
    y = _dot(h_ref[...], wd_ref[...])
    o_ref[...] = x + FFN_RES * mod_ref[m0 + 2:m0 + 3, :] * _rms(y, gpost_ref[...])


def ffn_sublayer(lay, x_all, n_rows, mod_l, m0, g_pre, g_post, wg, wu, wd):
    d = lay.d
    d_ff = wg.shape[1]
    tm = lay.tile((512, 256, 128))
    return pl.pallas_call(
        functools.partial(_ffn_kernel, m0=m0, ff_chunk=256),
        out_shape=jax.ShapeDtypeStruct((n_rows, d), F32),
        grid=(n_rows // tm,),
        in_specs=[
            pl.BlockSpec((tm, d), lambda i: (i, 0)),
            _mod_spec(lay, tm),
            _const_spec((1, d)),
            _const_spec((1, d)),
            _resident_spec((d, d_ff)),
            _resident_spec((d, d_ff)),
            _resident_spec((d_ff, d)),
        ],
        out_specs=pl.BlockSpec((tm, d), lambda i: (i, 0)),
        scratch_shapes=[pltpu.VMEM((tm, d_ff), BF16)],
        compiler_params=pltpu.CompilerParams(vmem_limit_bytes=V7X_VMEM_LIMIT),
        name="ffn_sublayer",
    )(x_all, mod_l, g_pre.reshape(1, d), g_post.reshape(1, d), wg, wu, wd)


def _even_in_kernel(x_ref, mod_ref, g_ref, w_ref, cos_ref, sin_ref, xa_ref, gg_ref, q_ref, k_ref, vt_ref, *, lw, qw, kw, vw,
                    q_scale):
    u = _pre(x_ref[...], g_ref[...], mod_ref[3:4, :], mod_ref[4:5, :]).astype(BF16)
    cos, sin = cos_ref[...], sin_ref[...]
    o = 0
    xa_ref[...] = _dot(u, w_ref[:, o:o + lw])
    o += lw
    gg_ref[...] = jax.nn.gelu(_dot(u, w_ref[:, o:o + lw]))
    o += lw
    q = _dot(u, w_ref[:, o:o + qw])
    qr = _dot(u, w_ref[:, o + qw:o + 2 * qw])
    o += 2 * qw
    reps = qw // cos.shape[1]
    q_ref[...] = ((q * jnp.concatenate([cos] * reps, axis=1) + qr * jnp.concatenate([sin] * reps, axis=1)) * q_scale).astype(BF16)
    k = _dot(u, w_ref[:, o:o + kw])
    kr = _dot(u, w_ref[:, o + kw:o + 2 * kw])
    o += 2 * kw
    reps = kw // cos.shape[1]
    k_ref[...] = (k * jnp.concatenate([cos] * reps, axis=1) + kr * jnp.concatenate([sin] * reps, axis=1)).astype(BF16)
    v = _dot(u, w_ref[:, o:o + vw])
    ch = vt_ref.shape[2]
    for c in range(vt_ref.shape[0]):
        vt_ref[c] = v[c * ch:(c + 1) * ch, :].T.astype(BF16)


def even_in(lay, x_all, mod_l, g_pre, w_cat, cos_t, sin_t, lw, qw, kw, vw, q_scale):
    d = lay.d
    tm = lay.tile((512, 256, 128))
    rb = lay.rope_block(tm)
    nt = lay.nt
    row = lambda i: (i, 0)
    ch = VT_CHUNK
    assert tm % ch == 0
    return pl.pallas_call(
        functools.partial(_even_in_kernel, lw=lw, qw=qw, kw=kw, vw=vw, q_scale=q_scale),
        out_shape=(
            jax.ShapeDtypeStruct((nt, lw), F32),
            jax.ShapeDtypeStruct((nt, lw), F32),
            jax.ShapeDtypeStruct((nt, qw), BF16),
            jax.ShapeDtypeStruct((nt, kw), BF16),
            jax.ShapeDtypeStruct((nt // ch, vw, ch), BF16),
        ),
        grid=(nt // tm,),
        in_specs=[
            pl.BlockSpec((tm, d), row),
            _mod_spec(lay, tm),
            _const_spec((1, d)),
            _resident_spec(w_cat.shape),
            pl.BlockSpec((tm, cos_t.shape[1]), lambda i: (rb(i), 0)),
            pl.BlockSpec((tm, cos_t.shape[1]), lambda i: (rb(i), 0)),
        ],
        out_specs=(
            pl.BlockSpec((tm, lw), row),
            pl.BlockSpec((tm, lw), row),
            pl.BlockSpec((tm, qw), row),
            pl.BlockSpec((tm, kw), row),
            pl.BlockSpec((tm // ch, vw, ch), lambda i: (i, 0, 0)),
        ),
        compiler_params=pltpu.CompilerParams(vmem_limit_bytes=V7X_VMEM_LIMIT),
        name="even_in",
    )(x_all, mod_l, g_pre.reshape(1, d), w_cat, cos_t, sin_t)


def _shift_rows(x, dist, fill, reverse):
    n = x.shape[0]
    rows = lax.broadcasted_iota(jnp.int32, x.shape, 0)
    if reverse:
        return jnp.where(rows < n - dist, pltpu.roll(x, n - dist, 0), fill)
    return jnp.where(rows >= dist, pltpu.roll(x, dist, 0), fill)


def _lru_kernel(xm_ref, xl_ref, xr_ref, cw_ref, cb_ref, wg_ref, bg_ref, lam_ref, *rest,
                reverse, n_ctx_chunks, n_lat_chunks):
    if reverse:
        hf_ref, gg_ref, o_ref, xbuf, carry = rest
    else:
        o_ref, xbuf, carry = rest
    s = pl.program_id(1)
    cl, lw = xm_ref.shape
    in_ctx = s < n_ctx_chunks
    pos = jnp.where(in_ctx, s, s - n_ctx_chunks)
    n_seq = jnp.where(in_ctx, n_ctx_chunks, n_lat_chunks)
    chunk = (n_seq - 1 - pos) if reverse else pos

    @pl.when(s == 0)
    def _():
        carry[...] = jnp.zeros_like(carry)

    xbuf[0:HALO, :] = jnp.where(chunk > 0, xl_ref[...], 0.0)
    xbuf[HALO:HALO + cl, :] = xm_ref[...]
    xbuf[HALO + cl:, :] = jnp.where(chunk < n_seq - 1, xr_ref[...], 0.0)
    taps = cw_ref.shape[0]
    left = taps // 2
    xin = cb_ref[...] + xbuf[pl.ds(HALO - left, cl), :] * cw_ref[0:1, :]
    for t in range(1, taps):
        xin = xin + xbuf[pl.ds(HALO - left + t, cl), :] * cw_ref[t:t + 1, :]

    gates = _dot(xin.astype(BF16), wg_ref[...]) + bg_ref[...]
    r = jax.nn.sigmoid(gates[:, :lw])
    gate_i = jax.nn.sigmoid(gates[:, lw:])
    log_a = (-LRU_C) * r * jax.nn.softplus(-lam_ref[...])
    a = jnp.exp(log_a)
    b = jnp.sqrt(-jnp.tanh(log_a) * (a * a + 1.0)) * gate_i * xin

    dist = 1
    while dist < cl:
        b = b + a * _shift_rows(b, dist, 0.0, reverse)
        a = a * _shift_rows(a, dist, 1.0, reverse)
        dist *= 2
    h = b + a * carry[...]
    carry[...] = h[0:1, :] if reverse else h[cl - 1:cl, :]
    if reverse:
        o_ref[...] = ((hf_ref[...] + h) * gg_ref[...]).astype(o_ref.dtype)
    else:
        o_ref[...] = h


def lru_scan(lay, xa, conv_w, conv_b, w_gate, b_gate, lam, reverse, hf=None, gg=None):
    nt, lw = xa.shape
    cl = LRU_CHUNK
    assert lay.t % cl == 0 and lay.lc % cl == 0
    ncc, nlc = lay.lc // cl, lay.t // cl
    n_lat_blocks = lay.n_lat // cl

    def chunk_block(b, s):
        in_ctx = s < ncc
        pos = jnp.where(in_ctx, s, s - ncc)
        if reverse:
            pos = jnp.where(in_ctx, ncc - 1 - pos, nlc - 1 - pos)
        return jnp.where(in_ctx, n_lat_blocks + b * ncc + pos, b * nlc + pos)

    per = cl // HALO
    last_halo = nt // HALO - 1
    main = pl.BlockSpec((cl, lw), lambda b, s: (chunk_block(b, s), 0))
    in_specs = [
        main,
        pl.BlockSpec((HALO, lw), lambda b, s: (jnp.maximum(chunk_block(b, s) * per - 1, 0), 0)),
        pl.BlockSpec((HALO, lw), lambda b, s: (jnp.minimum((chunk_block(b, s) + 1) * per, last_halo), 0)),
        _const_spec(conv_w.shape),
        _const_spec((1, lw)),
        _const_spec(w_gate.shape),
        _const_spec((1, 2 * lw)),
        _const_spec((1, lw)),
    ]
    args = [xa, xa, xa, conv_w, conv_b.reshape(1, lw), w_gate, b_gate.reshape(1, 2 * lw), lam.reshape(1, lw)]
    if reverse:
        in_specs += [main, main]
        args += [hf, gg]
    return pl.pallas_call(
        functools.partial(_lru_kernel, reverse=reverse, n_ctx_chunks=ncc, n_lat_chunks=nlc),
        out_shape=jax.ShapeDtypeStruct((nt, lw), BF16 if reverse else F32),
        grid=(lay.b, ncc + nlc),
        in_specs=in_specs,
        out_specs=main,
        scratch_shapes=[pltpu.VMEM((cl + 2 * HALO, lw), F32), pltpu.VMEM((1, lw), F32)],
        compiler_params=pltpu.CompilerParams(dimension_semantics=("arbitrary", "arbitrary")),
        name="lru_bwd" if reverse else "lru_fwd",
    )(*args)


def _win_attn_kernel(q_ref, kl_ref, kx_ref, vl_ref, vx_ref, sink_ref, o_ref, *, n_q_lat, n_kv, hd):
    j = pl.program_id(1)
    tq = q_ref.shape[0]
    t = kl_ref.shape[0]
    lc = kx_ref.shape[0]
    vch = vl_ref.shape[2]
    band = tq + 2 * WINDOW
    cols = GQA_GROUP * tq
    low = lax.broadcasted_iota(jnp.int32, (tq, 2 * hd), 1) < hd

    def colsum(x):
        return jnp.sum(jnp.sum(x.reshape(-1, 8, cols), axis=0), axis=0, keepdims=True)

    def colmax(x):
        return jnp.max(jnp.max(x.reshape(-1, 8, cols), axis=0), axis=0, keepdims=True)

    def attend(kv, k_parts, vt_parts, bias):
        qs = []
        for g in range(GQA_GROUP):
            h = kv * GQA_GROUP + g
            qp = q_ref[:, 2 * hd * (h // 2):2 * hd * (h // 2 + 1)]
            qs.append(jnp.where(low if h % 2 == 0 else jnp.logical_not(low), qp, jnp.zeros_like(qp)))
        qs = jnp.concatenate(qs, axis=0)
        sink = jnp.concatenate(
            [jnp.concatenate([sink_ref[kv * GQA_GROUP + g:kv * GQA_GROUP + g + 1, :]] * (tq // 128), axis=1)
             for g in range(GQA_GROUP)], axis=1) * LOG2_E
        s = [_dot_t(kp, qs) for kp in k_parts]
        if bias is not None:
            s[0] = s[0] + bias
        m = sink
        for sp in s:
            m = jnp.maximum(m, colmax(sp))
        den = jnp.exp2(sink - m)
        p = []
        for sp in s:
            pp = jnp.exp2(sp - m)
            den = den + colsum(pp)
            p.append(pp.astype(BF16))
        o_t = _dot(jnp.concatenate(vt_parts, axis=1), jnp.concatenate(p, axis=0)) * (1.0 / den)
        for pair in range(GQA_GROUP // 2):
            both = jnp.concatenate([o_t[:, (2 * pair + i) * tq:(2 * pair + i + 1) * tq] for i in range(2)], axis=0)
            c0 = 2 * hd * (kv * GQA_GROUP // 2 + pair)
            o_ref[:, c0:c0 + 2 * hd] = both.T.astype(o_ref.dtype)

    def ctx_parts(kv):
        ks = kx_ref[:, 2 * hd * kv:2 * hd * (kv + 1)]
        vts = [vx_ref[c][hd * kv:hd * (kv + 1), :] for c in range(lc // vch)]
        return ks, vts

    @pl.when(j < n_q_lat)
    def _():
        start = pl.multiple_of(jnp.clip(j * tq - WINDOW, 0, t - band), 128)
        kpos = start + lax.broadcasted_iota(jnp.int32, (band, tq), 0)
        qpos = j * tq + lax.broadcasted_iota(jnp.int32, (band, tq), 1)
        bias = jnp.where(jnp.abs(kpos - qpos) > WINDOW, -jnp.inf, 0.0).astype(F32)
        bias = jnp.concatenate([bias] * GQA_GROUP, axis=1)
        c0 = start // vch
        for kv in range(n_kv):
            kx, vxs = ctx_parts(kv)
            kb = kl_ref[pl.ds(start, band), 2 * hd * kv:2 * hd * (kv + 1)]
            vbs = [vl_ref[c0 + c][hd * kv:hd * (kv + 1), :] for c in range(band // vch)]
            attend(kv, [kb, kx], vbs + vxs, bias)

    @pl.when(j >= n_q_lat)
    def _():
        for kv in range(n_kv):
            kx, vxs = ctx_parts(kv)
            attend(kv, [kx], vxs, None)


def win_attn(lay, q, kd, vt, sink_rows, n_kv, hd):
    nt, qw = q.shape
    kw = kd.shape[1]
    tq = Q_TILE_WIN
    vch = vt.shape[2]
    assert lay.t % tq == 0 and lay.lc % tq == 0 and lay.n_lat % lay.lc == 0 and tq % 128 == 0
    assert vch == 128 and WINDOW % vch == 0 and lay.lc % vch == 0 and lay.t >= tq + 2 * WINDOW
    nb, ncq = lay.t // tq, lay.lc // tq
    ctx_q0 = lay.n_lat // tq
    ctx_k0 = lay.n_lat // lay.lc

    def qblock(b, j):
        return jnp.where(j < nb, b * nb + j, ctx_q0 + b * ncq + (j - nb))

    qspec = pl.BlockSpec((tq, qw), lambda b, j: (qblock(b, j), 0))
    return pl.pallas_call(
        functools.partial(_win_attn_kernel, n_q_lat=nb, n_kv=n_kv, hd=hd),
        out_shape=jax.ShapeDtypeStruct((nt, qw), BF16),
        grid=(lay.b, nb + ncq),
        in_specs=[
            qspec,
            pl.BlockSpec((lay.t, kw), lambda b, j: (b, 0)),
            pl.BlockSpec((lay.lc, kw), lambda b, j: (ctx_k0 + b, 0)),
            pl.BlockSpec((lay.t // vch, vt.shape[1], vch), lambda b, j: (b, 0, 0)),
            pl.BlockSpec((lay.lc // vch, vt.shape[1], vch), lambda b, j: (ctx_k0 + b, 0, 0)),
            _const_spec(sink_rows.shape),
        ],
        out_specs=qspec,
        compiler_params=pltpu.CompilerParams(vmem_limit_bytes=V7X_VMEM_LIMIT),
        name="win_attn",
    )(q, kd, kd, vt, vt, sink_rows)


def _mix_out_kernel(x_ref, mod_ref, g_ref, w_ref, *rest):
    o_ref = rest[-1]
    parts = [r[...] for r in rest[:-1]]
    y = _dot(jnp.concatenate(parts, axis=1) if len(parts) > 1 else parts[0], w_ref[...])
    o_ref[...] = x_ref[...] + mod_ref[5:6, :] * _rms(y, g_ref[...])


def mix_out(lay, x_all, n_rows, mod_l, g_post, w_out, parts):
    d = lay.d
    tm = lay.tile((512, 256, 128))
    row = lambda i: (i, 0)
    return pl.pallas_call(
        _mix_out_kernel,
        out_shape=jax.ShapeDtypeStruct((n_rows, d), F32),
        grid=(n_rows // tm,),
        in_specs=[pl.BlockSpec((tm, d), row), _mod_spec(lay, tm), _const_spec((1, d)), _resident_spec(w_out.shape)]
        + [pl.BlockSpec((tm, p.shape[1]), row) for p in parts],
        out_specs=pl.BlockSpec((tm, d), row),
        name="mix_out",
    )(x_all, mod_l, g_post.reshape(1, d), w_out, *parts)


def _odd_in_kernel(x_ref, mod_ref, g_ref, w_ref, cos_ref, sin_ref, gq_ref, gk_ref, q_ref, k_ref, vt_ref, *, qw, kw, hd):
    u = _pre(x_ref[...], g_ref[...], mod_ref[3:4, :], mod_ref[4:5, :]).astype(BF16)
    cos, sin = cos_ref[...], sin_ref[...]

    def normed_rope(o, width, gains, out_ref, post):
        cg = cos * gains[0:1, :]
        sg = sin * gains[1:2, :]
        step = 2 * hd
        for c0 in range(0, width, step):
            x2 = _dot(u, w_ref[:, o + c0:o + c0 + step])
            xr2 = _dot(u, w_ref[:, o + width + c0:o + width + c0 + step])
            for h0 in range(0, step, hd):
                x, xr = x2[:, h0:h0 + hd], xr2[:, h0:h0 + hd]
                r = lax.rsqrt(jnp.mean(x * x, axis=-1, keepdims=True) + EPS) * post
                out_ref[:, c0 + h0:c0 + h0 + hd] = (r * (x * cg + xr * sg)).astype(out_ref.dtype)

    normed_rope(0, qw, gq_ref[...], q_ref, hd ** -0.5 * LOG2_E)
    normed_rope(2 * qw, kw, gk_ref[...], k_ref, 1.0)
    o = 2 * qw + 2 * kw
    v = _dot(u, w_ref[:, o:o + kw])
    ch = vt_ref.shape[2]
    for c in range(vt_ref.shape[0]):
        vt_ref[c] = v[c * ch:(c + 1) * ch, :].T.astype(BF16)


def odd_in(lay, x_all, mod_l, g_pre, w_cat, cos_t, sin_t, gq, gk, qw, kw, hd):
    d = lay.d
    tm = lay.tile((512, 256, 128))
    rb = lay.rope_block(tm)
    nt = lay.nt
    row = lambda i: (i, 0)
    ch = KEY_CHUNK
    assert tm % ch == 0
    return pl.pallas_call(
        functools.partial(_odd_in_kernel, qw=qw, kw=kw, hd=hd),
        out_shape=(
            jax.ShapeDtypeStruct((nt, qw), BF16),
            jax.ShapeDtypeStruct((nt, kw), BF16),
            jax.ShapeDtypeStruct((nt // ch, kw, ch), BF16),
        ),
        grid=(nt // tm,),
        in_specs=[
            pl.BlockSpec((tm, d), row),
            _mod_spec(lay, tm),
            _const_spec((1, d)),
            _resident_spec(w_cat.shape),
            pl.BlockSpec((tm, hd), lambda i: (rb(i), 0)),
            pl.BlockSpec((tm, hd), lambda i: (rb(i), 0)),
            _const_spec(gq.shape),
            _const_spec(gk.shape),
        ],
        out_specs=(pl.BlockSpec((tm, qw), row), pl.BlockSpec((tm, kw), row),
                   pl.BlockSpec((tm // ch, kw, ch), lambda i: (i, 0, 0))),
        compiler_params=pltpu.CompilerParams(vmem_limit_bytes=V7X_VMEM_LIMIT),
        name="odd_in",
    )(x_all, mod_l, g_pre.reshape(1, d), w_cat, cos_t, sin_t, gq, gk)


def _dense_attn_kernel(q_ref, kl_ref, kx_ref, vl_ref, vx_ref, o_ref, s_ref, m_ref, l_ref, acc_ref, *, n_q_lat, nb):
    j = pl.program_id(2)
    tq = q_ref.shape[0]
    t, hd = kl_ref.shape
    lc = kx_ref.shape[0]
    ch = vl_ref.shape[2]
    cols = GQA_GROUP * tq
    qs = jnp.concatenate([q_ref[:, g * hd:(g + 1) * hd] for g in range(GQA_GROUP)], axis=0)

    def sum8(x):
        return jnp.sum(x.reshape(-1, 8, cols), axis=0)

    def colmax(x):
        return jnp.max(jnp.max(x.reshape(-1, 8, cols), axis=0), axis=0, keepdims=True)

    sx = _dot_t(kx_ref[...], qs)
    m0 = colmax(sx)
    p = jnp.exp2(sx - m0)
    m_ref[...] = m0
    l_ref[...] = sum8(p)
    acc_ref[...] = _dot(jnp.concatenate([vx_ref[c] for c in range(lc // ch)], axis=1), p.astype(BF16))

    @pl.when(j < n_q_lat)
    def _():
        n_blocks = t // nb

        def produce(i):
            s = _dot_t(kl_ref[i * nb:(i + 1) * nb, :], qs)
            s_ref[i % 2] = s
            return colmax(s)

        m = m_ref[...]
        block_max = produce(0)
        for i in range(n_blocks):
            m_new = jnp.maximum(m, block_max)
            if i + 1 < n_blocks:
                block_max = produce(i + 1)
            alpha = jnp.exp2(m - m_new)
            p = jnp.exp2(s_ref[i % 2] - m_new)
            vt = jnp.concatenate([vl_ref[i * (nb // ch) + c] for c in range(nb // ch)], axis=1)
            l_ref[...] = l_ref[...] * alpha + sum8(p)
            acc_ref[...] = acc_ref[...] * alpha + _dot(vt, p.astype(BF16))
            m = m_new

    o_t = acc_ref[...] * (1.0 / jnp.sum(l_ref[...], axis=0, keepdims=True))
    for g in range(GQA_GROUP):
        o_ref[:, g * hd:(g + 1) * hd] = o_t[:, g * tq:(g + 1) * tq].T.astype(o_ref.dtype)


def dense_attn(lay, q, k, vt, hd, with_ctx_queries):
    nt, qw = q.shape
    tq = 128
    ch = vt.shape[2]
    assert lay.t % tq == 0 and lay.lc % tq == 0 and lay.n_lat % lay.lc == 0 and lay.t % ch == 0 and lay.lc % ch == 0
    n_kv = k.shape[1] // hd
    gw = GQA_GROUP * hd
    nql, nqc = lay.t // tq, lay.lc // tq
    ctx_q0 = lay.n_lat // tq
    ctx_k0 = lay.n_lat // lay.lc
    n_rows = nt if with_ctx_queries else lay.n_lat

    def qblock(b, kv, j):
        return (jnp.where(j < nql, b * nql + j, ctx_q0 + b * nqc + (j - nql)), kv)

    qspec = pl.BlockSpec((tq, gw), qblock)
    cols = GQA_GROUP * tq
    nb = 1024 if lay.t % 1024 == 0 else ch
    assert nb % ch == 0 and lay.t % nb == 0
    return pl.pallas_call(
        functools.partial(_dense_attn_kernel, n_q_lat=nql, nb=nb),
        out_shape=jax.ShapeDtypeStruct((n_rows, qw), BF16),
        grid=(lay.b, n_kv, nql + (nqc if with_ctx_queries else 0)),
        in_specs=[
            qspec,
            pl.BlockSpec((lay.t, hd), lambda b, kv, j: (b, kv)),
            pl.BlockSpec((lay.lc, hd), lambda b, kv, j: (ctx_k0 + b, kv)),
            pl.BlockSpec((lay.t // ch, hd, ch), lambda b, kv, j: (b, kv, 0)),
            pl.BlockSpec((lay.lc // ch, hd, ch), lambda b, kv, j: (ctx_k0 + b, kv, 0)),
        ],
        out_specs=qspec,
        scratch_shapes=[pltpu.VMEM((2, nb, cols), F32), pltpu.VMEM((1, cols), F32), pltpu.VMEM((8, cols), F32),
                        pltpu.VMEM((hd, cols), F32)],
        compiler_params=pltpu.CompilerParams(vmem_limit_bytes=V7X_VMEM_LIMIT),
        name="dense_attn",
    )(q, k, k, vt, vt)


def _rot_cols(w):
    pairs = w.reshape(w.shape[0], -1, 2)
    return jnp.stack([-pairs[..., 1], pairs[..., 0]], axis=-1).reshape(w.shape)


def _dup_heads(w, hd):
    heads = w.reshape(w.shape[0], -1, 1, hd)
    return jnp.concatenate([heads, heads], axis=2).reshape(w.shape[0], -1)


def _block_diag(w):
    h, n, _ = w.shape
    eye = jnp.eye(h, dtype=w.dtype)
    return (eye[:, None, :, None] * w[:, :, None, :]).reshape(h * n, h * n)


def _rope_tables(t, hd, lanes, ident_rows):
    n_freq = hd // 4
    freq = ROPE_THETA ** (-jnp.arange(n_freq, dtype=F32) / n_freq)
    pos = jnp.arange(t)
    row = (pos // GRID_W).astype(F32)
    col = (pos % GRID_W).astype(F32)
    ang = jnp.concatenate([row[:, None] * freq, col[:, None] * freq], axis=-1)
    cos = jnp.repeat(jnp.cos(ang), 2, axis=-1)
    sin = jnp.repeat(jnp.sin(ang), 2, axis=-1)
    cos = jnp.concatenate([cos, jnp.ones((ident_rows, hd), F32)], axis=0)
    sin = jnp.concatenate([sin, jnp.zeros((ident_rows, hd), F32)], axis=0)
    reps = lanes // hd
    return jnp.tile(cos, (1, reps)), jnp.tile(sin, (1, reps))


def _swap_pairs(g):
    return g.reshape(-1, 2)[:, ::-1].reshape(g.shape)


def kernel(x, c, ctx, c_ctx, w_ada, b_ada, norm_pre, norm_post, ffn_w_gate, ffn_w_up, ffn_w_down, even_w_in,
           even_conv_w, even_conv_b, lru_w_a, lru_b_a, lru_w_x, lru_b_x, lru_lambda, attn_sink, even_w_out, odd_w_in,
           odd_q_norm, odd_k_norm, odd_w_out):
    b, t, d = x.shape
    lc = ctx.shape[1]
    depth = w_ada.shape[0]
    lay = _Layout(b, t, lc)
    lay.d = d
    assert b + 1 <= MOD_ROWS and w_ada.shape[2] == N_MOD * d

    lw = even_conv_w.shape[2]
    win_heads = attn_sink.shape[1]
    win_hd = (d - lw) // win_heads
    win_kv = win_heads // GQA_GROUP
    glb_hd = odd_q_norm.shape[1]
    glb_heads = d // glb_hd
    glb_kv = glb_heads // GQA_GROUP
    qw_e, kw_e = win_heads * win_hd, win_kv * win_hd
    qw_o, kw_o = glb_heads * glb_hd, glb_kv * glb_hd

    x_all = jnp.concatenate([x.reshape(b * t, d), ctx.reshape(b * lc, d)], axis=0)
    c_rows = jnp.concatenate([c, c_ctx[None, :], jnp.zeros((MOD_ROWS - b - 1, d), F32)], axis=0)
    mod = ada_mod(c_rows, w_ada, b_ada).reshape(depth, MOD_ROWS, N_MOD, d)

    tm = lay.tile((512, 256, 128))
    cos_e, sin_e = _rope_tables(t, win_hd, 2 * win_hd, tm)
    cos_o, sin_o = _rope_tables(t, glb_hd, glb_hd, tm)

    wg16, wu16, wd16 = ffn_w_gate.astype(BF16), ffn_w_up.astype(BF16), ffn_w_down.astype(BF16)

    for l in range(depth):
        last = l == depth - 1
        i = l // 2
        mod_l = mod[l]
        x_all = ffn_sublayer(lay, x_all, lay.nt, mod_l, 0, norm_pre[l, 0], norm_post[l, 0], wg16[l, 0], wu16[l, 0], wd16[l, 0])
        rows_out = lay.n_lat if last else lay.nt
        if l % 2 == 0:
            w = even_w_in[i]
            o1, o2, o3, o4 = 2 * lw, 2 * lw + qw_e, 2 * lw + qw_e + kw_e, 2 * lw + qw_e + 2 * kw_e
            wq = w[:, o1:o2]
            wk = w[:, o2:o3]
            w_cat = jnp.concatenate([
                w[:, :o1], wq, _rot_cols(wq), _dup_heads(wk, win_hd), _dup_heads(_rot_cols(wk), win_hd),
                w[:, o3:o4]], axis=1).astype(BF16)
            xa, gg, q, kd, vt = even_in(lay, x_all, mod_l, norm_pre[l, 1], w_cat, cos_e, sin_e, lw, qw_e, 2 * kw_e, kw_e,
                                        win_hd ** -0.5 * LOG2_E)
            spl_args = []
            for direction in range(2):
                w_gate = jnp.concatenate([_block_diag(lru_w_a[i, direction]), _block_diag(lru_w_x[i, direction])], axis=1).astype(BF16)
                b_gate = jnp.concatenate([lru_b_a[i, direction], lru_b_x[i, direction]])
                spl_args.append((w_gate, b_gate, lru_lambda[i, direction]))
            hf = lru_scan(lay, xa, even_conv_w[i], even_conv_b[i], *spl_args[0], reverse=False)
            ya = lru_scan(lay, xa, even_conv_w[i], even_conv_b[i], *spl_args[1], reverse=True, hf=hf, gg=gg)
            sink_rows = jnp.broadcast_to(attn_sink[i][:, None], (win_heads, 128)).astype(F32)
            ob = win_attn(lay, q, kd, vt, sink_rows, win_kv, win_hd)
            x_all = mix_out(lay, x_all, rows_out, mod_l, norm_post[l, 1], even_w_out[i].astype(BF16), [ya, ob])
        else:
            w = odd_w_in[i]
            wq, wk, wv = w[:, :qw_o], w[:, qw_o:qw_o + kw_o], w[:, qw_o + kw_o:]
            w_cat = jnp.concatenate([wq, _rot_cols(wq), wk, _rot_cols(wk), wv], axis=1).astype(BF16)
            gq = jnp.stack([odd_q_norm[i], _swap_pairs(odd_q_norm[i])])
            gk = jnp.stack([odd_k_norm[i], _swap_pairs(odd_k_norm[i])])
            q, k, vt = odd_in(lay, x_all, mod_l, norm_pre[l, 1], w_cat, cos_o, sin_o, gq, gk, qw_o, kw_o, glb_hd)
            o = dense_attn(lay, q, k, vt, glb_hd, with_ctx_queries=not last)
            x_all = mix_out(lay, x_all, rows_out, mod_l, norm_post[l, 1], odd_w_out[i].astype(BF16), [o])
        x_all = ffn_sublayer(lay, x_all, rows_out, mod_l, 6, norm_pre[l, 2], norm_post[l, 2], wg16[l, 1], wu16[l, 1], wd16[l, 1])
    return x_all[:b * t].reshape(b, t, d)
```

```python
import functools

import jax
import jax.numpy as jnp
import numpy as np
from jax import lax
from jax.experimental import pallas as pl
from jax.experimental.pallas import tpu as pltpu

F32 = jnp.float32
BF16 = jnp.bfloat16

EPS = 1e-6
FFN_RES = 0.5
LRU_C = 8.0
WINDOW = 128
GRID_W = 64
ROPE_THETA = 10000.0
GQA_GROUP = 4
N_MOD = 9
MOD_ROWS = 16

V7X_VMEM_LIMIT = 56 * 1024 * 1024
Q_TILE_WIN = 256
VT_CHUNK = 128
LRU_CHUNK = 256
HALO = 8
KEY_CHUNK = 256
LOG2_E = 1.4426950408889634
MAX_UNSHIFTED_SCORE = 60.0


def _const_spec(shape):
    zeros = (0,) * len(shape)
    return pl.BlockSpec(shape, lambda *_: zeros)


def _resident_spec(shape):
    zeros = (0,) * len(shape)
    return pl.BlockSpec(shape, lambda *_: zeros, pipeline_mode=pl.Buffered(1))


def _dot(a, b):
    return jnp.dot(a, b, preferred_element_type=F32)


def _dot_t(a, b):
    return lax.dot_general(a, b, (((1,), (1,)), ((), ())), preferred_element_type=F32)


def _rms(x, g):
    return x * lax.rsqrt(jnp.mean(x * x, axis=-1, keepdims=True) + EPS) * g


def _pre(x, g, shift, scale):
    return _rms(x, g) * (1.0 + scale) + shift


def _ada_kernel(c_ref, w_ref, b_ref, o_ref):
    c = c_ref[...]
    s = c * jax.nn.sigmoid(c)
    o_ref[...] = _dot(s.astype(BF16), w_ref[...].astype(BF16)) + b_ref[...]


def ada_mod(c_rows, w_ada, b_ada):
    depth, d, n = w_ada.shape
    tn = 1024
    return pl.pallas_call(
        _ada_kernel,
        out_shape=jax.ShapeDtypeStruct((depth, MOD_ROWS, n), F32),
        grid=(depth, n // tn),
        in_specs=[
            pl.BlockSpec((MOD_ROWS, d), lambda l, j: (0, 0)),
            pl.BlockSpec((None, d, tn), lambda l, j: (l, 0, j)),
            pl.BlockSpec((None, 1, tn), lambda l, j: (l, 0, j)),
        ],
        out_specs=pl.BlockSpec((None, MOD_ROWS, tn), lambda l, j: (l, 0, j)),
        name="ada_mod",
    )(c_rows, w_ada, b_ada.reshape(depth, 1, n))


class _Layout:
    def __init__(self, b, t, lc):
        self.b, self.t, self.lc = b, t, lc
        self.n_lat = b * t
        self.nt = b * (t + lc)

    def tile(self, candidates):
        for tm in candidates:
            if self.t % tm == 0 and (self.b * self.lc) % tm == 0:
                return tm
        raise ValueError("no token tile divides both the latent and the context stream")

    def mod_row(self, tm):
        n_lat_tiles, per_b, b = self.n_lat // tm, self.t // tm, self.b
        return lambda i: jnp.where(i < n_lat_tiles, i // per_b, b)

    def rope_block(self, tm):
        n_lat_tiles, per_b = self.n_lat // tm, self.t // tm
        return lambda i: jnp.where(i < n_lat_tiles, i % per_b, per_b)


def _mod_spec(lay, tm):
    row = lay.mod_row(tm)
    return pl.BlockSpec((None, N_MOD, lay_d(lay)), lambda i: (row(i), 0, 0))


def lay_d(lay):
    return lay.d


def _ffn_kernel(x_ref, mod_ref, gpre_ref, gpost_ref, wg_ref, wu_ref, wd_ref, o_ref, h_ref, *, m0, ff_chunk):
    x = x_ref[...]
    u = _pre(x, gpre_ref[...], mod_ref[m0:m0 + 1, :], mod_ref[m0 + 1:m0 + 2, :]).astype(BF16)
    d_ff = wg_ref.shape[1]
    for c0 in range(0, d_ff, ff_chunk):
        g = _dot(u, wg_ref[:, c0:c0 + ff_chunk])
        up = _dot(u, wu_ref[:, c0:c0 + ff_chunk])
        h_ref[:, c0:c0 + ff_chunk] = (g * jax.nn.sigmoid(g) * up).astype(BF16)
    y = _dot(h_ref[...], wd_ref[...])
    o_ref[...] = x + FFN_RES * mod_ref[m0 + 2:m0 + 3, :] * _rms(y, gpost_ref[...])


def ffn_sublayer(lay, x_all, n_rows, mod_l, m0, g_pre, g_post, wg, wu, wd):
    d = lay.d
    d_ff = wg.shape[1]
    tm = lay.tile((512, 256, 128))
    return pl.pallas_call(
        functools.partial(_ffn_kernel, m0=m0, ff_chunk=256),
        out_shape=jax.ShapeDtypeStruct((n_rows, d), F32),
        grid=(n_rows // tm,),
        in_specs=[
            pl.BlockSpec((tm, d), lambda i: (i, 0)),
            _mod_spec(lay, tm),
            _const_spec((1, d)),
            _const_spec((1, d)),
            _resident_spec((d, d_ff)),
            _resident_spec((d, d_ff)),
            _resident_spec((d_ff, d)),
        ],
        out_specs=pl.BlockSpec((tm, d), lambda i: (i, 0)),
        scratch_shapes=[pltpu.VMEM((tm, d_ff), BF16)],
        compiler_params=pltpu.CompilerParams(vmem_limit_bytes=V7X_VMEM_LIMIT),
        name="ffn_sublayer",
    )(x_all, mod_l, g_pre.reshape(1, d), g_post.reshape(1, d), wg, wu, wd)


def _even_in_kernel(x_ref, mod_ref, g_ref, w_ref, cos_ref, sin_ref, xa_ref, gg_ref, q_ref, k_ref, vt_ref, *, lw, qw, kw, vw,
                    q_scale):
    u = _pre(x_ref[...], g_ref[...], mod_ref[3:4, :], mod_ref[4:5, :]).astype(BF16)
    cos, sin = cos_ref[...], sin_ref[...]
    o = 0
    xa_ref[...] = _dot(u, w_ref[:, o:o + lw])
    o += lw
    gg_ref[...] = jax.nn.gelu(_dot(u, w_ref[:, o:o + lw]))
    o += lw
    q = _dot(u, w_ref[:, o:o + qw])
    qr = _dot(u, w_ref[:, o + qw:o + 2 * qw])
    o += 2 * qw
    reps = qw // cos.shape[1]
    q_ref[...] = ((q * jnp.concatenate([cos] * reps, axis=1) + qr * jnp.concatenate([sin] * reps, axis=1)) * q_scale).astype(BF16)
    k = _dot(u, w_ref[:, o:o + kw])
    kr = _dot(u, w_ref[:, o + kw:o + 2 * kw])
    o += 2 * kw
    reps = kw // cos.shape[1]
    k_ref[...] = (k * jnp.concatenate([cos] * reps, axis=1) + kr * jnp.concatenate([sin] * reps, axis=1)).astype(BF16)
    v = _dot(u, w_ref[:, o:o + vw])
    ch = vt_ref.shape[2]
    for c in range(vt_ref.shape[0]):
        vt_ref[c] = v[c * ch:(c + 1) * ch, :].T.astype(BF16)


def even_in(lay, x_all, mod_l, g_pre, w_cat, cos_t, sin_t, lw, qw, kw, vw, q_scale):
    d = lay.d
    tm = lay.tile((512, 256, 128))
    rb = lay.rope_block(tm)
    nt = lay.nt
    row = lambda i: (i, 0)
    ch = VT_CHUNK
    assert tm % ch == 0
    return pl.pallas_call(
        functools.partial(_even_in_kernel, lw=lw, qw=qw, kw=kw, vw=vw, q_scale=q_scale),
        out_shape=(
            jax.ShapeDtypeStruct((nt, lw), F32),
            jax.ShapeDtypeStruct((nt, lw), F32),
            jax.ShapeDtypeStruct((nt, qw), BF16),
            jax.ShapeDtypeStruct((nt, kw), BF16),
            jax.ShapeDtypeStruct((nt // ch, vw, ch), BF16),
        ),
        grid=(nt // tm,),
        in_specs=[
            pl.BlockSpec((tm, d), row),
            _mod_spec(lay, tm),
            _const_spec((1, d)),
            _resident_spec(w_cat.shape),
            pl.BlockSpec((tm, cos_t.shape[1]), lambda i: (rb(i), 0)),
            pl.BlockSpec((tm, cos_t.shape[1]), lambda i: (rb(i), 0)),
        ],
        out_specs=(
            pl.BlockSpec((tm, lw), row),
            pl.BlockSpec((tm, lw), row),
            pl.BlockSpec((tm, qw), row),
            pl.BlockSpec((tm, kw), row),
            pl.BlockSpec((tm // ch, vw, ch), lambda i: (i, 0, 0)),
        ),
        compiler_params=pltpu.CompilerParams(vmem_limit_bytes=V7X_VMEM_LIMIT),
        name="even_in",
    )(x_all, mod_l, g_pre.reshape(1, d), w_cat, cos_t, sin_t)


def _shift_rows(x, dist, fill, reverse):
    n = x.shape[0]
    rows = lax.broadcasted_iota(jnp.int32, x.shape, 0)
    if reverse:
        return jnp.where(rows < n - dist, pltpu.roll(x, n - dist, 0), fill)
    return jnp.where(rows >= dist, pltpu.roll(x, dist, 0), fill)


def _lru_kernel(xm_ref, xl_ref, xr_ref, cw_ref, cb_ref, wg_ref, bg_ref, lam_ref, *rest,
                reverse, n_ctx_chunks, n_lat_chunks):
    if reverse:
        hf_ref, gg_ref, o_ref, xbuf, carry = rest
    else:
        o_ref, xbuf, carry = rest
    s = pl.program_id(1)
    cl, lw = xm_ref.shape
    in_ctx = s < n_ctx_chunks
    pos = jnp.where(in_ctx, s, s - n_ctx_chunks)
    n_seq = jnp.where(in_ctx, n_ctx_chunks, n_lat_chunks)
    chunk = (n_seq - 1 - pos) if reverse else pos

    @pl.when(s == 0)
    def _():
        carry[...] = jnp.zeros_like(carry)

    xbuf[0:HALO, :] = jnp.where(chunk > 0, xl_ref[...], 0.0)
    xbuf[HALO:HALO + cl, :] = xm_ref[...]
    xbuf[HALO + cl:, :] = jnp.where(chunk < n_seq - 1, xr_ref[...], 0.0)
    taps = cw_ref.shape[0]
    left = taps // 2
    xin = cb_ref[...] + xbuf[pl.ds(HALO - left, cl), :] * cw_ref[0:1, :]
    for t in range(1, taps):
        xin = xin + xbuf[pl.ds(HALO - left + t, cl), :] * cw_ref[t:t + 1, :]

    gates = _dot(xin.astype(BF16), wg_ref[...]) + bg_ref[...]
    r = jax.nn.sigmoid(gates[:, :lw])
    gate_i = jax.nn.sigmoid(gates[:, lw:])
    log_a = (-LRU_C) * r * jax.nn.softplus(-lam_ref[...])
    a = jnp.exp(log_a)
    b = jnp.sqrt(-jnp.tanh(log_a) * (a * a + 1.0)) * gate_i * xin

    dist = 1
    while dist < cl:
        b = b + a * _shift_rows(b, dist, 0.0, reverse)
        a = a * _shift_rows(a, dist, 1.0, reverse)
        dist *= 2
    h = b + a * carry[...]
    carry[...] = h[0:1, :] if reverse else h[cl - 1:cl, :]
    if reverse:
        o_ref[...] = ((hf_ref[...] + h) * gg_ref[...]).astype(o_ref.dtype)
    else:
        o_ref[...] = h


def lru_scan(lay, xa, conv_w, conv_b, w_gate, b_gate, lam, reverse, hf=None, gg=None):
    nt, lw = xa.shape
    cl = LRU_CHUNK
    assert lay.t % cl == 0 and lay.lc % cl == 0
    ncc, nlc = lay.lc // cl, lay.t // cl
    n_lat_blocks = lay.n_lat // cl

    def chunk_block(b, s):
        in_ctx = s < ncc
        pos = jnp.where(in_ctx, s, s - ncc)
        if reverse:
            pos = jnp.where(in_ctx, ncc - 1 - pos, nlc - 1 - pos)
        return jnp.where(in_ctx, n_lat_blocks + b * ncc + pos, b * nlc + pos)

    per = cl // HALO
    last_halo = nt // HALO - 1
    main = pl.BlockSpec((cl, lw), lambda b, s: (chunk_block(b, s), 0))
    in_specs = [
        main,
        pl.BlockSpec((HALO, lw), lambda b, s: (jnp.maximum(chunk_block(b, s) * per - 1, 0), 0)),
        pl.BlockSpec((HALO, lw), lambda b, s: (jnp.minimum((chunk_block(b, s) + 1) * per, last_halo), 0)),
        _const_spec(conv_w.shape),
        _const_spec((1, lw)),
        _const_spec(w_gate.shape),
        _const_spec((1, 2 * lw)),
        _const_spec((1, lw)),
    ]
    args = [xa, xa, xa, conv_w, conv_b.reshape(1, lw), w_gate, b_gate.reshape(1, 2 * lw), lam.reshape(1, lw)]
    if reverse:
        in_specs += [main, main]
        args += [hf, gg]
    return pl.pallas_call(
        functools.partial(_lru_kernel, reverse=reverse, n_ctx_chunks=ncc, n_lat_chunks=nlc),
        out_shape=jax.ShapeDtypeStruct((nt, lw), BF16 if reverse else F32),
        grid=(lay.b, ncc + nlc),
        in_specs=in_specs,
        out_specs=main,
        scratch_shapes=[pltpu.VMEM((cl + 2 * HALO, lw), F32), pltpu.VMEM((1, lw), F32)],
        compiler_params=pltpu.CompilerParams(dimension_semantics=("arbitrary", "arbitrary")),
        name="lru_bwd" if reverse else "lru_fwd",
    )(*args)


def _win_attn_kernel(q_ref, kl_ref, kx_ref, vl_ref, vx_ref, sink_ref, o_ref, *, n_q_lat, n_kv, hd):
    j = pl.program_id(1)
    tq = q_ref.shape[0]
    t = kl_ref.shape[0]
    lc = kx_ref.shape[0]
    vch = vl_ref.shape[2]
    band = tq + 2 * WINDOW
    cols = GQA_GROUP * tq
    low = lax.broadcasted_iota(jnp.int32, (tq, 2 * hd), 1) < hd

    def colsum(x):
        return jnp.sum(jnp.sum(x.reshape(-1, 8, cols), axis=0), axis=0, keepdims=True)

    def colmax(x):
        return jnp.max(jnp.max(x.reshape(-1, 8, cols), axis=0), axis=0, keepdims=True)

    def attend(kv, k_parts, vt_parts, bias):
        qs = []
        for g in range(GQA_GROUP):
            h = kv * GQA_GROUP + g
            qp = q_ref[:, 2 * hd * (h // 2):2 * hd * (h // 2 + 1)]
            qs.append(jnp.where(low if h % 2 == 0 else jnp.logical_not(low), qp, jnp.zeros_like(qp)))
        qs = jnp.concatenate(qs, axis=0)
        sink = jnp.concatenate(
            [jnp.concatenate([sink_ref[kv * GQA_GROUP + g:kv * GQA_GROUP + g + 1, :]] * (tq // 128), axis=1)
             for g in range(GQA_GROUP)], axis=1) * LOG2_E
        s = [_dot_t(kp, qs) for kp in k_parts]
        if bias is not None:
            s[0] = s[0] + bias
        m = sink
        for sp in s:
            m = jnp.maximum(m, colmax(sp))
        den = jnp.exp2(sink - m)
        p = []
        for sp in s:
            pp = jnp.exp2(sp - m)
            den = den + colsum(pp)
            p.append(pp.astype(BF16))
        o_t = _dot(jnp.concatenate(vt_parts, axis=1), jnp.concatenate(p, axis=0)) * (1.0 / den)
        for pair in range(GQA_GROUP // 2):
            both = jnp.concatenate([o_t[:, (2 * pair + i) * tq:(2 * pair + i + 1) * tq] for i in range(2)], axis=0)
            c0 = 2 * hd * (kv * GQA_GROUP // 2 + pair)
            o_ref[:, c0:c0 + 2 * hd] = both.T.astype(o_ref.dtype)

    def ctx_parts(kv):
        ks = kx_ref[:, 2 * hd * kv:2 * hd * (kv + 1)]
        vts = [vx_ref[c][hd * kv:hd * (kv + 1), :] for c in range(lc // vch)]
        return ks, vts

    @pl.when(j < n_q_lat)
    def _():
        start = pl.multiple_of(jnp.clip(j * tq - WINDOW, 0, t - band), 128)
        kpos = start + lax.broadcasted_iota(jnp.int32, (band, tq), 0)
        qpos = j * tq + lax.broadcasted_iota(jnp.int32, (band, tq), 1)
        bias = jnp.where(jnp.abs(kpos - qpos) > WINDOW, -jnp.inf, 0.0).astype(F32)
        bias = jnp.concatenate([bias] * GQA_GROUP, axis=1)
        c0 = start // vch
        for kv in range(n_kv):
            kx, vxs = ctx_parts(kv)
            kb = kl_ref[pl.ds(start, band), 2 * hd * kv:2 * hd * (kv + 1)]
            vbs = [vl_ref[c0 + c][hd * kv:hd * (kv + 1), :] for c in range(band // vch)]
            attend(kv, [kb, kx], vbs + vxs, bias)

    @pl.when(j >= n_q_lat)
    def _():
        for kv in range(n_kv):
            kx, vxs = ctx_parts(kv)
            attend(kv, [kx], vxs, None)


def win_attn(lay, q, kd, vt, sink_rows, n_kv, hd):
    nt, qw = q.shape
    kw = kd.shape[1]
    tq = Q_TILE_WIN
    vch = vt.shape[2]
    assert lay.t % tq == 0 and lay.lc % tq == 0 and lay.n_lat % lay.lc == 0 and tq % 128 == 0
    assert vch == 128 and WINDOW % vch == 0 and lay.lc % vch == 0 and lay.t >= tq + 2 * WINDOW
    nb, ncq = lay.t // tq, lay.lc // tq
    ctx_q0 = lay.n_lat // tq
    ctx_k0 = lay.n_lat // lay.lc

    def qblock(b, j):
        return jnp.where(j < nb, b * nb + j, ctx_q0 + b * ncq + (j - nb))

    qspec = pl.BlockSpec((tq, qw), lambda b, j: (qblock(b, j), 0))
    return pl.pallas_call(
        functools.partial(_win_attn_kernel, n_q_lat=nb, n_kv=n_kv, hd=hd),
        out_shape=jax.ShapeDtypeStruct((nt, qw), BF16),
        grid=(lay.b, nb + ncq),
        in_specs=[
            qspec,
            pl.BlockSpec((lay.t, kw), lambda b, j: (b, 0)),
            pl.BlockSpec((lay.lc, kw), lambda b, j: (ctx_k0 + b, 0)),
            pl.BlockSpec((lay.t // vch, vt.shape[1], vch), lambda b, j: (b, 0, 0)),
            pl.BlockSpec((lay.lc // vch, vt.shape[1], vch), lambda b, j: (ctx_k0 + b, 0, 0)),
            _const_spec(sink_rows.shape),
        ],
        out_specs=qspec,
        compiler_params=pltpu.CompilerParams(vmem_limit_bytes=V7X_VMEM_LIMIT),
        name="win_attn",
    )(q, kd, kd, vt, vt, sink_rows)


def _mix_out_kernel(x_ref, mod_ref, g_ref, w_ref, *rest):
    o_ref = rest[-1]
    parts = [r[...] for r in rest[:-1]]
    y = _dot(jnp.concatenate(parts, axis=1) if len(parts) > 1 else parts[0], w_ref[...])
    o_ref[...] = x_ref[...] + mod_ref[5:6, :] * _rms(y, g_ref[...])


def mix_out(lay, x_all, n_rows, mod_l, g_post, w_out, parts):
    d = lay.d
    tm = lay.tile((512, 256, 128))
    row = lambda i: (i, 0)
    return pl.pallas_call(
        _mix_out_kernel,
        out_shape=jax.ShapeDtypeStruct((n_rows, d), F32),
        grid=(n_rows // tm,),
        in_specs=[pl.BlockSpec((tm, d), row), _mod_spec(lay, tm), _const_spec((1, d)), _resident_spec(w_out.shape)]
        + [pl.BlockSpec((tm, p.shape[1]), row) for p in parts],
        out_specs=pl.BlockSpec((tm, d), row),
        name="mix_out",
    )(x_all, mod_l, g_post.reshape(1, d), w_out, *parts)


def _odd_in_kernel(x_ref, mod_ref, g_ref, w_ref, cos_ref, sin_ref, gq_ref, gk_ref, q_ref, k_ref, vt_ref, *, qw, kw, hd):
    u = _pre(x_ref[...], g_ref[...], mod_ref[3:4, :], mod_ref[4:5, :]).astype(BF16)
    cos, sin = cos_ref[...], sin_ref[...]

    def normed_rope(o, width, gains, out_ref, post):
        cg = cos * gains[0:1, :]
        sg = sin * gains[1:2, :]
        step = 2 * hd
        for c0 in range(0, width, step):
            x2 = _dot(u, w_ref[:, o + c0:o + c0 + step])
            xr2 = _dot(u, w_ref[:, o + width + c0:o + width + c0 + step])
            for h0 in range(0, step, hd):
                x, xr = x2[:, h0:h0 + hd], xr2[:, h0:h0 + hd]
                r = lax.rsqrt(jnp.mean(x * x, axis=-1, keepdims=True) + EPS) * post
                out_ref[:, c0 + h0:c0 + h0 + hd] = (r * (x * cg + xr * sg)).astype(out_ref.dtype)

    normed_rope(0, qw, gq_ref[...], q_ref, hd ** -0.5 * LOG2_E)
    normed_rope(2 * qw, kw, gk_ref[...], k_ref, 1.0)
    o = 2 * qw + 2 * kw
    v = _dot(u, w_ref[:, o:o + kw])
    ch = vt_ref.shape[2]
    for c in range(vt_ref.shape[0]):
        vt_ref[c] = v[c * ch:(c + 1) * ch, :].T.astype(BF16)


def odd_in(lay, x_all, mod_l, g_pre, w_cat, cos_t, sin_t, gq, gk, qw, kw, hd):
    d = lay.d
    tm = lay.tile((512, 256, 128))
    rb = lay.rope_block(tm)
    nt = lay.nt
    row = lambda i: (i, 0)
    ch = KEY_CHUNK
    assert tm % ch == 0
    return pl.pallas_call(
        functools.partial(_odd_in_kernel, qw=qw, kw=kw, hd=hd),
        out_shape=(
            jax.ShapeDtypeStruct((nt, qw), BF16),
            jax.ShapeDtypeStruct((nt, kw), BF16),
            jax.ShapeDtypeStruct((nt // ch, kw, ch), BF16),
        ),
        grid=(nt // tm,),
        in_specs=[
            pl.BlockSpec((tm, d), row),
            _mod_spec(lay, tm),
            _const_spec((1, d)),
            _resident_spec(w_cat.shape),
            pl.BlockSpec((tm, hd), lambda i: (rb(i), 0)),
            pl.BlockSpec((tm, hd), lambda i: (rb(i), 0)),
            _const_spec(gq.shape),
            _const_spec(gk.shape),
        ],
        out_specs=(pl.BlockSpec((tm, qw), row), pl.BlockSpec((tm, kw), row),
                   pl.BlockSpec((tm // ch, kw, ch), lambda i: (i, 0, 0))),
        compiler_params=pltpu.CompilerParams(vmem_limit_bytes=V7X_VMEM_LIMIT),
        name="odd_in",
    )(x_all, mod_l, g_pre.reshape(1, d), w_cat, cos_t, sin_t, gq, gk)


def _dense_attn_kernel(q_ref, kl_ref, kx_ref, vl_ref, vx_ref, o_ref, s_ref, m_ref, l_ref, acc_ref, *, n_q_lat, nb):
    j = pl.program_id(2)
    tq = q_ref.shape[0]
    t, hd = kl_ref.shape
    lc = kx_ref.shape[0]
    ch = vl_ref.shape[2]
    cols = GQA_GROUP * tq
    qs = jnp.concatenate([q_ref[:, g * hd:(g + 1) * hd] for g in range(GQA_GROUP)], axis=0)

    def sum8(x):
        return jnp.sum(x.reshape(-1, 8, cols), axis=0)

    def colmax(x):
        return jnp.max(jnp.max(x.reshape(-1, 8, cols), axis=0), axis=0, keepdims=True)

    sx = _dot_t(kx_ref[...], qs)
    m0 = colmax(sx)
    p = jnp.exp2(sx - m0)
    m_ref[...] = m0
    l_ref[...] = sum8(p)
    acc_ref[...] = _dot(jnp.concatenate([vx_ref[c] for c in range(lc // ch)], axis=1), p.astype(BF16))

    @pl.when(j < n_q_lat)
    def _():
        n_blocks = t // nb

        def produce(i):
            s = _dot_t(kl_ref[i * nb:(i + 1) * nb, :], qs)
            s_ref[i % 2] = s
            return colmax(s)

        m = m_ref[...]
        block_max = produce(0)
        for i in range(n_blocks):
            m_new = jnp.maximum(m, block_max)
            if i + 1 < n_blocks:
                block_max = produce(i + 1)
            alpha = jnp.exp2(m - m_new)
            p = jnp.exp2(s_ref[i % 2] - m_new)
            vt = jnp.concatenate([vl_ref[i * (nb // ch) + c] for c in range(nb // ch)], axis=1)
            l_ref[...] = l_ref[...] * alpha + sum8(p)
            acc_ref[...] = acc_ref[...] * alpha + _dot(vt, p.astype(BF16))
            m = m_new

    o_t = acc_ref[...] * (1.0 / jnp.sum(l_ref[...], axis=0, keepdims=True))
    for g in range(GQA_GROUP):
        o_ref[:, g * hd:(g + 1) * hd] = o_t[:, g * tq:(g + 1) * tq].T.astype(o_ref.dtype)


def _dense_attn_unshifted_kernel(q_ref, kl_ref, kx_ref, vl_ref, vx_ref, o_ref, l_ref, acc_ref, *, n_q_lat, nb):
    j = pl.program_id(2)
    tq = q_ref.shape[0]
    t, hd = kl_ref.shape
    lc = kx_ref.shape[0]
    ch = vl_ref.shape[2]
    cols = GQA_GROUP * tq
    qs = jnp.concatenate([q_ref[:, g * hd:(g + 1) * hd] for g in range(GQA_GROUP)], axis=0)

    def block(k, vts):
        p = jnp.exp2(_dot_t(k, qs))
        return jnp.sum(p.reshape(-1, 8, cols), axis=0), _dot(jnp.concatenate(vts, axis=1), p.astype(BF16))

    l8, acc = block(kx_ref[...], [vx_ref[c] for c in range(lc // ch)])
    l_ref[...] = l8
    acc_ref[...] = acc

    @pl.when(j < n_q_lat)
    def _():
        l8, acc = l_ref[...], acc_ref[...]
        for i in range(t // nb):
            dl, da = block(kl_ref[i * nb:(i + 1) * nb, :], [vl_ref[i * (nb // ch) + c] for c in range(nb // ch)])
            l8, acc = l8 + dl, acc + da
        l_ref[...] = l8
        acc_ref[...] = acc

    o_t = acc_ref[...] * (1.0 / jnp.sum(l_ref[...], axis=0, keepdims=True))
    for g in range(GQA_GROUP):
        o_ref[:, g * hd:(g + 1) * hd] = o_t[:, g * tq:(g + 1) * tq].T.astype(o_ref.dtype)


def dense_attn(lay, q, k, vt, hd, with_ctx_queries, shifted):
    nt, qw = q.shape
    tq = 128
    ch = vt.shape[2]
    assert lay.t % tq == 0 and lay.lc % tq == 0 and lay.n_lat % lay.lc == 0 and lay.t % ch == 0 and lay.lc % ch == 0
    n_kv = k.shape[1] // hd
    gw = GQA_GROUP * hd
    nql, nqc = lay.t // tq, lay.lc // tq
    ctx_q0 = lay.n_lat // tq
    ctx_k0 = lay.n_lat // lay.lc
    n_rows = nt if with_ctx_queries else lay.n_lat

    def qblock(b, kv, j):
        return (jnp.where(j < nql, b * nql + j, ctx_q0 + b * nqc + (j - nql)), kv)

    qspec = pl.BlockSpec((tq, gw), qblock)
    cols = GQA_GROUP * tq
    nb = 1024 if lay.t % 1024 == 0 else ch
    assert nb % ch == 0 and lay.t % nb == 0
    stats = [pltpu.VMEM((8, cols), F32), pltpu.VMEM((hd, cols), F32)]
    if shifted:
        body = _dense_attn_kernel
        scratch = [pltpu.VMEM((2, nb, cols), F32), pltpu.VMEM((1, cols), F32)] + stats
    else:
        body = _dense_attn_unshifted_kernel
        scratch = stats
    return pl.pallas_call(
        functools.partial(body, n_q_lat=nql, nb=nb),
        out_shape=jax.ShapeDtypeStruct((n_rows, qw), BF16),
        grid=(lay.b, n_kv, nql + (nqc if with_ctx_queries else 0)),
        in_specs=[
            qspec,
            pl.BlockSpec((lay.t, hd), lambda b, kv, j: (b, kv)),
            pl.BlockSpec((lay.lc, hd), lambda b, kv, j: (ctx_k0 + b, kv)),
            pl.BlockSpec((lay.t // ch, hd, ch), lambda b, kv, j: (b, kv, 0)),
            pl.BlockSpec((lay.lc // ch, hd, ch), lambda b, kv, j: (ctx_k0 + b, kv, 0)),
        ],
        out_specs=qspec,
        scratch_shapes=scratch,
        compiler_params=pltpu.CompilerParams(vmem_limit_bytes=V7X_VMEM_LIMIT),
        name="dense_attn" if shifted else "dense_attn_unshifted",
    )(q, k, k, vt, vt)


def _rot_cols(w):
    pairs = w.reshape(w.shape[0], -1, 2)
    return jnp.stack([-pairs[..., 1], pairs[..., 0]], axis=-1).reshape(w.shape)


def _dup_heads(w, hd):
    heads = w.reshape(w.shape[0], -1, 1, hd)
    return jnp.concatenate([heads, heads], axis=2).reshape(w.shape[0], -1)


def _block_diag(w):
    h, n, _ = w.shape
    eye = jnp.eye(h, dtype=w.dtype)
    return (eye[:, None, :, None] * w[:, :, None, :]).reshape(h * n, h * n)


def _rope_tables(t, hd, lanes, ident_rows):
    n_freq = hd // 4
    freq = ROPE_THETA ** (-jnp.arange(n_freq, dtype=F32) / n_freq)
    pos = jnp.arange(t)
    row = (pos // GRID_W).astype(F32)
    col = (pos % GRID_W).astype(F32)
    ang = jnp.concatenate([row[:, None] * freq, col[:, None] * freq], axis=-1)
    cos = jnp.repeat(jnp.cos(ang), 2, axis=-1)
    sin = jnp.repeat(jnp.sin(ang), 2, axis=-1)
    cos = jnp.concatenate([cos, jnp.ones((ident_rows, hd), F32)], axis=0)
    sin = jnp.concatenate([sin, jnp.zeros((ident_rows, hd), F32)], axis=0)
    reps = lanes // hd
    return jnp.tile(cos, (1, reps)), jnp.tile(sin, (1, reps))


def _swap_pairs(g):
    return g.reshape(-1, 2)[:, ::-1].reshape(g.shape)


def kernel(x, c, ctx, c_ctx, w_ada, b_ada, norm_pre, norm_post, ffn_w_gate, ffn_w_up, ffn_w_down, even_w_in,
           even_conv_w, even_conv_b, lru_w_a, lru_b_a, lru_w_x, lru_b_x, lru_lambda, attn_sink, even_w_out, odd_w_in,
           odd_q_norm, odd_k_norm, odd_w_out):
    b, t, d = x.shape
    lc = ctx.shape[1]
    depth = w_ada.shape[0]
    lay = _Layout(b, t, lc)
    lay.d = d
    assert b + 1 <= MOD_ROWS and w_ada.shape[2] == N_MOD * d

    lw = even_conv_w.shape[2]
    win_heads = attn_sink.shape[1]
    win_hd = (d - lw) // win_heads
    win_kv = win_heads // GQA_GROUP
    glb_hd = odd_q_norm.shape[1]
    glb_heads = d // glb_hd
    glb_kv = glb_heads // GQA_GROUP
    qw_e, kw_e = win_heads * win_hd, win_kv * win_hd
    qw_o, kw_o = glb_heads * glb_hd, glb_kv * glb_hd

    x_all = jnp.concatenate([x.reshape(b * t, d), ctx.reshape(b * lc, d)], axis=0)
    c_rows = jnp.concatenate([c, c_ctx[None, :], jnp.zeros((MOD_ROWS - b - 1, d), F32)], axis=0)
    mod = ada_mod(c_rows, w_ada, b_ada).reshape(depth, MOD_ROWS, N_MOD, d)

    tm = lay.tile((512, 256, 128))
    cos_e, sin_e = _rope_tables(t, win_hd, 2 * win_hd, tm)
    cos_o, sin_o = _rope_tables(t, glb_hd, glb_hd, tm)

    wg16, wu16, wd16 = ffn_w_gate.astype(BF16), ffn_w_up.astype(BF16), ffn_w_down.astype(BF16)

    for l in range(depth):
        last = l == depth - 1
        i = l // 2
        mod_l = mod[l]
        x_all = ffn_sublayer(lay, x_all, lay.nt, mod_l, 0, norm_pre[l, 0], norm_post[l, 0], wg16[l, 0], wu16[l, 0], wd16[l, 0])
        rows_out = lay.n_lat if last else lay.nt
        if l % 2 == 0:
            w = even_w_in[i]
            o1, o2, o3, o4 = 2 * lw, 2 * lw + qw_e, 2 * lw + qw_e + kw_e, 2 * lw + qw_e + 2 * kw_e
            wq = w[:, o1:o2]
            wk = w[:, o2:o3]
            w_cat = jnp.concatenate([
                w[:, :o1], wq, _rot_cols(wq), _dup_heads(wk, win_hd), _dup_heads(_rot_cols(wk), win_hd),
                w[:, o3:o4]], axis=1).astype(BF16)
            xa, gg, q, kd, vt = even_in(lay, x_all, mod_l, norm_pre[l, 1], w_cat, cos_e, sin_e, lw, qw_e, 2 * kw_e, kw_e,
                                        win_hd ** -0.5 * LOG2_E)
            spl_args = []
            for direction in range(2):
                w_gate = jnp.concatenate([_block_diag(lru_w_a[i, direction]), _block_diag(lru_w_x[i, direction])], axis=1).astype(BF16)
                b_gate = jnp.concatenate([lru_b_a[i, direction], lru_b_x[i, direction]])
                spl_args.append((w_gate, b_gate, lru_lambda[i, direction]))
            hf = lru_scan(lay, xa, even_conv_w[i], even_conv_b[i], *spl_args[0], reverse=False)
            ya = lru_scan(lay, xa, even_conv_w[i], even_conv_b[i], *spl_args[1], reverse=True, hf=hf, gg=gg)
            sink_rows = jnp.broadcast_to(attn_sink[i][:, None], (win_heads, 128)).astype(F32)
            ob = win_attn(lay, q, kd, vt, sink_rows, win_kv, win_hd)
            x_all = mix_out(lay, x_all, rows_out, mod_l, norm_post[l, 1], even_w_out[i].astype(BF16), [ya, ob])
        else:
            w = odd_w_in[i]
            wq, wk, wv = w[:, :qw_o], w[:, qw_o:qw_o + kw_o], w[:, qw_o + kw_o:]
            w_cat = jnp.concatenate([wq, _rot_cols(wq), wk, _rot_cols(wk), wv], axis=1).astype(BF16)
            gq = jnp.stack([odd_q_norm[i], _swap_pairs(odd_q_norm[i])])
            gk = jnp.stack([odd_k_norm[i], _swap_pairs(odd_k_norm[i])])
            q, k, vt = odd_in(lay, x_all, mod_l, norm_pre[l, 1], w_cat, cos_o, sin_o, gq, gk, qw_o, kw_o, glb_hd)
            bound = LOG2_E * glb_hd ** 0.5 * jnp.max(jnp.abs(odd_q_norm[i])) * jnp.max(jnp.abs(odd_k_norm[i]))
            o = lax.cond(
                bound <= MAX_UNSHIFTED_SCORE,
                functools.partial(dense_attn, lay, hd=glb_hd, with_ctx_queries=not last, shifted=False),
                functools.partial(dense_attn, lay, hd=glb_hd, with_ctx_queries=not last, shifted=True),
                q, k, vt)
            x_all = mix_out(lay, x_all, rows_out, mod_l, norm_post[l, 1], odd_w_out[i].astype(BF16), [o])
        x_all = ffn_sublayer(lay, x_all, rows_out, mod_l, 6, norm_pre[l, 2], norm_post[l, 2], wg16[l, 1], wu16[l, 1], wd16[l, 1])
    return x_all[:b * t].reshape(b, t, d)
```

```python
import functools

import jax
import jax.numpy as jnp
import numpy as np
from jax import lax
from jax.experimental import pallas as pl
from jax.experimental.pallas import tpu as pltpu

F32 = jnp.float32
BF16 = jnp.bfloat16

EPS = 1e-6
FFN_RES = 0.5
LRU_C = 8.0
WINDOW = 128
GRID_W = 64
ROPE_THETA = 10000.0
GQA_GROUP = 4
N_MOD = 9
MOD_ROWS = 16

V7X_VMEM_LIMIT = 56 * 1024 * 1024
Q_TILE_WIN = 256
VT_CHUNK = 128
LRU_CHUNK = 256
HALO = 8
KEY_CHUNK = 256
LOG2_E = 1.4426950408889634
MAX_UNSHIFTED_SCORE = 60.0


def _const_spec(shape):
    zeros = (0,) * len(shape)
    return pl.BlockSpec(shape, lambda *_: zeros)


def _resident_spec(shape):
    zeros = (0,) * len(shape)
    return pl.BlockSpec(shape, lambda *_: zeros, pipeline_mode=pl.Buffered(1))


def _dot(a, b):
    return jnp.dot(a, b, preferred_element_type=F32)


def _dot_t(a, b):
    return lax.dot_general(a, b, (((1,), (1,)), ((), ())), preferred_element_type=F32)


def _rms(x, g):
    return x * lax.rsqrt(jnp.mean(x * x, axis=-1, keepdims=True) + EPS) * g


def _pre(x, g, shift, scale):
    return _rms(x, g) * (1.0 + scale) + shift


def _ada_kernel(c_ref, w_ref, b_ref, o_ref):
    c = c_ref[...]
    s = c * jax.nn.sigmoid(c)
    o_ref[...] = _dot(s.astype(BF16), w_ref[...].astype(BF16)) + b_ref[...]


def ada_mod(c_rows, w_ada, b_ada):
    depth, d, n = w_ada.shape
    tn = 1024
    return pl.pallas_call(
        _ada_kernel,
        out_shape=jax.ShapeDtypeStruct((depth, MOD_ROWS, n), F32),
        grid=(depth, n // tn),
        in_specs=[
            pl.BlockSpec((MOD_ROWS, d), lambda l, j: (0, 0)),
            pl.BlockSpec((None, d, tn), lambda l, j: (l, 0, j)),
            pl.BlockSpec((None, 1, tn), lambda l, j: (l, 0, j)),
        ],
        out_specs=pl.BlockSpec((None, MOD_ROWS, tn), lambda l, j: (l, 0, j)),
        name="ada_mod",
    )(c_rows, w_ada, b_ada.reshape(depth, 1, n))


class _Layout:
    def __init__(self, b, t, lc):
        self.b, self.t, self.lc = b, t, lc
        self.n_lat = b * t
        self.nt = b * (t + lc)

    def tile(self, candidates):
        for tm in candidates:
            if self.t % tm == 0 and (self.b * self.lc) % tm == 0:
                return tm
        raise ValueError("no token tile divides both the latent and the context stream")

    def mod_row(self, tm):
        n_lat_tiles, per_b, b = self.n_lat // tm, self.t // tm, self.b
        return lambda i: jnp.where(i < n_lat_tiles, i // per_b, b)

    def rope_block(self, tm):
        n_lat_tiles, per_b = self.n_lat // tm, self.t // tm
        return lambda i: jnp.where(i < n_lat_tiles, i % per_b, per_b)


def _mod_spec(lay, tm):
    row = lay.mod_row(tm)
    return pl.BlockSpec((None, N_MOD, lay_d(lay)), lambda i: (row(i), 0, 0))


def lay_d(lay):
    return lay.d


def _ffn_kernel(x_ref, *rest, m0, ff_chunk, n_lat_tiles):
    x = x_ref[...]
    if n_lat_tiles is not None:
        x = jnp.where(pl.program_id(0) < n_lat_tiles, x, rest[0][...])
        rest = rest[1:]
    mod_ref, gpre_ref, gpost_ref, wg_ref, wu_ref, wd_ref = rest[:6]
    rest = rest[6:]
    o_ref, h_ref = rest[-2:]
    if len(rest) > 2:
        gmix_ref, wmix_ref = rest[:2]
        parts = [r[...] for r in rest[2:-2]]
        y = _dot(jnp.concatenate(parts, axis=1) if len(parts) > 1 else parts[0], wmix_ref[...])
        x = x + mod_ref[5:6, :] * _rms(y, gmix_ref[...])
    u = _pre(x, gpre_ref[...], mod_ref[m0:m0 + 1, :], mod_ref[m0 + 1:m0 + 2, :]).astype(BF16)
    d_ff = wg_ref.shape[1]
    for c0 in range(0, d_ff, ff_chunk):
        g = _dot(u, wg_ref[:, c0:c0 + ff_chunk])
        up = _dot(u, wu_ref[:, c0:c0 + ff_chunk])
        h_ref[:, c0:c0 + ff_chunk] = (g * jax.nn.sigmoid(g) * up).astype(BF16)
    y = _dot(h_ref[...], wd_ref[...])
    o_ref[...] = x + FFN_RES * mod_ref[m0 + 2:m0 + 3, :] * _rms(y, gpost_ref[...])


def ffn_sublayer(lay, x_all, n_rows, mod_l, m0, g_pre, g_post, wg, wu, wd, mix=None):
    d = lay.d
    d_ff = wg.shape[1]
    tm = lay.tile((512, 256, 128))
    row = lambda i: (i, 0)
    n_lat_tiles = None
    if isinstance(x_all, tuple):
        n_lat_tiles = lay.n_lat // tm
        in_specs = [pl.BlockSpec((tm, d), lambda i: (jnp.minimum(i, n_lat_tiles - 1), 0)),
                    pl.BlockSpec((tm, d), lambda i: (jnp.maximum(i - n_lat_tiles, 0), 0))]
        args = list(x_all)
    else:
        in_specs = [pl.BlockSpec((tm, d), row)]
        args = [x_all]
    in_specs += [
        _mod_spec(lay, tm),
        _const_spec((1, d)),
        _const_spec((1, d)),
        _resident_spec((d, d_ff)),
        _resident_spec((d, d_ff)),
        _resident_spec((d_ff, d)),
    ]
    args += [mod_l, g_pre.reshape(1, d), g_post.reshape(1, d), wg, wu, wd]
    if mix is not None:
        g_mix, w_mix, parts = mix
        in_specs += [_const_spec((1, d)), _resident_spec(w_mix.shape)] + [pl.BlockSpec((tm, p.shape[1]), row) for p in parts]
        args += [g_mix.reshape(1, d), w_mix] + list(parts)
    return pl.pallas_call(
        functools.partial(_ffn_kernel, m0=m0, ff_chunk=256, n_lat_tiles=n_lat_tiles),
        out_shape=jax.ShapeDtypeStruct((n_rows, d), F32),
        grid=(n_rows // tm,),
        in_specs=in_specs,
        out_specs=pl.BlockSpec((tm, d), row),
        scratch_shapes=[pltpu.VMEM((tm, d_ff), BF16)],
        compiler_params=pltpu.CompilerParams(vmem_limit_bytes=V7X_VMEM_LIMIT),
        name="ffn_sublayer" if mix is None else "mix_ffn_sublayer",
    )(*args)


def _even_in_kernel(x_ref, mod_ref, g_ref, w_ref, cos_ref, sin_ref, xa_ref, gg_ref, q_ref, k_ref, vt_ref, *, lw, qw, kw, vw,
                    q_scale):
    u = _pre(x_ref[...], g_ref[...], mod_ref[3:4, :], mod_ref[4:5, :]).astype(BF16)
    cos, sin = cos_ref[...], sin_ref[...]
    o = 0
    xa_ref[...] = _dot(u, w_ref[:, o:o + lw])
    o += lw
    gg_ref[...] = jax.nn.gelu(_dot(u, w_ref[:, o:o + lw]))
    o += lw
    q = _dot(u, w_ref[:, o:o + qw])
    qr = _dot(u, w_ref[:, o + qw:o + 2 * qw])
    o += 2 * qw
    reps = qw // cos.shape[1]
    q_ref[...] = ((q * jnp.concatenate([cos] * reps, axis=1) + qr * jnp.concatenate([sin] * reps, axis=1)) * q_scale).astype(BF16)
    k = _dot(u, w_ref[:, o:o + kw])
    kr = _dot(u, w_ref[:, o + kw:o + 2 * kw])
    o += 2 * kw
    reps = kw // cos.shape[1]
    k_ref[...] = (k * jnp.concatenate([cos] * reps, axis=1) + kr * jnp.concatenate([sin] * reps, axis=1)).astype(BF16)
    v = _dot(u, w_ref[:, o:o + vw])
    ch = vt_ref.shape[2]
    for c in range(vt_ref.shape[0]):
        vt_ref[c] = v[c * ch:(c + 1) * ch, :].T.astype(BF16)


def even_in(lay, x_all, mod_l, g_pre, w_cat, cos_t, sin_t, lw, qw, kw, vw, q_scale):
    d = lay.d
    tm = lay.tile((512, 256, 128))
    rb = lay.rope_block(tm)
    nt = lay.nt
    row = lambda i: (i, 0)
    ch = VT_CHUNK
    assert tm % ch == 0
    return pl.pallas_call(
        functools.partial(_even_in_kernel, lw=lw, qw=qw, kw=kw, vw=vw, q_scale=q_scale),
        out_shape=(
            jax.ShapeDtypeStruct((nt, lw), F32),
            jax.ShapeDtypeStruct((nt, lw), F32),
            jax.ShapeDtypeStruct((nt, qw), BF16),
            jax.ShapeDtypeStruct((nt, kw), BF16),
            jax.ShapeDtypeStruct((nt // ch, vw, ch), BF16),
        ),
        grid=(nt // tm,),
        in_specs=[
            pl.BlockSpec((tm, d), row),
            _mod_spec(lay, tm),
            _const_spec((1, d)),
            _resident_spec(w_cat.shape),
            pl.BlockSpec((tm, cos_t.shape[1]), lambda i: (rb(i), 0)),
            pl.BlockSpec((tm, cos_t.shape[1]), lambda i: (rb(i), 0)),
        ],
        out_specs=(
            pl.BlockSpec((tm, lw), row),
            pl.BlockSpec((tm, lw), row),
            pl.BlockSpec((tm, qw), row),
            pl.BlockSpec((tm, kw), row),
            pl.BlockSpec((tm // ch, vw, ch), lambda i: (i, 0, 0)),
        ),
        compiler_params=pltpu.CompilerParams(vmem_limit_bytes=V7X_VMEM_LIMIT),
        name="even_in",
    )(x_all, mod_l, g_pre.reshape(1, d), w_cat, cos_t, sin_t)


def _shift_rows(x, dist, fill, reverse):
    n = x.shape[0]
    rows = lax.broadcasted_iota(jnp.int32, x.shape, 0)
    if reverse:
        return jnp.where(rows < n - dist, pltpu.roll(x, n - dist, 0), fill)
    return jnp.where(rows >= dist, pltpu.roll(x, dist, 0), fill)


def _lru_kernel(xm_ref, xl_ref, xr_ref, cw_ref, cb_ref, wg_ref, bg_ref, lam_ref, *rest,
                reverse, n_ctx_chunks, n_lat_chunks):
    if reverse:
        hf_ref, gg_ref, o_ref, xbuf, carry = rest
    else:
        o_ref, xbuf, carry = rest
    s = pl.program_id(1)
    cl, lw = xm_ref.shape
    in_ctx = s < n_ctx_chunks
    pos = jnp.where(in_ctx, s, s - n_ctx_chunks)
    n_seq = jnp.where(in_ctx, n_ctx_chunks, n_lat_chunks)
    chunk = (n_seq - 1 - pos) if reverse else pos

    @pl.when(s == 0)
    def _():
        carry[...] = jnp.zeros_like(carry)

    xbuf[0:HALO, :] = jnp.where(chunk > 0, xl_ref[...], 0.0)
    xbuf[HALO:HALO + cl, :] = xm_ref[...]
    xbuf[HALO + cl:, :] = jnp.where(chunk < n_seq - 1, xr_ref[...], 0.0)
    taps = cw_ref.shape[0]
    left = taps // 2
    xin = cb_ref[...] + xbuf[pl.ds(HALO - left, cl), :] * cw_ref[0:1, :]
    for t in range(1, taps):
        xin = xin + xbuf[pl.ds(HALO - left + t, cl), :] * cw_ref[t:t + 1, :]

    gates = _dot(xin.astype(BF16), wg_ref[...]) + bg_ref[...]
    r = jax.nn.sigmoid(gates[:, :lw])
    gate_i = jax.nn.sigmoid(gates[:, lw:])
    log_a = (-LRU_C) * r * jax.nn.softplus(-lam_ref[...])
    a = jnp.exp(log_a)
    b = jnp.sqrt(-jnp.tanh(log_a) * (a * a + 1.0)) * gate_i * xin

    dist = 1
    while dist < cl:
        b = b + a * _shift_rows(b, dist, 0.0, reverse)
        a = a * _shift_rows(a, dist, 1.0, reverse)
        dist *= 2
    h = b + a * carry[...]
    carry[...] = h[0:1, :] if reverse else h[cl - 1:cl, :]
    if reverse:
        o_ref[...] = ((hf_ref[...] + h) * gg_ref[...]).astype(o_ref.dtype)
    else:
        o_ref[...] = h


def lru_scan(lay, xa, conv_w, conv_b, w_gate, b_gate, lam, reverse, hf=None, gg=None):
    nt, lw = xa.shape
    cl = LRU_CHUNK
    assert lay.t % cl == 0 and lay.lc % cl == 0
    ncc, nlc = lay.lc // cl, lay.t // cl
    n_lat_blocks = lay.n_lat // cl

    def chunk_block(b, s):
        in_ctx = s < ncc
        pos = jnp.where(in_ctx, s, s - ncc)
        if reverse:
            pos = jnp.where(in_ctx, ncc - 1 - pos, nlc - 1 - pos)
        return jnp.where(in_ctx, n_lat_blocks + b * ncc + pos, b * nlc + pos)

    per = cl // HALO
    last_halo = nt // HALO - 1
    main = pl.BlockSpec((cl, lw), lambda b, s: (chunk_block(b, s), 0))
    in_specs = [
        main,
        pl.BlockSpec((HALO, lw), lambda b, s: (jnp.maximum(chunk_block(b, s) * per - 1, 0), 0)),
        pl.BlockSpec((HALO, lw), lambda b, s: (jnp.minimum((chunk_block(b, s) + 1) * per, last_halo), 0)),
        _const_spec(conv_w.shape),
        _const_spec((1, lw)),
        _const_spec(w_gate.shape),
        _const_spec((1, 2 * lw)),
        _const_spec((1, lw)),
    ]
    args = [xa, xa, xa, conv_w, conv_b.reshape(1, lw), w_gate, b_gate.reshape(1, 2 * lw), lam.reshape(1, lw)]
    if reverse:
        in_specs += [main, main]
        args += [hf, gg]
    return pl.pallas_call(
        functools.partial(_lru_kernel, reverse=reverse, n_ctx_chunks=ncc, n_lat_chunks=nlc),
        out_shape=jax.ShapeDtypeStruct((nt, lw), BF16 if reverse else F32),
        grid=(lay.b, ncc + nlc),
        in_specs=in_specs,
        out_specs=main,
        scratch_shapes=[pltpu.VMEM((cl + 2 * HALO, lw), F32), pltpu.VMEM((1, lw), F32)],
        compiler_params=pltpu.CompilerParams(dimension_semantics=("arbitrary", "arbitrary")),
        name="lru_bwd" if reverse else "lru_fwd",
    )(*args)


def _win_attn_kernel(q_ref, kl_ref, kx_ref, vl_ref, vx_ref, sink_ref, o_ref, *, n_q_lat, n_kv, hd):
    j = pl.program_id(1)
    tq = q_ref.shape[0]
    t = kl_ref.shape[0]
    lc = kx_ref.shape[0]
    vch = vl_ref.shape[2]
    band = tq + 2 * WINDOW
    cols = GQA_GROUP * tq
    low = lax.broadcasted_iota(jnp.int32, (tq, 2 * hd), 1) < hd

    def colsum(x):
        return jnp.sum(jnp.sum(x.reshape(-1, 8, cols), axis=0), axis=0, keepdims=True)

    def colmax(x):
        return jnp.max(jnp.max(x.reshape(-1, 8, cols), axis=0), axis=0, keepdims=True)

    def attend(kv, k_parts, vt_parts, bias):
        qs = []
        for g in range(GQA_GROUP):
            h = kv * GQA_GROUP + g
            qp = q_ref[:, 2 * hd * (h // 2):2 * hd * (h // 2 + 1)]
            qs.append(jnp.where(low if h % 2 == 0 else jnp.logical_not(low), qp, jnp.zeros_like(qp)))
        qs = jnp.concatenate(qs, axis=0)
        sink = jnp.concatenate(
            [jnp.concatenate([sink_ref[kv * GQA_GROUP + g:kv * GQA_GROUP + g + 1, :]] * (tq // 128), axis=1)
             for g in range(GQA_GROUP)], axis=1) * LOG2_E
        s = [_dot_t(kp, qs) for kp in k_parts]
        if bias is not None:
            s[0] = s[0] + bias
        m = sink
        for sp in s:
            m = jnp.maximum(m, colmax(sp))
        den = jnp.exp2(sink - m)
        p = []
        for sp in s:
            pp = jnp.exp2(sp - m)
            den = den + colsum(pp)
            p.append(pp.astype(BF16))
        o_t = _dot(jnp.concatenate(vt_parts, axis=1), jnp.concatenate(p, axis=0)) * (1.0 / den)
        for pair in range(GQA_GROUP // 2):
            both = jnp.concatenate([o_t[:, (2 * pair + i) * tq:(2 * pair + i + 1) * tq] for i in range(2)], axis=0)
            c0 = 2 * hd * (kv * GQA_GROUP // 2 + pair)
            o_ref[:, c0:c0 + 2 * hd] = both.T.astype(o_ref.dtype)

    def ctx_parts(kv):
        ks = kx_ref[:, 2 * hd * kv:2 * hd * (kv + 1)]
        vts = [vx_ref[c][hd * kv:hd * (kv + 1), :] for c in range(lc // vch)]
        return ks, vts

    @pl.when(j < n_q_lat)
    def _():
        start = pl.multiple_of(jnp.clip(j * tq - WINDOW, 0, t - band), 128)
        kpos = start + lax.broadcasted_iota(jnp.int32, (band, tq), 0)
        qpos = j * tq + lax.broadcasted_iota(jnp.int32, (band, tq), 1)
        bias = jnp.where(jnp.abs(kpos - qpos) > WINDOW, -jnp.inf, 0.0).astype(F32)
        bias = jnp.concatenate([bias] * GQA_GROUP, axis=1)
        c0 = start // vch
        for kv in range(n_kv):
            kx, vxs = ctx_parts(kv)
            kb = kl_ref[pl.ds(start, band), 2 * hd * kv:2 * hd * (kv + 1)]
            vbs = [vl_ref[c0 + c][hd * kv:hd * (kv + 1), :] for c in range(band // vch)]
            attend(kv, [kb, kx], vbs + vxs, bias)

    @pl.when(j >= n_q_lat)
    def _():
        for kv in range(n_kv):
            kx, vxs = ctx_parts(kv)
            attend(kv, [kx], vxs, None)


def win_attn(lay, q, kd, vt, sink_rows, n_kv, hd):
    nt, qw = q.shape
    kw = kd.shape[1]
    tq = Q_TILE_WIN
    vch = vt.shape[2]
    assert lay.t % tq == 0 and lay.lc % tq == 0 and lay.n_lat % lay.lc == 0 and tq % 128 == 0
    assert vch == 128 and WINDOW % vch == 0 and lay.lc % vch == 0 and lay.t >= tq + 2 * WINDOW
    nb, ncq = lay.t // tq, lay.lc // tq
    ctx_q0 = lay.n_lat // tq
    ctx_k0 = lay.n_lat // lay.lc

    def qblock(b, j):
        return jnp.where(j < nb, b * nb + j, ctx_q0 + b * ncq + (j - nb))

    qspec = pl.BlockSpec((tq, qw), lambda b, j: (qblock(b, j), 0))
    return pl.pallas_call(
        functools.partial(_win_attn_kernel, n_q_lat=nb, n_kv=n_kv, hd=hd),
        out_shape=jax.ShapeDtypeStruct((nt, qw), BF16),
        grid=(lay.b, nb + ncq),
        in_specs=[
            qspec,
            pl.BlockSpec((lay.t, kw), lambda b, j: (b, 0)),
            pl.BlockSpec((lay.lc, kw), lambda b, j: (ctx_k0 + b, 0)),
            pl.BlockSpec((lay.t // vch, vt.shape[1], vch), lambda b, j: (b, 0, 0)),
            pl.BlockSpec((lay.lc // vch, vt.shape[1], vch), lambda b, j: (ctx_k0 + b, 0, 0)),
            _const_spec(sink_rows.shape),
        ],
        out_specs=qspec,
        compiler_params=pltpu.CompilerParams(vmem_limit_bytes=V7X_VMEM_LIMIT),
        name="win_attn",
    )(q, kd, kd, vt, vt, sink_rows)


def _odd_in_kernel(x_ref, mod_ref, g_ref, w_ref, cos_ref, sin_ref, gq_ref, gk_ref, q_ref, k_ref, vt_ref, *, qw, kw, hd):
    u = _pre(x_ref[...], g_ref[...], mod_ref[3:4, :], mod_ref[4:5, :]).astype(BF16)
    cos, sin = cos_ref[...], sin_ref[...]

    def normed_rope(o, width, gains, out_ref, post):
        cg = cos * gains[0:1, :]
        sg = sin * gains[1:2, :]
        step = 2 * hd
        for c0 in range(0, width, step):
            x2 = _dot(u, w_ref[:, o + c0:o + c0 + step])
            xr2 = _dot(u, w_ref[:, o + width + c0:o + width + c0 + step])
            for h0 in range(0, step, hd):
                x, xr = x2[:, h0:h0 + hd], xr2[:, h0:h0 + hd]
                r = lax.rsqrt(jnp.mean(x * x, axis=-1, keepdims=True) + EPS) * post
                out_ref[:, c0 + h0:c0 + h0 + hd] = (r * (x * cg + xr * sg)).astype(out_ref.dtype)

    normed_rope(0, qw, gq_ref[...], q_ref, hd ** -0.5 * LOG2_E)
    normed_rope(2 * qw, kw, gk_ref[...], k_ref, 1.0)
    o = 2 * qw + 2 * kw
    v = _dot(u, w_ref[:, o:o + kw])
    ch = vt_ref.shape[2]
    for c in range(vt_ref.shape[0]):
        vt_ref[c] = v[c * ch:(c + 1) * ch, :].T.astype(BF16)


def odd_in(lay, x_all, mod_l, g_pre, w_cat, cos_t, sin_t, gq, gk, qw, kw, hd):
    d = lay.d
    tm = lay.tile((512, 256, 128))
    rb = lay.rope_block(tm)
    nt = lay.nt
    row = lambda i: (i, 0)
    ch = KEY_CHUNK
    assert tm % ch == 0
    return pl.pallas_call(
        functools.partial(_odd_in_kernel, qw=qw, kw=kw, hd=hd),
        out_shape=(
            jax.ShapeDtypeStruct((nt, qw), BF16),
            jax.ShapeDtypeStruct((nt, kw), BF16),
            jax.ShapeDtypeStruct((nt // ch, kw, ch), BF16),
        ),
        grid=(nt // tm,),
        in_specs=[
            pl.BlockSpec((tm, d), row),
            _mod_spec(lay, tm),
            _const_spec((1, d)),
            _resident_spec(w_cat.shape),
            pl.BlockSpec((tm, hd), lambda i: (rb(i), 0)),
            pl.BlockSpec((tm, hd), lambda i: (rb(i), 0)),
            _const_spec(gq.shape),
            _const_spec(gk.shape),
        ],
        out_specs=(pl.BlockSpec((tm, qw), row), pl.BlockSpec((tm, kw), row),
                   pl.BlockSpec((tm // ch, kw, ch), lambda i: (i, 0, 0))),
        compiler_params=pltpu.CompilerParams(vmem_limit_bytes=V7X_VMEM_LIMIT),
        name="odd_in",
    )(x_all, mod_l, g_pre.reshape(1, d), w_cat, cos_t, sin_t, gq, gk)


def _dense_attn_kernel(q_ref, kl_ref, kx_ref, vl_ref, vx_ref, o_ref, s_ref, m_ref, l_ref, acc_ref, *, n_q_lat, nb):
    j = pl.program_id(2)
    tq = q_ref.shape[0]
    t, hd = kl_ref.shape
    lc = kx_ref.shape[0]
    ch = vl_ref.shape[2]
    cols = GQA_GROUP * tq
    qs = jnp.concatenate([q_ref[:, g * hd:(g + 1) * hd] for g in range(GQA_GROUP)], axis=0)

    def sum8(x):
        return jnp.sum(x.reshape(-1, 8, cols), axis=0)

    def colmax(x):
        return jnp.max(jnp.max(x.reshape(-1, 8, cols), axis=0), axis=0, keepdims=True)

    sx = _dot_t(kx_ref[...], qs)
    m0 = colmax(sx)
    p = jnp.exp2(sx - m0)
    m_ref[...] = m0
    l_ref[...] = sum8(p)
    acc_ref[...] = _dot(jnp.concatenate([vx_ref[c] for c in range(lc // ch)], axis=1), p.astype(BF16))

    @pl.when(j < n_q_lat)
    def _():
        n_blocks = t // nb

        def produce(i):
            s = _dot_t(kl_ref[i * nb:(i + 1) * nb, :], qs)
            s_ref[i % 2] = s
            return colmax(s)

        m = m_ref[...]
        block_max = produce(0)
        for i in range(n_blocks):
            m_new = jnp.maximum(m, block_max)
            if i + 1 < n_blocks:
                block_max = produce(i + 1)
            alpha = jnp.exp2(m - m_new)
            p = jnp.exp2(s_ref[i % 2] - m_new)
            vt = jnp.concatenate([vl_ref[i * (nb // ch) + c] for c in range(nb // ch)], axis=1)
            l_ref[...] = l_ref[...] * alpha + sum8(p)
            acc_ref[...] = acc_ref[...] * alpha + _dot(vt, p.astype(BF16))
            m = m_new

    o_t = acc_ref[...] * (1.0 / jnp.sum(l_ref[...], axis=0, keepdims=True))
    for g in range(GQA_GROUP):
        o_ref[:, g * hd:(g + 1) * hd] = o_t[:, g * tq:(g + 1) * tq].T.astype(o_ref.dtype)


def _dense_attn_unshifted_kernel(q_ref, kl_ref, kx_ref, vl_ref, vx_ref, o_ref, l_ref, acc_ref, *, n_q_lat, nb):
    j = pl.program_id(2)
    tq = q_ref.shape[0]
    t, hd = kl_ref.shape
    lc = kx_ref.shape[0]
    ch = vl_ref.shape[2]
    cols = GQA_GROUP * tq
    qs = jnp.concatenate([q_ref[:, g * hd:(g + 1) * hd] for g in range(GQA_GROUP)], axis=0)

    def block(k, vts):
        p = jnp.exp2(_dot_t(k, qs))
        return jnp.sum(p.reshape(-1, 8, cols), axis=0), _dot(jnp.concatenate(vts, axis=1), p.astype(BF16))

    l8, acc = block(kx_ref[...], [vx_ref[c] for c in range(lc // ch)])
    l_ref[...] = l8
    acc_ref[...] = acc

    @pl.when(j < n_q_lat)
    def _():
        l8, acc = l_ref[...], acc_ref[...]
        for i in range(t // nb):
            dl, da = block(kl_ref[i * nb:(i + 1) * nb, :], [vl_ref[i * (nb // ch) + c] for c in range(nb // ch)])
            l8, acc = l8 + dl, acc + da
        l_ref[...] = l8
        acc_ref[...] = acc

    o_t = acc_ref[...] * (1.0 / jnp.sum(l_ref[...], axis=0, keepdims=True))
    for g in range(GQA_GROUP):
        o_ref[:, g * hd:(g + 1) * hd] = o_t[:, g * tq:(g + 1) * tq].T.astype(o_ref.dtype)


def dense_attn(lay, q, k, vt, hd, with_ctx_queries, shifted):
    nt, qw = q.shape
    tq = 128 if shifted else 256
    ch = vt.shape[2]
    assert lay.t % tq == 0 and lay.lc % tq == 0 and lay.n_lat % lay.lc == 0 and lay.t % ch == 0 and lay.lc % ch == 0
    n_kv = k.shape[1] // hd
    gw = GQA_GROUP * hd
    nql, nqc = lay.t // tq, lay.lc // tq
    ctx_q0 = lay.n_lat // tq
    ctx_k0 = lay.n_lat // lay.lc
    n_rows = nt if with_ctx_queries else lay.n_lat

    def qblock(b, kv, j):
        return (jnp.where(j < nql, b * nql + j, ctx_q0 + b * nqc + (j - nql)), kv)

    qspec = pl.BlockSpec((tq, gw), qblock)
    cols = GQA_GROUP * tq
    nb = 1024 if lay.t % 1024 == 0 else ch
    assert nb % ch == 0 and lay.t % nb == 0
    stats = [pltpu.VMEM((8, cols), F32), pltpu.VMEM((hd, cols), F32)]
    if shifted:
        body = _dense_attn_kernel
        scratch = [pltpu.VMEM((2, nb, cols), F32), pltpu.VMEM((1, cols), F32)] + stats
    else:
        body = _dense_attn_unshifted_kernel
        scratch = stats
    return pl.pallas_call(
        functools.partial(body, n_q_lat=nql, nb=nb),
        out_shape=jax.ShapeDtypeStruct((n_rows, qw), BF16),
        grid=(lay.b, n_kv, nql + (nqc if with_ctx_queries else 0)),
        in_specs=[
            qspec,
            pl.BlockSpec((lay.t, hd), lambda b, kv, j: (b, kv)),
            pl.BlockSpec((lay.lc, hd), lambda b, kv, j: (ctx_k0 + b, kv)),
            pl.BlockSpec((lay.t // ch, hd, ch), lambda b, kv, j: (b, kv, 0)),
            pl.BlockSpec((lay.lc // ch, hd, ch), lambda b, kv, j: (ctx_k0 + b, kv, 0)),
        ],
        out_specs=qspec,
        scratch_shapes=scratch,
        compiler_params=pltpu.CompilerParams(vmem_limit_bytes=V7X_VMEM_LIMIT),
        name="dense_attn" if shifted else "dense_attn_unshifted",
    )(q, k, k, vt, vt)


def _rot_cols(w):
    pairs = w.reshape(w.shape[0], -1, 2)
    return jnp.stack([-pairs[..., 1], pairs[..., 0]], axis=-1).reshape(w.shape)


def _dup_heads(w, hd):
    heads = w.reshape(w.shape[0], -1, 1, hd)
    return jnp.concatenate([heads, heads], axis=2).reshape(w.shape[0], -1)


def _block_diag(w):
    h, n, _ = w.shape
    eye = jnp.eye(h, dtype=w.dtype)
    return (eye[:, None, :, None] * w[:, :, None, :]).reshape(h * n, h * n)


def _rope_tables(t, hd, lanes, ident_rows):
    n_freq = hd // 4
    freq = ROPE_THETA ** (-jnp.arange(n_freq, dtype=F32) / n_freq)
    pos = jnp.arange(t)
    row = (pos // GRID_W).astype(F32)
    col = (pos % GRID_W).astype(F32)
    ang = jnp.concatenate([row[:, None] * freq, col[:, None] * freq], axis=-1)
    cos = jnp.repeat(jnp.cos(ang), 2, axis=-1)
    sin = jnp.repeat(jnp.sin(ang), 2, axis=-1)
    cos = jnp.concatenate([cos, jnp.ones((ident_rows, hd), F32)], axis=0)
    sin = jnp.concatenate([sin, jnp.zeros((ident_rows, hd), F32)], axis=0)
    reps = lanes // hd
    return jnp.tile(cos, (1, reps)), jnp.tile(sin, (1, reps))


def _swap_pairs(g):
    return g.reshape(-1, 2)[:, ::-1].reshape(g.shape)


def kernel(x, c, ctx, c_ctx, w_ada, b_ada, norm_pre, norm_post, ffn_w_gate, ffn_w_up, ffn_w_down, even_w_in,
           even_conv_w, even_conv_b, lru_w_a, lru_b_a, lru_w_x, lru_b_x, lru_lambda, attn_sink, even_w_out, odd_w_in,
           odd_q_norm, odd_k_norm, odd_w_out):
    b, t, d = x.shape
    lc = ctx.shape[1]
    depth = w_ada.shape[0]
    lay = _Layout(b, t, lc)
    lay.d = d
    assert b + 1 <= MOD_ROWS and w_ada.shape[2] == N_MOD * d

    lw = even_conv_w.shape[2]
    win_heads = attn_sink.shape[1]
    win_hd = (d - lw) // win_heads
    win_kv = win_heads // GQA_GROUP
    glb_hd = odd_q_norm.shape[1]
    glb_heads = d // glb_hd
    glb_kv = glb_heads // GQA_GROUP
    qw_e, kw_e = win_heads * win_hd, win_kv * win_hd
    qw_o, kw_o = glb_heads * glb_hd, glb_kv * glb_hd

    x_all = (x.reshape(b * t, d), ctx.reshape(b * lc, d))
    c_rows = jnp.concatenate([c, c_ctx[None, :], jnp.zeros((MOD_ROWS - b - 1, d), F32)], axis=0)
    mod = ada_mod(c_rows, w_ada, b_ada).reshape(depth, MOD_ROWS, N_MOD, d)

    tm = lay.tile((512, 256, 128))
    cos_e, sin_e = _rope_tables(t, win_hd, 2 * win_hd, tm)
    cos_o, sin_o = _rope_tables(t, glb_hd, glb_hd, tm)

    wg16, wu16, wd16 = ffn_w_gate.astype(BF16), ffn_w_up.astype(BF16), ffn_w_down.astype(BF16)

    for l in range(depth):
        last = l == depth - 1
        i = l // 2
        mod_l = mod[l]
        x_all = ffn_sublayer(lay, x_all, lay.nt, mod_l, 0, norm_pre[l, 0], norm_post[l, 0], wg16[l, 0], wu16[l, 0], wd16[l, 0])
        rows_out = lay.n_lat if last else lay.nt
        if l % 2 == 0:
            w = even_w_in[i]
            o1, o2, o3, o4 = 2 * lw, 2 * lw + qw_e, 2 * lw + qw_e + kw_e, 2 * lw + qw_e + 2 * kw_e
            wq = w[:, o1:o2]
            wk = w[:, o2:o3]
            w_cat = jnp.concatenate([
                w[:, :o1], wq, _rot_cols(wq), _dup_heads(wk, win_hd), _dup_heads(_rot_cols(wk), win_hd),
                w[:, o3:o4]], axis=1).astype(BF16)
            xa, gg, q, kd, vt = even_in(lay, x_all, mod_l, norm_pre[l, 1], w_cat, cos_e, sin_e, lw, qw_e, 2 * kw_e, kw_e,
                                        win_hd ** -0.5 * LOG2_E)
            spl_args = []
            for direction in range(2):
                w_gate = jnp.concatenate([_block_diag(lru_w_a[i, direction]), _block_diag(lru_w_x[i, direction])], axis=1).astype(BF16)
                b_gate = jnp.concatenate([lru_b_a[i, direction], lru_b_x[i, direction]])
                spl_args.append((w_gate, b_gate, lru_lambda[i, direction]))
            hf = lru_scan(lay, xa, even_conv_w[i], even_conv_b[i], *spl_args[0], reverse=False)
            ya = lru_scan(lay, xa, even_conv_w[i], even_conv_b[i], *spl_args[1], reverse=True, hf=hf, gg=gg)
            sink_rows = jnp.broadcast_to(attn_sink[i][:, None], (win_heads, 128)).astype(F32)
            ob = win_attn(lay, q, kd, vt, sink_rows, win_kv, win_hd)
            mix = (norm_post[l, 1], even_w_out[i].astype(BF16), [ya, ob])
        else:
            w = odd_w_in[i]
            wq, wk, wv = w[:, :qw_o], w[:, qw_o:qw_o + kw_o], w[:, qw_o + kw_o:]
            w_cat = jnp.concatenate([wq, _rot_cols(wq), wk, _rot_cols(wk), wv], axis=1).astype(BF16)
            gq = jnp.stack([odd_q_norm[i], _swap_pairs(odd_q_norm[i])])
            gk = jnp.stack([odd_k_norm[i], _swap_pairs(odd_k_norm[i])])
            q, k, vt = odd_in(lay, x_all, mod_l, norm_pre[l, 1], w_cat, cos_o, sin_o, gq, gk, qw_o, kw_o, glb_hd)
            bound = LOG2_E * glb_hd ** 0.5 * jnp.max(jnp.abs(odd_q_norm[i])) * jnp.max(jnp.abs(odd_k_norm[i]))
            o = lax.cond(
                bound <= MAX_UNSHIFTED_SCORE,
                functools.partial(dense_attn, lay, hd=glb_hd, with_ctx_queries=not last, shifted=False),
                functools.partial(dense_attn, lay, hd=glb_hd, with_ctx_queries=not last, shifted=True),
                q, k, vt)
            mix = (norm_post[l, 1], odd_w_out[i].astype(BF16), [o])
        x_all = ffn_sublayer(lay, x_all, rows_out, mod_l, 6, norm_pre[l, 2], norm_post[l, 2], wg16[l, 1], wu16[l, 1], wd16[l, 1],
                             mix=mix)
    return x_all[:b * t].reshape(b, t, d)
```

```python
import functools

import jax
import jax.numpy as jnp
import numpy as np
from jax import lax
from jax.experimental import pallas as pl
from jax.experimental.pallas import tpu as pltpu

F32 = jnp.float32
BF16 = jnp.bfloat16

EPS = 1e-6
FFN_RES = 0.5
LRU_C = 8.0
WINDOW = 128
GRID_W = 64
ROPE_THETA = 10000.0
GQA_GROUP = 4
N_MOD = 9
MOD_ROWS = 16

V7X_VMEM_LIMIT = 56 * 1024 * 1024
Q_TILE_WIN = 256
VT_CHUNK = 128
LRU_CHUNK = 256
LANES = 128
SEG_PAD = 4
HALO = 8
FFN_SUBTILES = 2
KEY_CHUNK = 256
LOG2_E = 1.4426950408889634
MAX_UNSHIFTED_SCORE = 60.0


def _const_spec(shape):
    zeros = (0,) * len(shape)
    return pl.BlockSpec(shape, lambda *_: zeros)


def _resident_spec(shape):
    zeros = (0,) * len(shape)
    return pl.BlockSpec(shape, lambda *_: zeros, pipeline_mode=pl.Buffered(1))


def _dot(a, b):
    return jnp.dot(a, b, preferred_element_type=F32)


def _dot_t(a, b):
    return lax.dot_general(a, b, (((1,), (1,)), ((), ())), preferred_element_type=F32)


def _rms(x, g):
    return x * lax.rsqrt(jnp.mean(x * x, axis=-1, keepdims=True) + EPS) * g


def _pre(x, g, shift, scale):
    return _rms(x, g) * (1.0 + scale) + shift


def _ada_kernel(c_ref, w_ref, b_ref, o_ref):
    c = c_ref[...]
    s = c * jax.nn.sigmoid(c)
    o_ref[...] = _dot(s.astype(BF16), w_ref[...].astype(BF16)) + b_ref[...]


def ada_mod(c_rows, w_ada, b_ada):
    depth, d, n = w_ada.shape
    tn = 1024
    return pl.pallas_call(
        _ada_kernel,
        out_shape=jax.ShapeDtypeStruct((depth, MOD_ROWS, n), F32),
        grid=(depth, n // tn),
        in_specs=[
            pl.BlockSpec((MOD_ROWS, d), lambda l, j: (0, 0)),
            pl.BlockSpec((None, d, tn), lambda l, j: (l, 0, j)),
            pl.BlockSpec((None, 1, tn), lambda l, j: (l, 0, j)),
        ],
        out_specs=pl.BlockSpec((None, MOD_ROWS, tn), lambda l, j: (l, 0, j)),
        name="ada_mod",
    )(c_rows, w_ada, b_ada.reshape(depth, 1, n))


class _Layout:
    def __init__(self, b, t, lc):
        self.b, self.t, self.lc = b, t, lc
        self.n_lat = b * t
        self.nt = b * (t + lc)

    def tile(self, candidates):
        for tm in candidates:
            if self.t % tm == 0 and (self.b * self.lc) % tm == 0:
                return tm
        raise ValueError("no token tile divides both the latent and the context stream")

    def mod_row(self, tm):
        n_lat_tiles, per_b, b = self.n_lat // tm, self.t // tm, self.b
        return lambda i: jnp.where(i < n_lat_tiles, i // per_b, b)

    def rope_block(self, tm):
        n_lat_tiles, per_b = self.n_lat // tm, self.t // tm
        return lambda i: jnp.where(i < n_lat_tiles, i % per_b, per_b)


def _mod_spec(lay, tm):
    row = lay.mod_row(tm)
    return pl.BlockSpec((None, N_MOD, lay_d(lay)), lambda i: (row(i), 0, 0))


def lay_d(lay):
    return lay.d


def _ffn_kernel(x_ref, *rest, m0, ff_chunk, n_lat_tiles):
    ctx_ref = None
    if n_lat_tiles is not None:
        ctx_ref, rest = rest[0], rest[1:]
    mod_ref, gpre_ref, gpost_ref, wg_ref, wu_ref, wd_ref = rest[:6]
    rest = rest[6:]
    o_ref, h_ref = rest[-2:]
    tm = x_ref.shape[0]
    d_ff = wg_ref.shape[1]
    for r0 in range(0, tm, tm // FFN_SUBTILES):
        rows = slice(r0, r0 + tm // FFN_SUBTILES)
        x = x_ref[rows, :]
        if ctx_ref is not None:
            x = jnp.where(pl.program_id(0) < n_lat_tiles, x, ctx_ref[rows, :])
        if len(rest) > 2:
            gmix_ref, wmix_ref = rest[:2]
            parts = [r[rows, :] for r in rest[2:-2]]
            y = _dot(jnp.concatenate(parts, axis=1) if len(parts) > 1 else parts[0], wmix_ref[...])
            x = x + mod_ref[5:6, :] * _rms(y, gmix_ref[...])
        u = _pre(x, gpre_ref[...], mod_ref[m0:m0 + 1, :], mod_ref[m0 + 1:m0 + 2, :]).astype(BF16)
        for c0 in range(0, d_ff, ff_chunk):
            g = _dot(u, wg_ref[:, c0:c0 + ff_chunk])
            up = _dot(u, wu_ref[:, c0:c0 + ff_chunk])
            h_ref[rows, c0:c0 + ff_chunk] = (g * jax.nn.sigmoid(g) * up).astype(BF16)
        y = _dot(h_ref[rows, :], wd_ref[...])
        o_ref[rows, :] = x + FFN_RES * mod_ref[m0 + 2:m0 + 3, :] * _rms(y, gpost_ref[...])


def ffn_sublayer(lay, x_all, n_rows, mod_l, m0, g_pre, g_post, wg, wu, wd, which, mix=None):
    d = lay.d
    d_ff = wg.shape[-1]
    picked = lambda r, c: pl.BlockSpec((None, None, r, c), lambda i: which + (0, 0), pipeline_mode=pl.Buffered(1))
    tm = lay.tile((512, 256, 128))
    row = lambda i: (i, 0)
    n_lat_tiles = None
    if isinstance(x_all, tuple):
        n_lat_tiles = lay.n_lat // tm
        in_specs = [pl.BlockSpec((tm, d), lambda i: (jnp.minimum(i, n_lat_tiles - 1), 0)),
                    pl.BlockSpec((tm, d), lambda i: (jnp.maximum(i - n_lat_tiles, 0), 0))]
        args = list(x_all)
    else:
        in_specs = [pl.BlockSpec((tm, d), row)]
        args = [x_all]
    in_specs += [
        _mod_spec(lay, tm),
        _const_spec((1, d)),
        _const_spec((1, d)),
        picked(d, d_ff),
        picked(d, d_ff),
        picked(d_ff, d),
    ]
    args += [mod_l, g_pre.reshape(1, d), g_post.reshape(1, d), wg, wu, wd]
    if mix is not None:
        g_mix, w_mix, parts = mix
        in_specs += [_const_spec((1, d)), _resident_spec(w_mix.shape)] + [pl.BlockSpec((tm, p.shape[1]), row) for p in parts]
        args += [g_mix.reshape(1, d), w_mix] + list(parts)
    return pl.pallas_call(
        functools.partial(_ffn_kernel, m0=m0, ff_chunk=256, n_lat_tiles=n_lat_tiles),
        out_shape=jax.ShapeDtypeStruct((n_rows, d), F32),
        grid=(n_rows // tm,),
        in_specs=in_specs,
        out_specs=pl.BlockSpec((tm, d), row),
        scratch_shapes=[pltpu.VMEM((tm, d_ff), BF16)],
        compiler_params=pltpu.CompilerParams(vmem_limit_bytes=V7X_VMEM_LIMIT),
        name="ffn_sublayer" if mix is None else "mix_ffn_sublayer",
    )(*args)


def _even_in_kernel(x_ref, mod_ref, g_ref, w_ref, cos_ref, sin_ref, xa_ref, gg_ref, q_ref, k_ref, vt_ref, *, lw, qw, kw, vw,
                    q_scale):
    u = _pre(x_ref[...], g_ref[...], mod_ref[3:4, :], mod_ref[4:5, :]).astype(BF16)
    cos, sin = cos_ref[...], sin_ref[...]
    o = 0
    xa_ref[...] = _dot(u, w_ref[:, o:o + lw])
    o += lw
    gg_ref[...] = jax.nn.gelu(_dot(u, w_ref[:, o:o + lw]))
    o += lw
    q = _dot(u, w_ref[:, o:o + qw])
    qr = _dot(u, w_ref[:, o + qw:o + 2 * qw])
    o += 2 * qw
    reps = qw // cos.shape[1]
    q_ref[...] = ((q * jnp.concatenate([cos] * reps, axis=1) + qr * jnp.concatenate([sin] * reps, axis=1)) * q_scale).astype(BF16)
    k = _dot(u, w_ref[:, o:o + kw])
    kr = _dot(u, w_ref[:, o + kw:o + 2 * kw])
    o += 2 * kw
    reps = kw // cos.shape[1]
    k_ref[...] = (k * jnp.concatenate([cos] * reps, axis=1) + kr * jnp.concatenate([sin] * reps, axis=1)).astype(BF16)
    v = _dot(u, w_ref[:, o:o + vw])
    ch = vt_ref.shape[2]
    for c in range(vt_ref.shape[0]):
        vt_ref[c] = v[c * ch:(c + 1) * ch, :].T.astype(BF16)


def even_in(lay, x_all, mod_l, g_pre, w_cat, cos_t, sin_t, lw, qw, kw, vw, q_scale):
    d = lay.d
    tm = lay.tile((512, 256, 128))
    rb = lay.rope_block(tm)
    nt = lay.nt
    row = lambda i: (i, 0)
    ch = VT_CHUNK
    assert tm % ch == 0
    return pl.pallas_call(
        functools.partial(_even_in_kernel, lw=lw, qw=qw, kw=kw, vw=vw, q_scale=q_scale),
        out_shape=(
            jax.ShapeDtypeStruct((nt, lw), F32),
            jax.ShapeDtypeStruct((nt, lw), F32),
            jax.ShapeDtypeStruct((nt, qw), BF16),
            jax.ShapeDtypeStruct((nt, kw), BF16),
            jax.ShapeDtypeStruct((nt // ch, vw, ch), BF16),
        ),
        grid=(nt // tm,),
        in_specs=[
            pl.BlockSpec((tm, d), row),
            _mod_spec(lay, tm),
            _const_spec((1, d)),
            _resident_spec(w_cat.shape),
            pl.BlockSpec((tm, cos_t.shape[1]), lambda i: (rb(i), 0)),
            pl.BlockSpec((tm, cos_t.shape[1]), lambda i: (rb(i), 0)),
        ],
        out_specs=(
            pl.BlockSpec((tm, lw), row),
            pl.BlockSpec((tm, lw), row),
            pl.BlockSpec((tm, qw), row),
            pl.BlockSpec((tm, kw), row),
            pl.BlockSpec((tm // ch, vw, ch), lambda i: (i, 0, 0)),
        ),
        compiler_params=pltpu.CompilerParams(vmem_limit_bytes=V7X_VMEM_LIMIT),
        name="even_in",
    )(x_all, mod_l, g_pre.reshape(1, d), w_cat, cos_t, sin_t)


def _shift_rows(x, dist, fill, reverse):
    n = x.shape[0]
    rows = lax.broadcasted_iota(jnp.int32, x.shape, 0)
    if reverse:
        return jnp.where(rows < n - dist, pltpu.roll(x, n - dist, 0), fill)
    return jnp.where(rows >= dist, pltpu.roll(x, dist, 0), fill)


def _lru_kernel(xm_ref, xl_ref, xr_ref, cw_ref, cb_ref, wg_ref, bg_ref, lam_ref, *rest,
                reverse, n_ctx_chunks, n_lat_chunks):
    if reverse:
        hf_ref, gg_ref, o_ref, xbuf, carry, a_s, b_s, h_s = rest
    else:
        o_ref, xbuf, carry, a_s, b_s, h_s = rest
    s = pl.program_id(1)
    cl, lw = xm_ref.shape
    in_ctx = s < n_ctx_chunks
    pos = jnp.where(in_ctx, s, s - n_ctx_chunks)
    n_seq = jnp.where(in_ctx, n_ctx_chunks, n_lat_chunks)
    chunk = (n_seq - 1 - pos) if reverse else pos

    @pl.when(s == 0)
    def _():
        carry[...] = jnp.zeros_like(carry)

    xbuf[0:HALO, :] = jnp.where(chunk > 0, xl_ref[...], 0.0)
    xbuf[HALO:HALO + cl, :] = xm_ref[...]
    xbuf[HALO + cl:, :] = jnp.where(chunk < n_seq - 1, xr_ref[...], 0.0)
    taps = cw_ref.shape[0]
    left = taps // 2
    xin = cb_ref[...] + xbuf[pl.ds(HALO - left, cl), :] * cw_ref[0:1, :]
    for t in range(1, taps):
        xin = xin + xbuf[pl.ds(HALO - left + t, cl), :] * cw_ref[t:t + 1, :]

    gates = _dot(xin.astype(BF16), wg_ref[...]) + bg_ref[...]
    r = jax.nn.sigmoid(gates[:, :lw])
    gate_i = jax.nn.sigmoid(gates[:, lw:])
    log_a = (-LRU_C) * r * jax.nn.softplus(-lam_ref[...])
    a = jnp.exp(log_a)
    b = jnp.sqrt(-jnp.tanh(log_a) * (a * a + 1.0)) * gate_i * xin

    seg = cl // 8
    pitch = a_s.shape[1] // 8
    order = range(seg - 1, -1, -1) if reverse else range(seg)
    for g in range(lw // LANES):
        cs = slice(g * LANES, (g + 1) * LANES)
        for i in range(8):
            a_s[g, i * pitch:i * pitch + seg, :] = a[i * seg:(i + 1) * seg, cs]
            b_s[g, i * pitch:i * pitch + seg, :] = b[i * seg:(i + 1) * seg, cs]
        step = lambda ref, j: ref[g, pl.ds(j, 8, stride=pitch), :]
        for n, j in enumerate(order):
            aj, bj = step(a_s, j), step(b_s, j)
            f, p = (bj, aj) if n == 0 else (aj * f + bj, aj * p)
        first = lax.broadcasted_iota(jnp.int32, f.shape, 0) == (7 if reverse else 0)
        f = f + jnp.where(first, p * carry[:, cs], 0.0)
        dist = 1
        while dist < 8:
            f = f + p * _shift_rows(f, dist, 0.0, reverse)
            p = p * _shift_rows(p, dist, 1.0, reverse)
            dist *= 2
        h = jnp.where(first, carry[:, cs], _shift_rows(f, 1, 0.0, reverse))
        carry[:, cs] = f[0:1, :] if reverse else f[7:8, :]
        for j in order:
            h = step(a_s, j) * h + step(b_s, j)
            h_s[g, pl.ds(j, 8, stride=pitch), :] = h
    h = jnp.concatenate(
        [jnp.concatenate([h_s[g, i * pitch:i * pitch + seg, :] for i in range(8)], axis=0) for g in range(lw // LANES)],
        axis=1)
    if reverse:
        o_ref[...] = ((hf_ref[...] + h) * gg_ref[...]).astype(o_ref.dtype)
    else:
        o_ref[...] = h


def lru_scan(lay, xa, conv_w, conv_b, w_gate, b_gate, lam, reverse, hf=None, gg=None):
    nt, lw = xa.shape
    cl = LRU_CHUNK
    assert lay.t % cl == 0 and lay.lc % cl == 0
    ncc, nlc = lay.lc // cl, lay.t // cl
    n_lat_blocks = lay.n_lat // cl

    def chunk_block(b, s):
        in_ctx = s < ncc
        pos = jnp.where(in_ctx, s, s - ncc)
        if reverse:
            pos = jnp.where(in_ctx, ncc - 1 - pos, nlc - 1 - pos)
        return jnp.where(in_ctx, n_lat_blocks + b * ncc + pos, b * nlc + pos)

    per = cl // HALO
    last_halo = nt // HALO - 1
    main = pl.BlockSpec((cl, lw), lambda b, s: (chunk_block(b, s), 0))
    in_specs = [
        main,
        pl.BlockSpec((HALO, lw), lambda b, s: (jnp.maximum(chunk_block(b, s) * per - 1, 0), 0)),
        pl.BlockSpec((HALO, lw), lambda b, s: (jnp.minimum((chunk_block(b, s) + 1) * per, last_halo), 0)),
        _const_spec(conv_w.shape),
        _const_spec((1, lw)),
        _const_spec(w_gate.shape),
        _const_spec((1, 2 * lw)),
        _const_spec((1, lw)),
    ]
    args = [xa, xa, xa, conv_w, conv_b.reshape(1, lw), w_gate, b_gate.reshape(1, 2 * lw), lam.reshape(1, lw)]
    if reverse:
        in_specs += [main, main]
        args += [hf, gg]
    return pl.pallas_call(
        functools.partial(_lru_kernel, reverse=reverse, n_ctx_chunks=ncc, n_lat_chunks=nlc),
        out_shape=jax.ShapeDtypeStruct((nt, lw), BF16 if reverse else F32),
        grid=(lay.b, ncc + nlc),
        in_specs=in_specs,
        out_specs=main,
        scratch_shapes=[pltpu.VMEM((cl + 2 * HALO, lw), F32), pltpu.VMEM((1, lw), F32)]
        + [pltpu.VMEM((lw // LANES, cl + 8 * SEG_PAD, LANES), F32)] * 3,
        compiler_params=pltpu.CompilerParams(dimension_semantics=("arbitrary", "arbitrary")),
        name="lru_bwd" if reverse else "lru_fwd",
    )(*args)


def _win_attn_kernel(q_ref, kl_ref, kx_ref, vl_ref, vx_ref, sink_ref, o_ref, *, n_q_lat, n_kv, hd):
    j = pl.program_id(1)
    tq = q_ref.shape[0]
    t = kl_ref.shape[0]
    lc = kx_ref.shape[0]
    vch = vl_ref.shape[2]
    band = tq + 2 * WINDOW
    cols = GQA_GROUP * tq
    low = lax.broadcasted_iota(jnp.int32, (tq, 2 * hd), 1) < hd

    def colsum(x):
        return jnp.sum(jnp.sum(x.reshape(-1, 8, cols), axis=0), axis=0, keepdims=True)

    def colmax(x):
        return jnp.max(jnp.max(x.reshape(-1, 8, cols), axis=0), axis=0, keepdims=True)

    def attend(kv, k_parts, vt_parts, bias):
        qs = []
        for g in range(GQA_GROUP):
            h = kv * GQA_GROUP + g
            qp = q_ref[:, 2 * hd * (h // 2):2 * hd * (h // 2 + 1)]
            qs.append(jnp.where(low if h % 2 == 0 else jnp.logical_not(low), qp, jnp.zeros_like(qp)))
        qs = jnp.concatenate(qs, axis=0)
        sink = jnp.concatenate(
            [jnp.concatenate([sink_ref[kv * GQA_GROUP + g:kv * GQA_GROUP + g + 1, :]] * (tq // 128), axis=1)
             for g in range(GQA_GROUP)], axis=1) * LOG2_E
        s = [_dot_t(kp, qs) for kp in k_parts]
        if bias is not None:
            s[0] = s[0] + bias
        m = sink
        for sp in s:
            m = jnp.maximum(m, colmax(sp))
        den = jnp.exp2(sink - m)
        p = []
        for sp in s:
            pp = jnp.exp2(sp - m)
            den = den + colsum(pp)
            p.append(pp.astype(BF16))
        o_t = _dot(jnp.concatenate(vt_parts, axis=1), jnp.concatenate(p, axis=0)) * (1.0 / den)
        for pair in range(GQA_GROUP // 2):
            both = jnp.concatenate([o_t[:, (2 * pair + i) * tq:(2 * pair + i + 1) * tq] for i in range(2)], axis=0)
            c0 = 2 * hd * (kv * GQA_GROUP // 2 + pair)
            o_ref[:, c0:c0 + 2 * hd] = both.T.astype(o_ref.dtype)

    def ctx_parts(kv):
        ks = kx_ref[:, 2 * hd * kv:2 * hd * (kv + 1)]
        vts = [vx_ref[c][hd * kv:hd * (kv + 1), :] for c in range(lc // vch)]
        return ks, vts

    @pl.when(j < n_q_lat)
    def _():
        start = pl.multiple_of(jnp.clip(j * tq - WINDOW, 0, t - band), 128)
        kpos = start + lax.broadcasted_iota(jnp.int32, (band, tq), 0)
        qpos = j * tq + lax.broadcasted_iota(jnp.int32, (band, tq), 1)
        bias = jnp.where(jnp.abs(kpos - qpos) > WINDOW, -jnp.inf, 0.0).astype(F32)
        bias = jnp.concatenate([bias] * GQA_GROUP, axis=1)
        c0 = start // vch
        for kv in range(n_kv):
            kx, vxs = ctx_parts(kv)
            kb = kl_ref[pl.ds(start, band), 2 * hd * kv:2 * hd * (kv + 1)]
            vbs = [vl_ref[c0 + c][hd * kv:hd * (kv + 1), :] for c in range(band // vch)]
            attend(kv, [kb, kx], vbs + vxs, bias)

    @pl.when(j >= n_q_lat)
    def _():
        for kv in range(n_kv):
            kx, vxs = ctx_parts(kv)
            attend(kv, [kx], vxs, None)


def win_attn(lay, q, kd, vt, sink_rows, n_kv, hd):
    nt, qw = q.shape
    kw = kd.shape[1]
    tq = Q_TILE_WIN
    vch = vt.shape[2]
    assert lay.t % tq == 0 and lay.lc % tq == 0 and lay.n_lat % lay.lc == 0 and tq % 128 == 0
    assert vch == 128 and WINDOW % vch == 0 and lay.lc % vch == 0 and lay.t >= tq + 2 * WINDOW
    nb, ncq = lay.t // tq, lay.lc // tq
    ctx_q0 = lay.n_lat // tq
    ctx_k0 = lay.n_lat // lay.lc

    def qblock(b, j):
        return jnp.where(j < nb, b * nb + j, ctx_q0 + b * ncq + (j - nb))

    qspec = pl.BlockSpec((tq, qw), lambda b, j: (qblock(b, j), 0))
    return pl.pallas_call(
        functools.partial(_win_attn_kernel, n_q_lat=nb, n_kv=n_kv, hd=hd),
        out_shape=jax.ShapeDtypeStruct((nt, qw), BF16),
        grid=(lay.b, nb + ncq),
        in_specs=[
            qspec,
            pl.BlockSpec((lay.t, kw), lambda b, j: (b, 0)),
            pl.BlockSpec((lay.lc, kw), lambda b, j: (ctx_k0 + b, 0)),
            pl.BlockSpec((lay.t // vch, vt.shape[1], vch), lambda b, j: (b, 0, 0)),
            pl.BlockSpec((lay.lc // vch, vt.shape[1], vch), lambda b, j: (ctx_k0 + b, 0, 0)),
            _const_spec(sink_rows.shape),
        ],
        out_specs=qspec,
        compiler_params=pltpu.CompilerParams(vmem_limit_bytes=V7X_VMEM_LIMIT),
        name="win_attn",
    )(q, kd, kd, vt, vt, sink_rows)


def _odd_in_kernel(x_ref, mod_ref, g_ref, w_ref, cos_ref, sin_ref, gq_ref, gk_ref, q_ref, k_ref, vt_ref, *, qw, kw, hd):
    u = _pre(x_ref[...], g_ref[...], mod_ref[3:4, :], mod_ref[4:5, :]).astype(BF16)
    cos, sin = cos_ref[...], sin_ref[...]

    def normed_rope(o, width, gains, out_ref, post):
        cg = cos * gains[0:1, :]
        sg = sin * gains[1:2, :]
        step = 2 * hd
        for c0 in range(0, width, step):
            x2 = _dot(u, w_ref[:, o + c0:o + c0 + step])
            xr2 = _dot(u, w_ref[:, o + width + c0:o + width + c0 + step])
            for h0 in range(0, step, hd):
                x, xr = x2[:, h0:h0 + hd], xr2[:, h0:h0 + hd]
                r = lax.rsqrt(jnp.mean(x * x, axis=-1, keepdims=True) + EPS) * post
                out_ref[:, c0 + h0:c0 + h0 + hd] = (r * (x * cg + xr * sg)).astype(out_ref.dtype)

    normed_rope(0, qw, gq_ref[...], q_ref, hd ** -0.5 * LOG2_E)
    normed_rope(2 * qw, kw, gk_ref[...], k_ref, 1.0)
    o = 2 * qw + 2 * kw
    v = _dot(u, w_ref[:, o:o + kw])
    ch = vt_ref.shape[2]
    for c in range(vt_ref.shape[0]):
        vt_ref[c] = v[c * ch:(c + 1) * ch, :].T.astype(BF16)


def odd_in(lay, x_all, mod_l, g_pre, w_cat, cos_t, sin_t, gq, gk, qw, kw, hd):
    d = lay.d
    tm = lay.tile((512, 256, 128))
    rb = lay.rope_block(tm)
    nt = lay.nt
    row = lambda i: (i, 0)
    ch = KEY_CHUNK
    assert tm % ch == 0
    return pl.pallas_call(
        functools.partial(_odd_in_kernel, qw=qw, kw=kw, hd=hd),
        out_shape=(
            jax.ShapeDtypeStruct((nt, qw), BF16),
            jax.ShapeDtypeStruct((nt, kw), BF16),
            jax.ShapeDtypeStruct((nt // ch, kw, ch), BF16),
        ),
        grid=(nt // tm,),
        in_specs=[
            pl.BlockSpec((tm, d), row),
            _mod_spec(lay, tm),
            _const_spec((1, d)),
            _resident_spec(w_cat.shape),
            pl.BlockSpec((tm, hd), lambda i: (rb(i), 0)),
            pl.BlockSpec((tm, hd), lambda i: (rb(i), 0)),
            _const_spec(gq.shape),
            _const_spec(gk.shape),
        ],
        out_specs=(pl.BlockSpec((tm, qw), row), pl.BlockSpec((tm, kw), row),
                   pl.BlockSpec((tm // ch, kw, ch), lambda i: (i, 0, 0))),
        compiler_params=pltpu.CompilerParams(vmem_limit_bytes=V7X_VMEM_LIMIT),
        name="odd_in",
    )(x_all, mod_l, g_pre.reshape(1, d), w_cat, cos_t, sin_t, gq, gk)


def _dense_attn_kernel(q_ref, kl_ref, kx_ref, vl_ref, vx_ref, o_ref, s_ref, m_ref, l_ref, acc_ref, *, n_q_lat, nb):
    j = pl.program_id(2)
    tq = q_ref.shape[0]
    t, hd = kl_ref.shape
    lc = kx_ref.shape[0]
    ch = vl_ref.shape[2]
    cols = GQA_GROUP * tq
    qs = jnp.concatenate([q_ref[:, g * hd:(g + 1) * hd] for g in range(GQA_GROUP)], axis=0)

    def sum8(x):
        return jnp.sum(x.reshape(-1, 8, cols), axis=0)

    def colmax(x):
        return jnp.max(jnp.max(x.reshape(-1, 8, cols), axis=0), axis=0, keepdims=True)

    sx = _dot_t(kx_ref[...], qs)
    m0 = colmax(sx)
    p = jnp.exp2(sx - m0)
    m_ref[...] = m0
    l_ref[...] = sum8(p)
    acc_ref[...] = _dot(jnp.concatenate([vx_ref[c] for c in range(lc // ch)], axis=1), p.astype(BF16))

    @pl.when(j < n_q_lat)
    def _():
        n_blocks = t // nb

        def produce(i):
            s = _dot_t(kl_ref[i * nb:(i + 1) * nb, :], qs)
            s_ref[i % 2] = s
            return colmax(s)

        m = m_ref[...]
        block_max = produce(0)
        for i in range(n_blocks):
            m_new = jnp.maximum(m, block_max)
            if i + 1 < n_blocks:
                block_max = produce(i + 1)
            alpha = jnp.exp2(m - m_new)
            p = jnp.exp2(s_ref[i % 2] - m_new)
            vt = jnp.concatenate([vl_ref[i * (nb // ch) + c] for c in range(nb // ch)], axis=1)
            l_ref[...] = l_ref[...] * alpha + sum8(p)
            acc_ref[...] = acc_ref[...] * alpha + _dot(vt, p.astype(BF16))
            m = m_new

    o_t = acc_ref[...] * (1.0 / jnp.sum(l_ref[...], axis=0, keepdims=True))
    for g in range(GQA_GROUP):
        o_ref[:, g * hd:(g + 1) * hd] = o_t[:, g * tq:(g + 1) * tq].T.astype(o_ref.dtype)


def _dense_attn_unshifted_kernel(q_ref, kl_ref, kx_ref, vl_ref, vx_ref, o_ref, l_ref, acc_ref, *, n_q_lat, nb):
    j = pl.program_id(2)
    tq = q_ref.shape[0]
    t, hd = kl_ref.shape
    lc = kx_ref.shape[0]
    ch = vl_ref.shape[2]
    cols = GQA_GROUP * tq
    qs = jnp.concatenate([q_ref[:, g * hd:(g + 1) * hd] for g in range(GQA_GROUP)], axis=0)

    def block(k, vts):
        p = jnp.exp2(_dot_t(k, qs))
        return jnp.sum(p.reshape(-1, 8, cols), axis=0), _dot(jnp.concatenate(vts, axis=1), p.astype(BF16))

    l8, acc = block(kx_ref[...], [vx_ref[c] for c in range(lc // ch)])
    l_ref[...] = l8
    acc_ref[...] = acc

    @pl.when(j < n_q_lat)
    def _():
        l8, acc = l_ref[...], acc_ref[...]
        for i in range(t // nb):
            dl, da = block(kl_ref[i * nb:(i + 1) * nb, :], [vl_ref[i * (nb // ch) + c] for c in range(nb // ch)])
            l8, acc = l8 + dl, acc + da
        l_ref[...] = l8
        acc_ref[...] = acc

    o_t = acc_ref[...] * (1.0 / jnp.sum(l_ref[...], axis=0, keepdims=True))
    for g in range(GQA_GROUP):
        o_ref[:, g * hd:(g + 1) * hd] = o_t[:, g * tq:(g + 1) * tq].T.astype(o_ref.dtype)


def dense_attn(lay, q, k, vt, hd, with_ctx_queries, shifted):
    nt, qw = q.shape
    tq = 128 if shifted else 256
    ch = vt.shape[2]
    assert lay.t % tq == 0 and lay.lc % tq == 0 and lay.n_lat % lay.lc == 0 and lay.t % ch == 0 and lay.lc % ch == 0
    n_kv = k.shape[1] // hd
    gw = GQA_GROUP * hd
    nql, nqc = lay.t // tq, lay.lc // tq
    ctx_q0 = lay.n_lat // tq
    ctx_k0 = lay.n_lat // lay.lc
    n_rows = nt if with_ctx_queries else lay.n_lat

    def qblock(b, kv, j):
        return (jnp.where(j < nql, b * nql + j, ctx_q0 + b * nqc + (j - nql)), kv)

    qspec = pl.BlockSpec((tq, gw), qblock)
    cols = GQA_GROUP * tq
    nb = 1024 if lay.t % 1024 == 0 else ch
    assert nb % ch == 0 and lay.t % nb == 0
    stats = [pltpu.VMEM((8, cols), F32), pltpu.VMEM((hd, cols), F32)]
    if shifted:
        body = _dense_attn_kernel
        scratch = [pltpu.VMEM((2, nb, cols), F32), pltpu.VMEM((1, cols), F32)] + stats
    else:
        body = _dense_attn_unshifted_kernel
        scratch = stats
    return pl.pallas_call(
        functools.partial(body, n_q_lat=nql, nb=nb),
        out_shape=jax.ShapeDtypeStruct((n_rows, qw), BF16),
        grid=(lay.b, n_kv, nql + (nqc if with_ctx_queries else 0)),
        in_specs=[
            qspec,
            pl.BlockSpec((lay.t, hd), lambda b, kv, j: (b, kv)),
            pl.BlockSpec((lay.lc, hd), lambda b, kv, j: (ctx_k0 + b, kv)),
            pl.BlockSpec((lay.t // ch, hd, ch), lambda b, kv, j: (b, kv, 0)),
            pl.BlockSpec((lay.lc // ch, hd, ch), lambda b, kv, j: (ctx_k0 + b, kv, 0)),
        ],
        out_specs=qspec,
        scratch_shapes=scratch,
        compiler_params=pltpu.CompilerParams(vmem_limit_bytes=V7X_VMEM_LIMIT),
        name="dense_attn" if shifted else "dense_attn_unshifted",
    )(q, k, k, vt, vt)


def _rot_cols(w):
    pairs = w.reshape(w.shape[0], -1, 2)
    return jnp.stack([-pairs[..., 1], pairs[..., 0]], axis=-1).reshape(w.shape)


def _dup_heads(w, hd):
    heads = w.reshape(w.shape[0], -1, 1, hd)
    return jnp.concatenate([heads, heads], axis=2).reshape(w.shape[0], -1)


def _block_diag(w):
    h, n, _ = w.shape
    eye = jnp.eye(h, dtype=w.dtype)
    return (eye[:, None, :, None] * w[:, :, None, :]).reshape(h * n, h * n)


def _rope_tables(t, hd, lanes, ident_rows):
    n_freq = hd // 4
    freq = ROPE_THETA ** (-jnp.arange(n_freq, dtype=F32) / n_freq)
    pos = jnp.arange(t)
    row = (pos // GRID_W).astype(F32)
    col = (pos % GRID_W).astype(F32)
    ang = jnp.concatenate([row[:, None] * freq, col[:, None] * freq], axis=-1)
    cos = jnp.repeat(jnp.cos(ang), 2, axis=-1)
    sin = jnp.repeat(jnp.sin(ang), 2, axis=-1)
    cos = jnp.concatenate([cos, jnp.ones((ident_rows, hd), F32)], axis=0)
    sin = jnp.concatenate([sin, jnp.zeros((ident_rows, hd), F32)], axis=0)
    reps = lanes // hd
    return jnp.tile(cos, (1, reps)), jnp.tile(sin, (1, reps))


def _swap_pairs(g):
    return g.reshape(-1, 2)[:, ::-1].reshape(g.shape)


def kernel(x, c, ctx, c_ctx, w_ada, b_ada, norm_pre, norm_post, ffn_w_gate, ffn_w_up, ffn_w_down, even_w_in,
           even_conv_w, even_conv_b, lru_w_a, lru_b_a, lru_w_x, lru_b_x, lru_lambda, attn_sink, even_w_out, odd_w_in,
           odd_q_norm, odd_k_norm, odd_w_out):
    b, t, d = x.shape
    lc = ctx.shape[1]
    depth = w_ada.shape[0]
    lay = _Layout(b, t, lc)
    lay.d = d
    assert b + 1 <= MOD_ROWS and w_ada.shape[2] == N_MOD * d

    lw = even_conv_w.shape[2]
    win_heads = attn_sink.shape[1]
    win_hd = (d - lw) // win_heads
    win_kv = win_heads // GQA_GROUP
    glb_hd = odd_q_norm.shape[1]
    glb_heads = d // glb_hd
    glb_kv = glb_heads // GQA_GROUP
    qw_e, kw_e = win_heads * win_hd, win_kv * win_hd
    qw_o, kw_o = glb_heads * glb_hd, glb_kv * glb_hd

    x_all = (x.reshape(b * t, d), ctx.reshape(b * lc, d))
    c_rows = jnp.concatenate([c, c_ctx[None, :], jnp.zeros((MOD_ROWS - b - 1, d), F32)], axis=0)
    mod = ada_mod(c_rows, w_ada, b_ada).reshape(depth, MOD_ROWS, N_MOD, d)

    tm = lay.tile((512, 256, 128))
    cos_e, sin_e = _rope_tables(t, win_hd, 2 * win_hd, tm)
    cos_o, sin_o = _rope_tables(t, glb_hd, glb_hd, tm)

    wg16, wu16, wd16 = ffn_w_gate.astype(BF16), ffn_w_up.astype(BF16), ffn_w_down.astype(BF16)

    for l in range(depth):
        last = l == depth - 1
        i = l // 2
        mod_l = mod[l]
        x_all = ffn_sublayer(lay, x_all, lay.nt, mod_l, 0, norm_pre[l, 0], norm_post[l, 0], wg16, wu16, wd16, (l, 0))
        rows_out = lay.n_lat if last else lay.nt
        if l % 2 == 0:
            w = even_w_in[i]
            o1, o2, o3, o4 = 2 * lw, 2 * lw + qw_e, 2 * lw + qw_e + kw_e, 2 * lw + qw_e + 2 * kw_e
            wq = w[:, o1:o2]
            wk = w[:, o2:o3]
            w_cat = jnp.concatenate([
                w[:, :o1], wq, _rot_cols(wq), _dup_heads(wk, win_hd), _dup_heads(_rot_cols(wk), win_hd),
                w[:, o3:o4]], axis=1).astype(BF16)
            xa, gg, q, kd, vt = even_in(lay, x_all, mod_l, norm_pre[l, 1], w_cat, cos_e, sin_e, lw, qw_e, 2 * kw_e, kw_e,
                                        win_hd ** -0.5 * LOG2_E)
            spl_args = []
            for direction in range(2):
                w_gate = jnp.concatenate([_block_diag(lru_w_a[i, direction]), _block_diag(lru_w_x[i, direction])], axis=1).astype(BF16)
                b_gate = jnp.concatenate([lru_b_a[i, direction], lru_b_x[i, direction]])
                spl_args.append((w_gate, b_gate, lru_lambda[i, direction]))
            hf = lru_scan(lay, xa, even_conv_w[i], even_conv_b[i], *spl_args[0], reverse=False)
            ya = lru_scan(lay, xa, even_conv_w[i], even_conv_b[i], *spl_args[1], reverse=True, hf=hf, gg=gg)
            sink_rows = jnp.broadcast_to(attn_sink[i][:, None], (win_heads, 128)).astype(F32)
            ob = win_attn(lay, q, kd, vt, sink_rows, win_kv, win_hd)
            mix = (norm_post[l, 1], even_w_out[i].astype(BF16), [ya, ob])
        else:
            w = odd_w_in[i]
            wq, wk, wv = w[:, :qw_o], w[:, qw_o:qw_o + kw_o], w[:, qw_o + kw_o:]
            w_cat = jnp.concatenate([wq, _rot_cols(wq), wk, _rot_cols(wk), wv], axis=1).astype(BF16)
            gq = jnp.stack([odd_q_norm[i], _swap_pairs(odd_q_norm[i])])
            gk = jnp.stack([odd_k_norm[i], _swap_pairs(odd_k_norm[i])])
            q, k, vt = odd_in(lay, x_all, mod_l, norm_pre[l, 1], w_cat, cos_o, sin_o, gq, gk, qw_o, kw_o, glb_hd)
            bound = LOG2_E * glb_hd ** 0.5 * jnp.max(jnp.abs(odd_q_norm[i])) * jnp.max(jnp.abs(odd_k_norm[i]))
            o = lax.cond(
                bound <= MAX_UNSHIFTED_SCORE,
                functools.partial(dense_attn, lay, hd=glb_hd, with_ctx_queries=not last, shifted=False),
                functools.partial(dense_attn, lay, hd=glb_hd, with_ctx_queries=not last, shifted=True),
                q, k, vt)
            mix = (norm_post[l, 1], odd_w_out[i].astype(BF16), [o])
        x_all = ffn_sublayer(lay, x_all, rows_out, mod_l, 6, norm_pre[l, 2], norm_post[l, 2], wg16, wu16, wd16, (l, 1),
                             mix=mix)
    return x_all[:b * t].reshape(b, t, d)
```

```python
import functools

import jax
import jax.numpy as jnp
import numpy as np
from jax import lax
from jax.experimental import pallas as pl
from jax.experimental.pallas import tpu as pltpu

F32 = jnp.float32
BF16 = jnp.bfloat16

EPS = 1e-6
FFN_RES = 0.5
LRU_C = 8.0
WINDOW = 128
GRID_W = 64
ROPE_THETA = 10000.0
GQA_GROUP = 4
N_MOD = 9
MOD_ROWS = 16

V7X_VMEM_LIMIT = 56 * 1024 * 1024
Q_TILE_WIN = 256
VT_CHUNK = 128
LRU_CHUNK = 256
LANES = 128
SEG_PAD = 4
HALO = 8
FFN_SUBTILES = 1
KEY_CHUNK = 256
LOG2_E = 1.4426950408889634
MAX_UNSHIFTED_SCORE = 60.0


def _const_spec(shape):
    zeros = (0,) * len(shape)
    return pl.BlockSpec(shape, lambda *_: zeros)


def _resident_spec(shape):
    zeros = (0,) * len(shape)
    return pl.BlockSpec(shape, lambda *_: zeros, pipeline_mode=pl.Buffered(1))


def _dot(a, b):
    return jnp.dot(a, b, preferred_element_type=F32)


def _dot_t(a, b):
    return lax.dot_general(a, b, (((1,), (1,)), ((), ())), preferred_element_type=F32)


def _rope(x, cos, sin_signed):
    return x * cos + pltpu.roll(x, LANES // 2, 1) * sin_signed


def _rms(x, g):
    return x * lax.rsqrt(jnp.mean(x * x, axis=-1, keepdims=True) + EPS) * g


def _pre(x, g, shift, scale):
    return _rms(x, g) * (1.0 + scale) + shift


def _ada_kernel(c_ref, w_ref, b_ref, o_ref):
    c = c_ref[...]
    s = c * jax.nn.sigmoid(c)
    o_ref[...] = _dot(s.astype(BF16), w_ref[...].astype(BF16)) + b_ref[...]


def ada_mod(c_rows, w_ada, b_ada):
    depth, d, n = w_ada.shape
    tn = 1024
    return pl.pallas_call(
        _ada_kernel,
        out_shape=jax.ShapeDtypeStruct((depth, MOD_ROWS, n), F32),
        grid=(depth, n // tn),
        in_specs=[
            pl.BlockSpec((MOD_ROWS, d), lambda l, j: (0, 0)),
            pl.BlockSpec((None, d, tn), lambda l, j: (l, 0, j)),
            pl.BlockSpec((None, 1, tn), lambda l, j: (l, 0, j)),
        ],
        out_specs=pl.BlockSpec((None, MOD_ROWS, tn), lambda l, j: (l, 0, j)),
        name="ada_mod",
    )(c_rows, w_ada, b_ada.reshape(depth, 1, n))


class _Layout:
    def __init__(self, b, t, lc):
        self.b, self.t, self.lc = b, t, lc
        self.n_lat = b * t
        self.nt = b * (t + lc)

    def tile(self, candidates):
        for tm in candidates:
            if self.t % tm == 0 and (self.b * self.lc) % tm == 0:
                return tm
        raise ValueError("no token tile divides both the latent and the context stream")

    def mod_row(self, tm):
        n_lat_tiles, per_b, b = self.n_lat // tm, self.t // tm, self.b
        return lambda i: jnp.where(i < n_lat_tiles, i // per_b, b)

    def rope_block(self, tm):
        n_lat_tiles, per_b = self.n_lat // tm, self.t // tm
        return lambda i: jnp.where(i < n_lat_tiles, i % per_b, per_b)


def _mod_spec(lay, tm):
    row = lay.mod_row(tm)
    return pl.BlockSpec((None, N_MOD, lay_d(lay)), lambda i: (row(i), 0, 0))


def lay_d(lay):
    return lay.d


def _ffn_kernel(x_ref, *rest, m0, ff_chunk, n_lat_tiles):
    ctx_ref = None
    if n_lat_tiles is not None:
        ctx_ref, rest = rest[0], rest[1:]
    mod_ref, gpre_ref, gpost_ref, wg_ref, wu_ref, wd_ref = rest[:6]
    rest = rest[6:]
    o_ref, h_ref = rest[-2:]
    tm = x_ref.shape[0]
    d_ff = wg_ref.shape[1]
    for r0 in range(0, tm, tm // FFN_SUBTILES):
        rows = slice(r0, r0 + tm // FFN_SUBTILES)
        x = x_ref[rows, :]
        if ctx_ref is not None:
            x = jnp.where(pl.program_id(0) < n_lat_tiles, x, ctx_ref[rows, :])
        if len(rest) > 2:
            gmix_ref, wmix_ref = rest[:2]
            parts = [r[rows, :] for r in rest[2:-2]]
            y = _dot(jnp.concatenate(parts, axis=1) if len(parts) > 1 else parts[0], wmix_ref[...])
            x = x + mod_ref[5:6, :] * _rms(y, gmix_ref[...])
        u = _pre(x, gpre_ref[...], mod_ref[m0:m0 + 1, :], mod_ref[m0 + 1:m0 + 2, :]).astype(BF16)
        for c0 in range(0, d_ff, ff_chunk):
            g = _dot(u, wg_ref[:, c0:c0 + ff_chunk])
            up = _dot(u, wu_ref[:, c0:c0 + ff_chunk])
            h_ref[rows, c0:c0 + ff_chunk] = (g * jax.nn.sigmoid(g) * up).astype(BF16)
        y = _dot(h_ref[rows, :], wd_ref[...])
        o_ref[rows, :] = x + FFN_RES * mod_ref[m0 + 2:m0 + 3, :] * _rms(y, gpost_ref[...])


def ffn_sublayer(lay, x_all, n_rows, mod_l, m0, g_pre, g_post, wg, wu, wd, which, mix=None):
    d = lay.d
    d_ff = wg.shape[-1]
    picked = lambda r, c: pl.BlockSpec((None, None, r, c), lambda i: which + (0, 0), pipeline_mode=pl.Buffered(1))
    tm = lay.tile((512, 256, 128))
    row = lambda i: (i, 0)
    n_lat_tiles = None
    if isinstance(x_all, tuple):
        n_lat_tiles = lay.n_lat // tm
        in_specs = [pl.BlockSpec((tm, d), lambda i: (jnp.minimum(i, n_lat_tiles - 1), 0)),
                    pl.BlockSpec((tm, d), lambda i: (jnp.maximum(i - n_lat_tiles, 0), 0))]
        args = list(x_all)
    else:
        in_specs = [pl.BlockSpec((tm, d), row)]
        args = [x_all]
    in_specs += [
        _mod_spec(lay, tm),
        _const_spec((1, d)),
        _const_spec((1, d)),
        picked(d, d_ff),
        picked(d, d_ff),
        picked(d_ff, d),
    ]
    args += [mod_l, g_pre.reshape(1, d), g_post.reshape(1, d), wg, wu, wd]
    if mix is not None:
        g_mix, w_mix, parts = mix
        in_specs += [_const_spec((1, d)), _resident_spec(w_mix.shape)] + [pl.BlockSpec((tm, p.shape[1]), row) for p in parts]
        args += [g_mix.reshape(1, d), w_mix] + list(parts)
    return pl.pallas_call(
        functools.partial(_ffn_kernel, m0=m0, ff_chunk=256, n_lat_tiles=n_lat_tiles),
        out_shape=jax.ShapeDtypeStruct((n_rows, d), F32),
        grid=(n_rows // tm,),
        in_specs=in_specs,
        out_specs=pl.BlockSpec((tm, d), row),
        scratch_shapes=[pltpu.VMEM((tm, d_ff), BF16)],
        compiler_params=pltpu.CompilerParams(vmem_limit_bytes=V7X_VMEM_LIMIT),
        name="ffn_sublayer" if mix is None else "mix_ffn_sublayer",
    )(*args)


def _even_in_kernel(x_ref, mod_ref, g_ref, w_ref, cos_ref, sin_ref, xa_ref, gg_ref, q_ref, k_ref, vt_ref, *, lw, qw, kw, vw,
                    q_scale):
    u = _pre(x_ref[...], g_ref[...], mod_ref[3:4, :], mod_ref[4:5, :]).astype(BF16)
    cos, sin = cos_ref[...], sin_ref[...]
    o = 0
    xa_ref[...] = _dot(u, w_ref[:, o:o + lw])
    o += lw
    gg_ref[...] = jax.nn.gelu(_dot(u, w_ref[:, o:o + lw]))
    o += lw
    q = _dot(u, w_ref[:, o:o + qw])
    o += qw
    for c0 in range(0, qw, LANES):
        q_ref[:, c0:c0 + LANES] = (_rope(q[:, c0:c0 + LANES], cos, sin) * q_scale).astype(BF16)
    k = _dot(u, w_ref[:, o:o + kw])
    o += kw
    for c0 in range(0, kw, LANES):
        k_ref[:, c0:c0 + LANES] = _rope(k[:, c0:c0 + LANES], cos, sin).astype(BF16)
    v = _dot(u, w_ref[:, o:o + vw])
    ch = vt_ref.shape[2]
    for c in range(vt_ref.shape[0]):
        vt_ref[c] = v[c * ch:(c + 1) * ch, :].T.astype(BF16)


def even_in(lay, x_all, mod_l, g_pre, w_cat, cos_t, sin_t, lw, qw, kw, vw, q_scale):
    d = lay.d
    tm = lay.tile((512, 256, 128))
    rb = lay.rope_block(tm)
    nt = lay.nt
    row = lambda i: (i, 0)
    ch = VT_CHUNK
    assert tm % ch == 0
    return pl.pallas_call(
        functools.partial(_even_in_kernel, lw=lw, qw=qw, kw=kw, vw=vw, q_scale=q_scale),
        out_shape=(
            jax.ShapeDtypeStruct((nt, lw), F32),
            jax.ShapeDtypeStruct((nt, lw), F32),
            jax.ShapeDtypeStruct((nt, qw), BF16),
            jax.ShapeDtypeStruct((nt, kw), BF16),
            jax.ShapeDtypeStruct((nt // ch, vw, ch), BF16),
        ),
        grid=(nt // tm,),
        in_specs=[
            pl.BlockSpec((tm, d), row),
            _mod_spec(lay, tm),
            _const_spec((1, d)),
            _resident_spec(w_cat.shape),
            pl.BlockSpec((tm, cos_t.shape[1]), lambda i: (rb(i), 0)),
            pl.BlockSpec((tm, cos_t.shape[1]), lambda i: (rb(i), 0)),
        ],
        out_specs=(
            pl.BlockSpec((tm, lw), row),
            pl.BlockSpec((tm, lw), row),
            pl.BlockSpec((tm, qw), row),
            pl.BlockSpec((tm, kw), row),
            pl.BlockSpec((tm // ch, vw, ch), lambda i: (i, 0, 0)),
        ),
        compiler_params=pltpu.CompilerParams(vmem_limit_bytes=V7X_VMEM_LIMIT),
        name="even_in",
    )(x_all, mod_l, g_pre.reshape(1, d), w_cat, cos_t, sin_t)


def _shift_rows(x, dist, fill, reverse):
    n = x.shape[0]
    rows = lax.broadcasted_iota(jnp.int32, x.shape, 0)
    if reverse:
        return jnp.where(rows < n - dist, pltpu.roll(x, n - dist, 0), fill)
    return jnp.where(rows >= dist, pltpu.roll(x, dist, 0), fill)


def _lru_kernel(xm_ref, xl_ref, xr_ref, cw_ref, cb_ref, wg_ref, bg_ref, lam_ref, *rest,
                reverse, n_ctx_chunks, n_lat_chunks):
    if reverse:
        hf_ref, gg_ref, o_ref, xbuf, carry, a_s, b_s, h_s = rest
    else:
        o_ref, xbuf, carry, a_s, b_s, h_s = rest
    s = pl.program_id(1)
    cl, lw = xm_ref.shape
    in_ctx = s < n_ctx_chunks
    pos = jnp.where(in_ctx, s, s - n_ctx_chunks)
    n_seq = jnp.where(in_ctx, n_ctx_chunks, n_lat_chunks)
    chunk = (n_seq - 1 - pos) if reverse else pos

    @pl.when(s == 0)
    def _():
        carry[...] = jnp.zeros_like(carry)

    xbuf[0:HALO, :] = jnp.where(chunk > 0, xl_ref[...], 0.0)
    xbuf[HALO:HALO + cl, :] = xm_ref[...]
    xbuf[HALO + cl:, :] = jnp.where(chunk < n_seq - 1, xr_ref[...], 0.0)
    taps = cw_ref.shape[0]
    left = taps // 2
    xin = cb_ref[...] + xbuf[pl.ds(HALO - left, cl), :] * cw_ref[0:1, :]
    for t in range(1, taps):
        xin = xin + xbuf[pl.ds(HALO - left + t, cl), :] * cw_ref[t:t + 1, :]

    gates = _dot(xin.astype(BF16), wg_ref[...]) + bg_ref[...]
    r = jax.nn.sigmoid(gates[:, :lw])
    gate_i = jax.nn.sigmoid(gates[:, lw:])
    log_a = (-LRU_C) * r * jax.nn.softplus(-lam_ref[...])
    a = jnp.exp(log_a)
    b = jnp.sqrt(-jnp.tanh(log_a) * (a * a + 1.0)) * gate_i * xin

    seg = cl // 8
    pitch = a_s.shape[1] // 8
    order = range(seg - 1, -1, -1) if reverse else range(seg)
    for g in range(lw // LANES):
        cs = slice(g * LANES, (g + 1) * LANES)
        for i in range(8):
            a_s[g, i * pitch:i * pitch + seg, :] = a[i * seg:(i + 1) * seg, cs]
            b_s[g, i * pitch:i * pitch + seg, :] = b[i * seg:(i + 1) * seg, cs]
        step = lambda ref, j: ref[g, pl.ds(j, 8, stride=pitch), :]
        for n, j in enumerate(order):
            aj, bj = step(a_s, j), step(b_s, j)
            f, p = (bj, aj) if n == 0 else (aj * f + bj, aj * p)
        first = lax.broadcasted_iota(jnp.int32, f.shape, 0) == (7 if reverse else 0)
        f = f + jnp.where(first, p * carry[:, cs], 0.0)
        dist = 1
        while dist < 8:
            f = f + p * _shift_rows(f, dist, 0.0, reverse)
            p = p * _shift_rows(p, dist, 1.0, reverse)
            dist *= 2
        h = jnp.where(first, carry[:, cs], _shift_rows(f, 1, 0.0, reverse))
        carry[:, cs] = f[0:1, :] if reverse else f[7:8, :]
        for j in order:
            h = step(a_s, j) * h + step(b_s, j)
            h_s[g, pl.ds(j, 8, stride=pitch), :] = h
    h = jnp.concatenate(
        [jnp.concatenate([h_s[g, i * pitch:i * pitch + seg, :] for i in range(8)], axis=0) for g in range(lw // LANES)],
        axis=1)
    if reverse:
        o_ref[...] = ((hf_ref[...] + h) * gg_ref[...]).astype(o_ref.dtype)
    else:
        o_ref[...] = h


def lru_scan(lay, xa, conv_w, conv_b, w_gate, b_gate, lam, reverse, hf=None, gg=None):
    nt, lw = xa.shape
    cl = LRU_CHUNK
    assert lay.t % cl == 0 and lay.lc % cl == 0
    ncc, nlc = lay.lc // cl, lay.t // cl
    n_lat_blocks = lay.n_lat // cl

    def chunk_block(b, s):
        in_ctx = s < ncc
        pos = jnp.where(in_ctx, s, s - ncc)
        if reverse:
            pos = jnp.where(in_ctx, ncc - 1 - pos, nlc - 1 - pos)
        return jnp.where(in_ctx, n_lat_blocks + b * ncc + pos, b * nlc + pos)

    per = cl // HALO
    last_halo = nt // HALO - 1
    main = pl.BlockSpec((cl, lw), lambda b, s: (chunk_block(b, s), 0))
    in_specs = [
        main,
        pl.BlockSpec((HALO, lw), lambda b, s: (jnp.maximum(chunk_block(b, s) * per - 1, 0), 0)),
        pl.BlockSpec((HALO, lw), lambda b, s: (jnp.minimum((chunk_block(b, s) + 1) * per, last_halo), 0)),
        _const_spec(conv_w.shape),
        _const_spec((1, lw)),
        _const_spec(w_gate.shape),
        _const_spec((1, 2 * lw)),
        _const_spec((1, lw)),
    ]
    args = [xa, xa, xa, conv_w, conv_b.reshape(1, lw), w_gate, b_gate.reshape(1, 2 * lw), lam.reshape(1, lw)]
    if reverse:
        in_specs += [main, main]
        args += [hf, gg]
    return pl.pallas_call(
        functools.partial(_lru_kernel, reverse=reverse, n_ctx_chunks=ncc, n_lat_chunks=nlc),
        out_shape=jax.ShapeDtypeStruct((nt, lw), BF16 if reverse else F32),
        grid=(lay.b, ncc + nlc),
        in_specs=in_specs,
        out_specs=main,
        scratch_shapes=[pltpu.VMEM((cl + 2 * HALO, lw), F32), pltpu.VMEM((1, lw), F32)]
        + [pltpu.VMEM((lw // LANES, cl + 8 * SEG_PAD, LANES), F32)] * 3,
        compiler_params=pltpu.CompilerParams(dimension_semantics=("arbitrary", "arbitrary")),
        name="lru_bwd" if reverse else "lru_fwd",
    )(*args)


def _win_attn_kernel(q_ref, kl_ref, kx_ref, vl_ref, vx_ref, sink_ref, o_ref, *, n_q_lat, n_kv, hd):
    j = pl.program_id(1)
    tq = q_ref.shape[0]
    t = kl_ref.shape[0]
    lc = kx_ref.shape[0]
    vch = vl_ref.shape[2]
    band = tq + 2 * WINDOW
    cols = GQA_GROUP * tq
    low = lax.broadcasted_iota(jnp.int32, (tq, 2 * hd), 1) % hd < hd // 2

    def colsum(x):
        return jnp.sum(jnp.sum(x.reshape(-1, 8, cols), axis=0), axis=0, keepdims=True)

    def colmax(x):
        return jnp.max(jnp.max(x.reshape(-1, 8, cols), axis=0), axis=0, keepdims=True)

    def attend(kv, k_parts, vt_parts, bias):
        qs = []
        for g in range(GQA_GROUP):
            h = kv * GQA_GROUP + g
            qp = q_ref[:, 2 * hd * (h // 2):2 * hd * (h // 2 + 1)]
            qs.append(jnp.where(low if h % 2 == 0 else jnp.logical_not(low), qp, jnp.zeros_like(qp)))
        qs = jnp.concatenate(qs, axis=0)
        sink = jnp.concatenate(
            [jnp.concatenate([sink_ref[kv * GQA_GROUP + g:kv * GQA_GROUP + g + 1, :]] * (tq // 128), axis=1)
             for g in range(GQA_GROUP)], axis=1) * LOG2_E
        s = [_dot_t(kp, qs) for kp in k_parts]
        if bias is not None:
            s[0] = s[0] + bias
        m = sink
        for sp in s:
            m = jnp.maximum(m, colmax(sp))
        den = jnp.exp2(sink - m)
        p = []
        for sp in s:
            pp = jnp.exp2(sp - m)
            den = den + colsum(pp)
            p.append(pp.astype(BF16))
        o_t = _dot(jnp.concatenate(vt_parts, axis=1), jnp.concatenate(p, axis=0)) * (1.0 / den)
        for pair in range(GQA_GROUP // 2):
            both = jnp.concatenate([o_t[:, (2 * pair + i) * tq:(2 * pair + i + 1) * tq] for i in range(2)], axis=0)
            c0 = 2 * hd * (kv * GQA_GROUP // 2 + pair)
            o_ref[:, c0:c0 + 2 * hd] = both.T.astype(o_ref.dtype)

    def ctx_parts(kv):
        ks = kx_ref[:, 2 * hd * kv:2 * hd * (kv + 1)]
        vts = [vx_ref[c][hd * kv:hd * (kv + 1), :] for c in range(lc // vch)]
        return ks, vts

    @pl.when(j < n_q_lat)
    def _():
        start = pl.multiple_of(jnp.clip(j * tq - WINDOW, 0, t - band), 128)
        kpos = start + lax.broadcasted_iota(jnp.int32, (band, tq), 0)
        qpos = j * tq + lax.broadcasted_iota(jnp.int32, (band, tq), 1)
        bias = jnp.where(jnp.abs(kpos - qpos) > WINDOW, -jnp.inf, 0.0).astype(F32)
        bias = jnp.concatenate([bias] * GQA_GROUP, axis=1)
        c0 = start // vch
        for kv in range(n_kv):
            kx, vxs = ctx_parts(kv)
            kb = kl_ref[pl.ds(start, band), 2 * hd * kv:2 * hd * (kv + 1)]
            vbs = [vl_ref[c0 + c][hd * kv:hd * (kv + 1), :] for c in range(band // vch)]
            attend(kv, [kb, kx], vbs + vxs, bias)

    @pl.when(j >= n_q_lat)
    def _():
        for kv in range(n_kv):
            kx, vxs = ctx_parts(kv)
            attend(kv, [kx], vxs, None)


def win_attn(lay, q, kd, vt, sink_rows, n_kv, hd):
    nt, qw = q.shape
    kw = kd.shape[1]
    tq = Q_TILE_WIN
    vch = vt.shape[2]
    assert lay.t % tq == 0 and lay.lc % tq == 0 and lay.n_lat % lay.lc == 0 and tq % 128 == 0
    assert vch == 128 and WINDOW % vch == 0 and lay.lc % vch == 0 and lay.t >= tq + 2 * WINDOW
    nb, ncq = lay.t // tq, lay.lc // tq
    ctx_q0 = lay.n_lat // tq
    ctx_k0 = lay.n_lat // lay.lc

    def qblock(b, j):
        return jnp.where(j < nb, b * nb + j, ctx_q0 + b * ncq + (j - nb))

    qspec = pl.BlockSpec((tq, qw), lambda b, j: (qblock(b, j), 0))
    return pl.pallas_call(
        functools.partial(_win_attn_kernel, n_q_lat=nb, n_kv=n_kv, hd=hd),
        out_shape=jax.ShapeDtypeStruct((nt, qw), BF16),
        grid=(lay.b, nb + ncq),
        in_specs=[
            qspec,
            pl.BlockSpec((lay.t, kw), lambda b, j: (b, 0)),
            pl.BlockSpec((lay.lc, kw), lambda b, j: (ctx_k0 + b, 0)),
            pl.BlockSpec((lay.t // vch, vt.shape[1], vch), lambda b, j: (b, 0, 0)),
            pl.BlockSpec((lay.lc // vch, vt.shape[1], vch), lambda b, j: (ctx_k0 + b, 0, 0)),
            _const_spec(sink_rows.shape),
        ],
        out_specs=qspec,
        compiler_params=pltpu.CompilerParams(vmem_limit_bytes=V7X_VMEM_LIMIT),
        name="win_attn",
    )(q, kd, kd, vt, vt, sink_rows)


def _odd_in_kernel(x_ref, mod_ref, g_ref, w_ref, cos_ref, sin_ref, gq_ref, gk_ref, q_ref, k_ref, vt_ref, *, qw, kw, hd):
    u = _pre(x_ref[...], g_ref[...], mod_ref[3:4, :], mod_ref[4:5, :]).astype(BF16)
    cos, sin = cos_ref[...], sin_ref[...]

    def normed_rope(o, width, gains, out_ref, post):
        cg = cos * gains[0:1, :]
        sg = sin * gains[1:2, :]
        step = 2 * hd
        for c0 in range(0, width, step):
            x2 = _dot(u, w_ref[:, o + c0:o + c0 + step])
            xp2 = _dot(u, w_ref[:, o + width + c0:o + width + c0 + step])
            for h0 in range(0, step, hd):
                x, xp = x2[:, h0:h0 + hd], xp2[:, h0:h0 + hd]
                r = lax.rsqrt(jnp.mean(x * x, axis=-1, keepdims=True) + EPS) * post
                out_ref[:, c0 + h0:c0 + h0 + hd] = (r * (x * cg + xp * sg)).astype(out_ref.dtype)

    normed_rope(0, qw, gq_ref[...], q_ref, hd ** -0.5 * LOG2_E)
    normed_rope(2 * qw, kw, gk_ref[...], k_ref, 1.0)
    o = 2 * qw + 2 * kw
    v = _dot(u, w_ref[:, o:o + kw])
    ch = vt_ref.shape[2]
    for c in range(vt_ref.shape[0]):
        vt_ref[c] = v[c * ch:(c + 1) * ch, :].T.astype(BF16)


def odd_in(lay, x_all, mod_l, g_pre, w_cat, cos_t, sin_t, gq, gk, qw, kw, hd):
    d = lay.d
    tm = lay.tile((512, 256, 128))
    rb = lay.rope_block(tm)
    nt = lay.nt
    row = lambda i: (i, 0)
    ch = KEY_CHUNK
    assert tm % ch == 0
    return pl.pallas_call(
        functools.partial(_odd_in_kernel, qw=qw, kw=kw, hd=hd),
        out_shape=(
            jax.ShapeDtypeStruct((nt, qw), BF16),
            jax.ShapeDtypeStruct((nt, kw), BF16),
            jax.ShapeDtypeStruct((nt // ch, kw, ch), BF16),
        ),
        grid=(nt // tm,),
        in_specs=[
            pl.BlockSpec((tm, d), row),
            _mod_spec(lay, tm),
            _const_spec((1, d)),
            _resident_spec(w_cat.shape),
            pl.BlockSpec((tm, hd), lambda i: (rb(i), 0)),
            pl.BlockSpec((tm, hd), lambda i: (rb(i), 0)),
            _const_spec(gq.shape),
            _const_spec(gk.shape),
        ],
        out_specs=(pl.BlockSpec((tm, qw), row), pl.BlockSpec((tm, kw), row),
                   pl.BlockSpec((tm // ch, kw, ch), lambda i: (i, 0, 0))),
        compiler_params=pltpu.CompilerParams(vmem_limit_bytes=V7X_VMEM_LIMIT),
        name="odd_in",
    )(x_all, mod_l, g_pre.reshape(1, d), w_cat, cos_t, sin_t, gq, gk)


def _dense_attn_kernel(q_ref, kl_ref, kx_ref, vl_ref, vx_ref, o_ref, s_ref, m_ref, l_ref, acc_ref, *, n_q_lat, nb):
    j = pl.program_id(2)
    tq = q_ref.shape[0]
    t, hd = kl_ref.shape
    lc = kx_ref.shape[0]
    ch = vl_ref.shape[2]
    cols = GQA_GROUP * tq
    qs = jnp.concatenate([q_ref[:, g * hd:(g + 1) * hd] for g in range(GQA_GROUP)], axis=0)

    def sum8(x):
        return jnp.sum(x.reshape(-1, 8, cols), axis=0)

    def colmax(x):
        return jnp.max(jnp.max(x.reshape(-1, 8, cols), axis=0), axis=0, keepdims=True)

    sx = _dot_t(kx_ref[...], qs)
    m0 = colmax(sx)
    p = jnp.exp2(sx - m0)
    m_ref[...] = m0
    l_ref[...] = sum8(p)
    acc_ref[...] = _dot(jnp.concatenate([vx_ref[c] for c in range(lc // ch)], axis=1), p.astype(BF16))

    @pl.when(j < n_q_lat)
    def _():
        n_blocks = t // nb

        def produce(i):
            s = _dot_t(kl_ref[i * nb:(i + 1) * nb, :], qs)
            s_ref[i % 2] = s
            return colmax(s)

        m = m_ref[...]
        block_max = produce(0)
        for i in range(n_blocks):
            m_new = jnp.maximum(m, block_max)
            if i + 1 < n_blocks:
                block_max = produce(i + 1)
            alpha = jnp.exp2(m - m_new)
            p = jnp.exp2(s_ref[i % 2] - m_new)
            vt = jnp.concatenate([vl_ref[i * (nb // ch) + c] for c in range(nb // ch)], axis=1)
            l_ref[...] = l_ref[...] * alpha + sum8(p)
            acc_ref[...] = acc_ref[...] * alpha + _dot(vt, p.astype(BF16))
            m = m_new

    o_t = acc_ref[...] * (1.0 / jnp.sum(l_ref[...], axis=0, keepdims=True))
    for g in range(GQA_GROUP):
        o_ref[:, g * hd:(g + 1) * hd] = o_t[:, g * tq:(g + 1) * tq].T.astype(o_ref.dtype)


def _dense_attn_unshifted_kernel(q_ref, kl_ref, kx_ref, vl_ref, vx_ref, o_ref, l_ref, acc_ref, *, n_q_lat, nb):
    j = pl.program_id(2)
    tq = q_ref.shape[0]
    t, hd = kl_ref.shape
    lc = kx_ref.shape[0]
    ch = vl_ref.shape[2]
    cols = GQA_GROUP * tq
    qs = jnp.concatenate([q_ref[:, g * hd:(g + 1) * hd] for g in range(GQA_GROUP)], axis=0)

    def block(k, vts):
        p = jnp.exp2(_dot_t(k, qs))
        return jnp.sum(p.reshape(-1, 8, cols), axis=0), _dot(jnp.concatenate(vts, axis=1), p.astype(BF16))

    l8, acc = block(kx_ref[...], [vx_ref[c] for c in range(lc // ch)])
    l_ref[...] = l8
    acc_ref[...] = acc

    @pl.when(j < n_q_lat)
    def _():
        l8, acc = l_ref[...], acc_ref[...]
        for i in range(t // nb):
            dl, da = block(kl_ref[i * nb:(i + 1) * nb, :], [vl_ref[i * (nb // ch) + c] for c in range(nb // ch)])
            l8, acc = l8 + dl, acc + da
        l_ref[...] = l8
        acc_ref[...] = acc

    o_t = acc_ref[...] * (1.0 / jnp.sum(l_ref[...], axis=0, keepdims=True))
    for g in range(GQA_GROUP):
        o_ref[:, g * hd:(g + 1) * hd] = o_t[:, g * tq:(g + 1) * tq].T.astype(o_ref.dtype)


def dense_attn(lay, q, k, vt, hd, with_ctx_queries, shifted):
    nt, qw = q.shape
    tq = 128 if shifted else 256
    ch = vt.shape[2]
    assert lay.t % tq == 0 and lay.lc % tq == 0 and lay.n_lat % lay.lc == 0 and lay.t % ch == 0 and lay.lc % ch == 0
    n_kv = k.shape[1] // hd
    gw = GQA_GROUP * hd
    nql, nqc = lay.t // tq, lay.lc // tq
    ctx_q0 = lay.n_lat // tq
    ctx_k0 = lay.n_lat // lay.lc
    n_rows = nt if with_ctx_queries else lay.n_lat

    def qblock(b, kv, j):
        return (jnp.where(j < nql, b * nql + j, ctx_q0 + b * nqc + (j - nql)), kv)

    qspec = pl.BlockSpec((tq, gw), qblock)
    cols = GQA_GROUP * tq
    nb = 1024 if lay.t % 1024 == 0 else ch
    assert nb % ch == 0 and lay.t % nb == 0
    stats = [pltpu.VMEM((8, cols), F32), pltpu.VMEM((hd, cols), F32)]
    if shifted:
        body = _dense_attn_kernel
        scratch = [pltpu.VMEM((2, nb, cols), F32), pltpu.VMEM((1, cols), F32)] + stats
    else:
        body = _dense_attn_unshifted_kernel
        scratch = stats
    return pl.pallas_call(
        functools.partial(body, n_q_lat=nql, nb=nb),
        out_shape=jax.ShapeDtypeStruct((n_rows, qw), BF16),
        grid=(lay.b, n_kv, nql + (nqc if with_ctx_queries else 0)),
        in_specs=[
            qspec,
            pl.BlockSpec((lay.t, hd), lambda b, kv, j: (b, kv)),
            pl.BlockSpec((lay.lc, hd), lambda b, kv, j: (ctx_k0 + b, kv)),
            pl.BlockSpec((lay.t // ch, hd, ch), lambda b, kv, j: (b, kv, 0)),
            pl.BlockSpec((lay.lc // ch, hd, ch), lambda b, kv, j: (ctx_k0 + b, kv, 0)),
        ],
        out_specs=qspec,
        scratch_shapes=scratch,
        compiler_params=pltpu.CompilerParams(vmem_limit_bytes=V7X_VMEM_LIMIT),
        name="dense_attn" if shifted else "dense_attn_unshifted",
    )(q, k, k, vt, vt)


def _split_pairs(w, hd, dup=False):
    d = w.shape[0]
    pairs = w.reshape(d, -1, hd // 2, 2)
    if dup:
        pairs = jnp.concatenate([pairs[:, :, None], pairs[:, :, None]], axis=2).reshape(d, -1, hd // 2, 2)
    per_group = LANES // hd
    groups = pairs.reshape(d, -1, per_group, hd // 2, 2)
    return jnp.moveaxis(groups, 4, 2).reshape(d, -1)


def _swap_halves(w):
    return jnp.flip(w.reshape(w.shape[0], -1, 2, LANES // 2), axis=2).reshape(w.shape)


def _block_diag(w):
    h, n, _ = w.shape
    eye = jnp.eye(h, dtype=w.dtype)
    return (eye[:, None, :, None] * w[:, :, None, :]).reshape(h * n, h * n)


def _rope_tables(t, hd, ident_rows):
    n_freq = hd // 4
    freq = ROPE_THETA ** (-jnp.arange(n_freq, dtype=F32) / n_freq)
    pos = jnp.arange(t)
    row = (pos // GRID_W).astype(F32)
    col = (pos % GRID_W).astype(F32)
    ang = jnp.concatenate([row[:, None] * freq, col[:, None] * freq], axis=-1)
    reps = LANES // hd
    cos = jnp.tile(jnp.cos(ang), (1, reps))
    sin = jnp.tile(jnp.sin(ang), (1, reps))
    cos = jnp.concatenate([cos, jnp.ones((ident_rows, LANES // 2), F32)], axis=0)
    sin = jnp.concatenate([sin, jnp.zeros((ident_rows, LANES // 2), F32)], axis=0)
    return jnp.concatenate([cos, cos], axis=1), jnp.concatenate([-sin, sin], axis=1)


def kernel(x, c, ctx, c_ctx, w_ada, b_ada, norm_pre, norm_post, ffn_w_gate, ffn_w_up, ffn_w_down, even_w_in,
           even_conv_w, even_conv_b, lru_w_a, lru_b_a, lru_w_x, lru_b_x, lru_lambda, attn_sink, even_w_out, odd_w_in,
           odd_q_norm, odd_k_norm, odd_w_out):
    b, t, d = x.shape
    lc = ctx.shape[1]
    depth = w_ada.shape[0]
    lay = _Layout(b, t, lc)
    lay.d = d
    assert b + 1 <= MOD_ROWS and w_ada.shape[2] == N_MOD * d

    lw = even_conv_w.shape[2]
    win_heads = attn_sink.shape[1]
    win_hd = (d - lw) // win_heads
    win_kv = win_heads // GQA_GROUP
    glb_hd = odd_q_norm.shape[1]
    glb_heads = d // glb_hd
    glb_kv = glb_heads // GQA_GROUP
    qw_e, kw_e = win_heads * win_hd, win_kv * win_hd
    qw_o, kw_o = glb_heads * glb_hd, glb_kv * glb_hd

    x_all = (x.reshape(b * t, d), ctx.reshape(b * lc, d))
    c_rows = jnp.concatenate([c, c_ctx[None, :], jnp.zeros((MOD_ROWS - b - 1, d), F32)], axis=0)
    mod = ada_mod(c_rows, w_ada, b_ada).reshape(depth, MOD_ROWS, N_MOD, d)

    tm = lay.tile((512, 256, 128))
    cos_e, sin_e = _rope_tables(t, win_hd, tm)
    cos_o, sin_o = _rope_tables(t, glb_hd, tm)

    wg16, wu16, wd16 = ffn_w_gate.astype(BF16), ffn_w_up.astype(BF16), ffn_w_down.astype(BF16)

    for l in range(depth):
        last = l == depth - 1
        i = l // 2
        mod_l = mod[l]
        x_all = ffn_sublayer(lay, x_all, lay.nt, mod_l, 0, norm_pre[l, 0], norm_post[l, 0], wg16, wu16, wd16, (l, 0))
        rows_out = lay.n_lat if last else lay.nt
        if l % 2 == 0:
            w = even_w_in[i]
            o1, o2, o3, o4 = 2 * lw, 2 * lw + qw_e, 2 * lw + qw_e + kw_e, 2 * lw + qw_e + 2 * kw_e
            wq = w[:, o1:o2]
            wk = w[:, o2:o3]
            w_cat = jnp.concatenate([
                w[:, :o1], _split_pairs(wq, win_hd), _split_pairs(wk, win_hd, dup=True), w[:, o3:o4]], axis=1).astype(BF16)
            xa, gg, q, kd, vt = even_in(lay, x_all, mod_l, norm_pre[l, 1], w_cat, cos_e, sin_e, lw, qw_e, 2 * kw_e, kw_e,
                                        win_hd ** -0.5 * LOG2_E)
            spl_args = []
            for direction in range(2):
                w_gate = jnp.concatenate([_block_diag(lru_w_a[i, direction]), _block_diag(lru_w_x[i, direction])], axis=1).astype(BF16)
                b_gate = jnp.concatenate([lru_b_a[i, direction], lru_b_x[i, direction]])
                spl_args.append((w_gate, b_gate, lru_lambda[i, direction]))
            hf = lru_scan(lay, xa, even_conv_w[i], even_conv_b[i], *spl_args[0], reverse=False)
            ya = lru_scan(lay, xa, even_conv_w[i], even_conv_b[i], *spl_args[1], reverse=True, hf=hf, gg=gg)
            sink_rows = jnp.broadcast_to(attn_sink[i][:, None], (win_heads, 128)).astype(F32)
            ob = win_attn(lay, q, kd, vt, sink_rows, win_kv, win_hd)
            mix = (norm_post[l, 1], even_w_out[i].astype(BF16), [ya, ob])
        else:
            w = odd_w_in[i]
            wq, wk, wv = w[:, :qw_o], w[:, qw_o:qw_o + kw_o], w[:, qw_o + kw_o:]
            wq, wk = _split_pairs(wq, glb_hd), _split_pairs(wk, glb_hd)
            w_cat = jnp.concatenate([wq, _swap_halves(wq), wk, _swap_halves(wk), wv], axis=1).astype(BF16)
            gq, gk = (_split_pairs(g[None, :], glb_hd) for g in (odd_q_norm[i], odd_k_norm[i]))
            gq, gk = (jnp.concatenate([g, jnp.roll(g, LANES // 2, axis=1)], axis=0) for g in (gq, gk))
            q, k, vt = odd_in(lay, x_all, mod_l, norm_pre[l, 1], w_cat, cos_o, sin_o, gq, gk, qw_o, kw_o, glb_hd)
            bound = LOG2_E * glb_hd ** 0.5 * jnp.max(jnp.abs(odd_q_norm[i])) * jnp.max(jnp.abs(odd_k_norm[i]))
            o = lax.cond(
                bound <= MAX_UNSHIFTED_SCORE,
                functools.partial(dense_attn, lay, hd=glb_hd, with_ctx_queries=not last, shifted=False),
                functools.partial(dense_attn, lay, hd=glb_hd, with_ctx_queries=not last, shifted=True),
                q, k, vt)
            mix = (norm_post[l, 1], odd_w_out[i].astype(BF16), [o])
        x_all = ffn_sublayer(lay, x_all, rows_out, mod_l, 6, norm_pre[l, 2], norm_post[l, 2], wg16, wu16, wd16, (l, 1),
                             mix=mix)
    return x_all[:b * t].reshape(b, t, d)
```

```python
import functools

import jax
import jax.numpy as jnp
import numpy as np
from jax import lax
from jax.experimental import pallas as pl
from jax.experimental.pallas import tpu as pltpu

F32 = jnp.float32
BF16 = jnp.bfloat16

EPS = 1e-6
FFN_RES = 0.5
LRU_C = 8.0
WINDOW = 128
GRID_W = 64
ROPE_THETA = 10000.0
GQA_GROUP = 4
N_MOD = 9
MOD_ROWS = 16

V7X_VMEM_LIMIT = 56 * 1024 * 1024
Q_TILE_WIN = 256
VT_CHUNK = 128
LRU_CHUNK = 256
LANES = 128
SEG_PAD = 4
HALO = 8
FFN_SUBTILES = 4
KEY_CHUNK = 256
LOG2_E = 1.4426950408889634
MAX_UNSHIFTED_SCORE = 60.0


def _const_spec(shape):
    zeros = (0,) * len(shape)
    return pl.BlockSpec(shape, lambda *_: zeros)


def _resident_spec(shape):
    zeros = (0,) * len(shape)
    return pl.BlockSpec(shape, lambda *_: zeros, pipeline_mode=pl.Buffered(1))


def _dot(a, b):
    return jnp.dot(a, b, preferred_element_type=F32)


def _dot_t(a, b):
    return lax.dot_general(a, b, (((1,), (1,)), ((), ())), preferred_element_type=F32)


def _rope(x, cos, sin_signed):
    return x * cos + pltpu.roll(x, LANES // 2, 1) * sin_signed


def _rms(x, g):
    return x * lax.rsqrt(jnp.mean(x * x, axis=-1, keepdims=True) + EPS) * g


def _pre(x, g, shift, scale):
    return _rms(x, g) * (1.0 + scale) + shift


def _ada_kernel(c_ref, w_ref, b_ref, o_ref):
    c = c_ref[...]
    s = c * jax.nn.sigmoid(c)
    o_ref[...] = _dot(s.astype(BF16), w_ref[...].astype(BF16)) + b_ref[...]


def ada_mod(c_rows, w_ada, b_ada):
    depth, d, n = w_ada.shape
    tn = 1024
    return pl.pallas_call(
        _ada_kernel,
        out_shape=jax.ShapeDtypeStruct((depth, MOD_ROWS, n), F32),
        grid=(depth, n // tn),
        in_specs=[
            pl.BlockSpec((MOD_ROWS, d), lambda l, j: (0, 0)),
            pl.BlockSpec((None, d, tn), lambda l, j: (l, 0, j)),
            pl.BlockSpec((None, 1, tn), lambda l, j: (l, 0, j)),
        ],
        out_specs=pl.BlockSpec((None, MOD_ROWS, tn), lambda l, j: (l, 0, j)),
        name="ada_mod",
    )(c_rows, w_ada, b_ada.reshape(depth, 1, n))


class _Layout:
    def __init__(self, b, t, lc):
        self.b, self.t, self.lc = b, t, lc
        self.n_lat = b * t
        self.nt = b * (t + lc)

    def tile(self, candidates):
        for tm in candidates:
            if self.t % tm == 0 and (self.b * self.lc) % tm == 0:
                return tm
        raise ValueError("no token tile divides both the latent and the context stream")

    def mod_row(self, tm):
        n_lat_tiles, per_b, b = self.n_lat // tm, self.t // tm, self.b
        return lambda i: jnp.where(i < n_lat_tiles, i // per_b, b)

    def rope_block(self, tm):
        n_lat_tiles, per_b = self.n_lat // tm, self.t // tm
        return lambda i: jnp.where(i < n_lat_tiles, i % per_b, per_b)


def _mod_spec(lay, tm):
    row = lay.mod_row(tm)
    return pl.BlockSpec((None, N_MOD, lay_d(lay)), lambda i: (row(i), 0, 0))


def lay_d(lay):
    return lay.d


def _ffn_kernel(x_ref, *rest, m0, ff_chunk, n_lat_tiles, subtiles):
    ctx_ref = None
    if n_lat_tiles is not None:
        ctx_ref, rest = rest[0], rest[1:]
    mod_ref, gpre_ref, gpost_ref, wg_ref, wu_ref, wd_ref = rest[:6]
    rest = rest[6:]
    o_ref, h_ref = rest[-2:]
    tm = x_ref.shape[0]
    d_ff = wg_ref.shape[1]
    for r0 in range(0, tm, tm // subtiles):
        rows = slice(r0, r0 + tm // subtiles)
        x = x_ref[rows, :]
        if ctx_ref is not None:
            x = jnp.where(pl.program_id(0) < n_lat_tiles, x, ctx_ref[rows, :])
        if len(rest) > 2:
            gmix_ref, wmix_ref = rest[:2]
            parts = [r[rows, :] for r in rest[2:-2]]
            y = _dot(jnp.concatenate(parts, axis=1) if len(parts) > 1 else parts[0], wmix_ref[...])
            x = x + mod_ref[5:6, :] * _rms(y, gmix_ref[...])
        u = _pre(x, gpre_ref[...], mod_ref[m0:m0 + 1, :], mod_ref[m0 + 1:m0 + 2, :]).astype(BF16)
        for c0 in range(0, d_ff, ff_chunk):
            g = _dot(u, wg_ref[:, c0:c0 + ff_chunk])
            up = _dot(u, wu_ref[:, c0:c0 + ff_chunk])
            h_ref[rows, c0:c0 + ff_chunk] = (g * jax.nn.sigmoid(g) * up).astype(BF16)
        y = _dot(h_ref[rows, :], wd_ref[...])
        o_ref[rows, :] = x + FFN_RES * mod_ref[m0 + 2:m0 + 3, :] * _rms(y, gpost_ref[...])


def ffn_sublayer(lay, x_all, n_rows, mod_l, m0, g_pre, g_post, wg, wu, wd, which, mix=None):
    d = lay.d
    d_ff = wg.shape[-1]
    picked = lambda r, c: pl.BlockSpec((None, None, r, c), lambda i: which + (0, 0), pipeline_mode=pl.Buffered(1))
    tm = lay.tile((1024, 512, 256, 128))
    row = lambda i: (i, 0)
    n_lat_tiles = None
    if isinstance(x_all, tuple):
        n_lat_tiles = lay.n_lat // tm
        in_specs = [pl.BlockSpec((tm, d), lambda i: (jnp.minimum(i, n_lat_tiles - 1), 0)),
                    pl.BlockSpec((tm, d), lambda i: (jnp.maximum(i - n_lat_tiles, 0), 0))]
        args = list(x_all)
    else:
        in_specs = [pl.BlockSpec((tm, d), row)]
        args = [x_all]
    in_specs += [
        _mod_spec(lay, tm),
        _const_spec((1, d)),
        _const_spec((1, d)),
        picked(d, d_ff),
        picked(d, d_ff),
        picked(d_ff, d),
    ]
    args += [mod_l, g_pre.reshape(1, d), g_post.reshape(1, d), wg, wu, wd]
    if mix is not None:
        g_mix, w_mix, parts = mix
        in_specs += [_const_spec((1, d)), _resident_spec(w_mix.shape)] + [pl.BlockSpec((tm, p.shape[1]), row) for p in parts]
        args += [g_mix.reshape(1, d), w_mix] + list(parts)
    return pl.pallas_call(
        functools.partial(_ffn_kernel, m0=m0, ff_chunk=256, n_lat_tiles=n_lat_tiles,
                          subtiles=FFN_SUBTILES if mix is None and tm % (8 * FFN_SUBTILES) == 0 else 1),
        out_shape=jax.ShapeDtypeStruct((n_rows, d), F32),
        grid=(n_rows // tm,),
        in_specs=in_specs,
        out_specs=pl.BlockSpec((tm, d), row),
        scratch_shapes=[pltpu.VMEM((tm, d_ff), BF16)],
        compiler_params=pltpu.CompilerParams(vmem_limit_bytes=V7X_VMEM_LIMIT),
        name="ffn_sublayer" if mix is None else "mix_ffn_sublayer",
    )(*args)


def _even_in_kernel(x_ref, mod_ref, g_ref, w_ref, cos_ref, sin_ref, xa_ref, gg_ref, q_ref, k_ref, vt_ref, *, lw, qw, kw, vw,
                    q_scale):
    u = _pre(x_ref[...], g_ref[...], mod_ref[3:4, :], mod_ref[4:5, :]).astype(BF16)
    cos, sin = cos_ref[...], sin_ref[...]
    o = 0
    xa_ref[...] = _dot(u, w_ref[:, o:o + lw])
    o += lw
    gg_ref[...] = jax.nn.gelu(_dot(u, w_ref[:, o:o + lw]))
    o += lw
    q = _dot(u, w_ref[:, o:o + qw])
    o += qw
    for c0 in range(0, qw, LANES):
        q_ref[:, c0:c0 + LANES] = (_rope(q[:, c0:c0 + LANES], cos, sin) * q_scale).astype(BF16)
    k = _dot(u, w_ref[:, o:o + kw])
    o += kw
    for c0 in range(0, kw, LANES):
        k_ref[:, c0:c0 + LANES] = _rope(k[:, c0:c0 + LANES], cos, sin).astype(BF16)
    v = _dot(u, w_ref[:, o:o + vw])
    ch = vt_ref.shape[2]
    for c in range(vt_ref.shape[0]):
        vt_ref[c] = v[c * ch:(c + 1) * ch, :].T.astype(BF16)


def even_in(lay, x_all, mod_l, g_pre, w_cat, cos_t, sin_t, lw, qw, kw, vw, q_scale):
    d = lay.d
    tm = lay.tile((512, 256, 128))
    rb = lay.rope_block(tm)
    nt = lay.nt
    row = lambda i: (i, 0)
    ch = VT_CHUNK
    assert tm % ch == 0
    return pl.pallas_call(
        functools.partial(_even_in_kernel, lw=lw, qw=qw, kw=kw, vw=vw, q_scale=q_scale),
        out_shape=(
            jax.ShapeDtypeStruct((nt, lw), F32),
            jax.ShapeDtypeStruct((nt, lw), F32),
            jax.ShapeDtypeStruct((nt, qw), BF16),
            jax.ShapeDtypeStruct((nt, kw), BF16),
            jax.ShapeDtypeStruct((nt // ch, vw, ch), BF16),
        ),
        grid=(nt // tm,),
        in_specs=[
            pl.BlockSpec((tm, d), row),
            _mod_spec(lay, tm),
            _const_spec((1, d)),
            _resident_spec(w_cat.shape),
            pl.BlockSpec((tm, cos_t.shape[1]), lambda i: (rb(i), 0)),
            pl.BlockSpec((tm, cos_t.shape[1]), lambda i: (rb(i), 0)),
        ],
        out_specs=(
            pl.BlockSpec((tm, lw), row),
            pl.BlockSpec((tm, lw), row),
            pl.BlockSpec((tm, qw), row),
            pl.BlockSpec((tm, kw), row),
            pl.BlockSpec((tm // ch, vw, ch), lambda i: (i, 0, 0)),
        ),
        compiler_params=pltpu.CompilerParams(vmem_limit_bytes=V7X_VMEM_LIMIT),
        name="even_in",
    )(x_all, mod_l, g_pre.reshape(1, d), w_cat, cos_t, sin_t)


def _shift_rows(x, dist, fill, reverse):
    n = x.shape[0]
    rows = lax.broadcasted_iota(jnp.int32, x.shape, 0)
    if reverse:
        return jnp.where(rows < n - dist, pltpu.roll(x, n - dist, 0), fill)
    return jnp.where(rows >= dist, pltpu.roll(x, dist, 0), fill)


def _lru_kernel(xm_ref, xl_ref, xr_ref, cw_ref, cb_ref, wg_ref, bg_ref, lam_ref, *rest,
                reverse, n_ctx_chunks, n_lat_chunks):
    if reverse:
        hf_ref, gg_ref, o_ref, xbuf, carry, a_s, b_s, h_s = rest
    else:
        o_ref, xbuf, carry, a_s, b_s, h_s = rest
    s = pl.program_id(1)
    cl, lw = xm_ref.shape
    in_ctx = s < n_ctx_chunks
    pos = jnp.where(in_ctx, s, s - n_ctx_chunks)
    n_seq = jnp.where(in_ctx, n_ctx_chunks, n_lat_chunks)
    chunk = (n_seq - 1 - pos) if reverse else pos

    @pl.when(s == 0)
    def _():
        carry[...] = jnp.zeros_like(carry)

    xbuf[0:HALO, :] = jnp.where(chunk > 0, xl_ref[...], 0.0)
    xbuf[HALO:HALO + cl, :] = xm_ref[...]
    xbuf[HALO + cl:, :] = jnp.where(chunk < n_seq - 1, xr_ref[...], 0.0)
    taps = cw_ref.shape[0]
    left = taps // 2
    xin = cb_ref[...] + xbuf[pl.ds(HALO - left, cl), :] * cw_ref[0:1, :]
    for t in range(1, taps):
        xin = xin + xbuf[pl.ds(HALO - left + t, cl), :] * cw_ref[t:t + 1, :]

    gates = _dot(xin.astype(BF16), wg_ref[...]) + bg_ref[...]
    r = 0.5 * jnp.tanh(0.5 * gates[:, :lw]) + 0.5
    gate_i = 0.5 * jnp.tanh(0.5 * gates[:, lw:]) + 0.5
    log_a = (-LRU_C) * r * jax.nn.softplus(-lam_ref[...])
    a = jnp.exp(log_a)
    b = jnp.sqrt(-jnp.tanh(log_a) * (a * a + 1.0)) * gate_i * xin

    seg = cl // 8
    pitch = a_s.shape[1] // 8
    order = range(seg - 1, -1, -1) if reverse else range(seg)
    for g in range(lw // LANES):
        cs = slice(g * LANES, (g + 1) * LANES)
        for i in range(8):
            a_s[g, i * pitch:i * pitch + seg, :] = a[i * seg:(i + 1) * seg, cs]
            b_s[g, i * pitch:i * pitch + seg, :] = b[i * seg:(i + 1) * seg, cs]
        step = lambda ref, j: ref[g, pl.ds(j, 8, stride=pitch), :]
        for n, j in enumerate(order):
            aj, bj = step(a_s, j), step(b_s, j)
            f, p = (bj, aj) if n == 0 else (aj * f + bj, aj * p)
        first = lax.broadcasted_iota(jnp.int32, f.shape, 0) == (7 if reverse else 0)
        f = f + jnp.where(first, p * carry[:, cs], 0.0)
        dist = 1
        while dist < 8:
            f = f + p * _shift_rows(f, dist, 0.0, reverse)
            p = p * _shift_rows(p, dist, 1.0, reverse)
            dist *= 2
        h = jnp.where(first, carry[:, cs], _shift_rows(f, 1, 0.0, reverse))
        carry[:, cs] = f[0:1, :] if reverse else f[7:8, :]
        for j in order:
            h = step(a_s, j) * h + step(b_s, j)
            h_s[g, pl.ds(j, 8, stride=pitch), :] = h
    h = jnp.concatenate(
        [jnp.concatenate([h_s[g, i * pitch:i * pitch + seg, :] for i in range(8)], axis=0) for g in range(lw // LANES)],
        axis=1)
    if reverse:
        o_ref[...] = ((hf_ref[...] + h) * gg_ref[...]).astype(o_ref.dtype)
    else:
        o_ref[...] = h


def lru_scan(lay, xa, conv_w, conv_b, w_gate, b_gate, lam, reverse, hf=None, gg=None):
    nt, lw = xa.shape
    cl = LRU_CHUNK
    assert lay.t % cl == 0 and lay.lc % cl == 0
    ncc, nlc = lay.lc // cl, lay.t // cl
    n_lat_blocks = lay.n_lat // cl

    def chunk_block(b, s):
        in_ctx = s < ncc
        pos = jnp.where(in_ctx, s, s - ncc)
        if reverse:
            pos = jnp.where(in_ctx, ncc - 1 - pos, nlc - 1 - pos)
        return jnp.where(in_ctx, n_lat_blocks + b * ncc + pos, b * nlc + pos)

    per = cl // HALO
    last_halo = nt // HALO - 1
    main = pl.BlockSpec((cl, lw), lambda b, s: (chunk_block(b, s), 0))
    in_specs = [
        main,
        pl.BlockSpec((HALO, lw), lambda b, s: (jnp.maximum(chunk_block(b, s) * per - 1, 0), 0)),
        pl.BlockSpec((HALO, lw), lambda b, s: (jnp.minimum((chunk_block(b, s) + 1) * per, last_halo), 0)),
        _const_spec(conv_w.shape),
        _const_spec((1, lw)),
        _const_spec(w_gate.shape),
        _const_spec((1, 2 * lw)),
        _const_spec((1, lw)),
    ]
    args = [xa, xa, xa, conv_w, conv_b.reshape(1, lw), w_gate, b_gate.reshape(1, 2 * lw), lam.reshape(1, lw)]
    if reverse:
        in_specs += [main, main]
        args += [hf, gg]
    return pl.pallas_call(
        functools.partial(_lru_kernel, reverse=reverse, n_ctx_chunks=ncc, n_lat_chunks=nlc),
        out_shape=jax.ShapeDtypeStruct((nt, lw), BF16 if reverse else F32),
        grid=(lay.b, ncc + nlc),
        in_specs=in_specs,
        out_specs=main,
        scratch_shapes=[pltpu.VMEM((cl + 2 * HALO, lw), F32), pltpu.VMEM((1, lw), F32)]
        + [pltpu.VMEM((lw // LANES, cl + 8 * SEG_PAD, LANES), F32)] * 3,
        compiler_params=pltpu.CompilerParams(dimension_semantics=("arbitrary", "arbitrary")),
        name="lru_bwd" if reverse else "lru_fwd",
    )(*args)


def _win_attn_kernel(q_ref, kl_ref, kx_ref, vl_ref, vx_ref, sink_ref, o_ref, *, n_q_lat, n_kv, hd):
    j = pl.program_id(1)
    tq = q_ref.shape[0]
    t = kl_ref.shape[0]
    lc = kx_ref.shape[0]
    vch = vl_ref.shape[2]
    band = tq + 2 * WINDOW
    cols = GQA_GROUP * tq
    low = lax.broadcasted_iota(jnp.int32, (tq, 2 * hd), 1) % hd < hd // 2

    def colsum(x):
        return jnp.sum(jnp.sum(x.reshape(-1, 8, cols), axis=0), axis=0, keepdims=True)

    def colmax(x):
        return jnp.max(jnp.max(x.reshape(-1, 8, cols), axis=0), axis=0, keepdims=True)

    def attend(kv, k_parts, vt_parts, bias):
        qs = []
        for g in range(GQA_GROUP):
            h = kv * GQA_GROUP + g
            qp = q_ref[:, 2 * hd * (h // 2):2 * hd * (h // 2 + 1)]
            qs.append(jnp.where(low if h % 2 == 0 else jnp.logical_not(low), qp, jnp.zeros_like(qp)))
        qs = jnp.concatenate(qs, axis=0)
        sink = jnp.concatenate(
            [jnp.concatenate([sink_ref[kv * GQA_GROUP + g:kv * GQA_GROUP + g + 1, :]] * (tq // 128), axis=1)
             for g in range(GQA_GROUP)], axis=1) * LOG2_E
        s = [_dot_t(kp, qs) for kp in k_parts]
        if bias is not None:
            s[0] = s[0] + bias
        m = sink
        for sp in s:
            m = jnp.maximum(m, colmax(sp))
        den = jnp.exp2(sink - m)
        p = []
        for sp in s:
            pp = jnp.exp2(sp - m)
            den = den + colsum(pp)
            p.append(pp.astype(BF16))
        o_t = _dot(jnp.concatenate(vt_parts, axis=1), jnp.concatenate(p, axis=0)) * (1.0 / den)
        for pair in range(GQA_GROUP // 2):
            both = jnp.concatenate([o_t[:, (2 * pair + i) * tq:(2 * pair + i + 1) * tq] for i in range(2)], axis=0)
            c0 = 2 * hd * (kv * GQA_GROUP // 2 + pair)
            o_ref[:, c0:c0 + 2 * hd] = both.T.astype(o_ref.dtype)

    def ctx_parts(kv):
        ks = kx_ref[:, 2 * hd * kv:2 * hd * (kv + 1)]
        vts = [vx_ref[c][hd * kv:hd * (kv + 1), :] for c in range(lc // vch)]
        return ks, vts

    @pl.when(j < n_q_lat)
    def _():
        start = pl.multiple_of(jnp.clip(j * tq - WINDOW, 0, t - band), 128)
        kpos = start + lax.broadcasted_iota(jnp.int32, (band, tq), 0)
        qpos = j * tq + lax.broadcasted_iota(jnp.int32, (band, tq), 1)
        bias = jnp.where(jnp.abs(kpos - qpos) > WINDOW, -jnp.inf, 0.0).astype(F32)
        bias = jnp.concatenate([bias] * GQA_GROUP, axis=1)
        c0 = start // vch
        for kv in range(n_kv):
            kx, vxs = ctx_parts(kv)
            kb = kl_ref[pl.ds(start, band), 2 * hd * kv:2 * hd * (kv + 1)]
            vbs = [vl_ref[c0 + c][hd * kv:hd * (kv + 1), :] for c in range(band // vch)]
            attend(kv, [kb, kx], vbs + vxs, bias)

    @pl.when(j >= n_q_lat)
    def _():
        for kv in range(n_kv):
            kx, vxs = ctx_parts(kv)
            attend(kv, [kx], vxs, None)


def win_attn(lay, q, kd, vt, sink_rows, n_kv, hd):
    nt, qw = q.shape
    kw = kd.shape[1]
    tq = Q_TILE_WIN
    vch = vt.shape[2]
    assert lay.t % tq == 0 and lay.lc % tq == 0 and lay.n_lat % lay.lc == 0 and tq % 128 == 0
    assert vch == 128 and WINDOW % vch == 0 and lay.lc % vch == 0 and lay.t >= tq + 2 * WINDOW
    nb, ncq = lay.t // tq, lay.lc // tq
    ctx_q0 = lay.n_lat // tq
    ctx_k0 = lay.n_lat // lay.lc

    def qblock(b, j):
        return jnp.where(j < nb, b * nb + j, ctx_q0 + b * ncq + (j - nb))

    qspec = pl.BlockSpec((tq, qw), lambda b, j: (qblock(b, j), 0))
    return pl.pallas_call(
        functools.partial(_win_attn_kernel, n_q_lat=nb, n_kv=n_kv, hd=hd),
        out_shape=jax.ShapeDtypeStruct((nt, qw), BF16),
        grid=(lay.b, nb + ncq),
        in_specs=[
            qspec,
            pl.BlockSpec((lay.t, kw), lambda b, j: (b, 0)),
            pl.BlockSpec((lay.lc, kw), lambda b, j: (ctx_k0 + b, 0)),
            pl.BlockSpec((lay.t // vch, vt.shape[1], vch), lambda b, j: (b, 0, 0)),
            pl.BlockSpec((lay.lc // vch, vt.shape[1], vch), lambda b, j: (ctx_k0 + b, 0, 0)),
            _const_spec(sink_rows.shape),
        ],
        out_specs=qspec,
        compiler_params=pltpu.CompilerParams(vmem_limit_bytes=V7X_VMEM_LIMIT),
        name="win_attn",
    )(q, kd, kd, vt, vt, sink_rows)


def _odd_in_kernel(x_ref, mod_ref, g_ref, w_ref, cos_ref, sin_ref, gq_ref, gk_ref, q_ref, k_ref, vt_ref, *, qw, kw, hd):
    u = _pre(x_ref[...], g_ref[...], mod_ref[3:4, :], mod_ref[4:5, :]).astype(BF16)
    cos, sin = cos_ref[...], sin_ref[...]

    def normed_rope(o, width, gains, out_ref, post):
        cg = cos * gains[0:1, :]
        sg = sin * gains[1:2, :]
        step = 2 * hd
        for c0 in range(0, width, step):
            x2 = _dot(u, w_ref[:, o + c0:o + c0 + step])
            xp2 = _dot(u, w_ref[:, o + width + c0:o + width + c0 + step])
            for h0 in range(0, step, hd):
                x, xp = x2[:, h0:h0 + hd], xp2[:, h0:h0 + hd]
                r = lax.rsqrt(jnp.mean(x * x, axis=-1, keepdims=True) + EPS) * post
                out_ref[:, c0 + h0:c0 + h0 + hd] = (r * (x * cg + xp * sg)).astype(out_ref.dtype)

    normed_rope(0, qw, gq_ref[...], q_ref, hd ** -0.5 * LOG2_E)
    normed_rope(2 * qw, kw, gk_ref[...], k_ref, 1.0)
    o = 2 * qw + 2 * kw
    v = _dot(u, w_ref[:, o:o + kw])
    ch = vt_ref.shape[2]
    for c in range(vt_ref.shape[0]):
        vt_ref[c] = v[c * ch:(c + 1) * ch, :].T.astype(BF16)


def odd_in(lay, x_all, mod_l, g_pre, w_cat, cos_t, sin_t, gq, gk, qw, kw, hd):
    d = lay.d
    tm = lay.tile((512, 256, 128))
    rb = lay.rope_block(tm)
    nt = lay.nt
    row = lambda i: (i, 0)
    ch = KEY_CHUNK
    assert tm % ch == 0
    return pl.pallas_call(
        functools.partial(_odd_in_kernel, qw=qw, kw=kw, hd=hd),
        out_shape=(
            jax.ShapeDtypeStruct((nt, qw), BF16),
            jax.ShapeDtypeStruct((nt, kw), BF16),
            jax.ShapeDtypeStruct((nt // ch, kw, ch), BF16),
        ),
        grid=(nt // tm,),
        in_specs=[
            pl.BlockSpec((tm, d), row),
            _mod_spec(lay, tm),
            _const_spec((1, d)),
            _resident_spec(w_cat.shape),
            pl.BlockSpec((tm, hd), lambda i: (rb(i), 0)),
            pl.BlockSpec((tm, hd), lambda i: (rb(i), 0)),
            _const_spec(gq.shape),
            _const_spec(gk.shape),
        ],
        out_specs=(pl.BlockSpec((tm, qw), row), pl.BlockSpec((tm, kw), row),
                   pl.BlockSpec((tm // ch, kw, ch), lambda i: (i, 0, 0))),
        compiler_params=pltpu.CompilerParams(vmem_limit_bytes=V7X_VMEM_LIMIT),
        name="odd_in",
    )(x_all, mod_l, g_pre.reshape(1, d), w_cat, cos_t, sin_t, gq, gk)


def _dense_attn_kernel(q_ref, kl_ref, kx_ref, vl_ref, vx_ref, o_ref, s_ref, m_ref, l_ref, acc_ref, *, n_q_lat, nb):
    j = pl.program_id(2)
    tq = q_ref.shape[0]
    t, hd = kl_ref.shape
    lc = kx_ref.shape[0]
    ch = vl_ref.shape[2]
    cols = GQA_GROUP * tq
    qs = jnp.concatenate([q_ref[:, g * hd:(g + 1) * hd] for g in range(GQA_GROUP)], axis=0)

    def sum8(x):
        return jnp.sum(x.reshape(-1, 8, cols), axis=0)

    def colmax(x):
        return jnp.max(jnp.max(x.reshape(-1, 8, cols), axis=0), axis=0, keepdims=True)

    sx = _dot_t(kx_ref[...], qs)
    m0 = colmax(sx)
    p = jnp.exp2(sx - m0)
    m_ref[...] = m0
    l_ref[...] = sum8(p)
    acc_ref[...] = _dot(jnp.concatenate([vx_ref[c] for c in range(lc // ch)], axis=1), p.astype(BF16))

    @pl.when(j < n_q_lat)
    def _():
        n_blocks = t // nb

        def produce(i):
            s = _dot_t(kl_ref[i * nb:(i + 1) * nb, :], qs)
            s_ref[i % 2] = s
            return colmax(s)

        m = m_ref[...]
        block_max = produce(0)
        for i in range(n_blocks):
            m_new = jnp.maximum(m, block_max)
            if i + 1 < n_blocks:
                block_max = produce(i + 1)
            alpha = jnp.exp2(m - m_new)
            p = jnp.exp2(s_ref[i % 2] - m_new)
            vt = jnp.concatenate([vl_ref[i * (nb // ch) + c] for c in range(nb // ch)], axis=1)
            l_ref[...] = l_ref[...] * alpha + sum8(p)
            acc_ref[...] = acc_ref[...] * alpha + _dot(vt, p.astype(BF16))
            m = m_new

    o_t = acc_ref[...] * (1.0 / jnp.sum(l_ref[...], axis=0, keepdims=True))
    for g in range(GQA_GROUP):
        o_ref[:, g * hd:(g + 1) * hd] = o_t[:, g * tq:(g + 1) * tq].T.astype(o_ref.dtype)


def _dense_attn_unshifted_kernel(q_ref, kl_ref, kx_ref, vl_ref, vx_ref, o_ref, l_ref, acc_ref, *, n_q_lat, nb):
    j = pl.program_id(2)
    tq = q_ref.shape[0]
    t, hd = kl_ref.shape
    lc = kx_ref.shape[0]
    ch = vl_ref.shape[2]
    cols = GQA_GROUP * tq
    qs = jnp.concatenate([q_ref[:, g * hd:(g + 1) * hd] for g in range(GQA_GROUP)], axis=0)

    def block(k, vts):
        p = jnp.exp2(_dot_t(k, qs))
        return jnp.sum(p.reshape(-1, 8, cols), axis=0), _dot(jnp.concatenate(vts, axis=1), p.astype(BF16))

    l8, acc = block(kx_ref[...], [vx_ref[c] for c in range(lc // ch)])
    l_ref[...] = l8
    acc_ref[...] = acc

    @pl.when(j < n_q_lat)
    def _():
        l8, acc = l_ref[...], acc_ref[...]
        for i in range(t // nb):
            dl, da = block(kl_ref[i * nb:(i + 1) * nb, :], [vl_ref[i * (nb // ch) + c] for c in range(nb // ch)])
            l8, acc = l8 + dl, acc + da
        l_ref[...] = l8
        acc_ref[...] = acc

    o_t = acc_ref[...] * (1.0 / jnp.sum(l_ref[...], axis=0, keepdims=True))
    for g in range(GQA_GROUP):
        o_ref[:, g * hd:(g + 1) * hd] = o_t[:, g * tq:(g + 1) * tq].T.astype(o_ref.dtype)


def dense_attn(lay, q, k, vt, hd, with_ctx_queries, shifted):
    nt, qw = q.shape
    tq = 128 if shifted else 256
    ch = vt.shape[2]
    assert lay.t % tq == 0 and lay.lc % tq == 0 and lay.n_lat % lay.lc == 0 and lay.t % ch == 0 and lay.lc % ch == 0
    n_kv = k.shape[1] // hd
    gw = GQA_GROUP * hd
    nql, nqc = lay.t // tq, lay.lc // tq
    ctx_q0 = lay.n_lat // tq
    ctx_k0 = lay.n_lat // lay.lc
    n_rows = nt if with_ctx_queries else lay.n_lat

    def qblock(b, kv, j):
        return (jnp.where(j < nql, b * nql + j, ctx_q0 + b * nqc + (j - nql)), kv)

    qspec = pl.BlockSpec((tq, gw), qblock)
    cols = GQA_GROUP * tq
    nb = 1024 if lay.t % 1024 == 0 else ch
    assert nb % ch == 0 and lay.t % nb == 0
    stats = [pltpu.VMEM((8, cols), F32), pltpu.VMEM((hd, cols), F32)]
    if shifted:
        body = _dense_attn_kernel
        scratch = [pltpu.VMEM((2, nb, cols), F32), pltpu.VMEM((1, cols), F32)] + stats
    else:
        body = _dense_attn_unshifted_kernel
        scratch = stats
    return pl.pallas_call(
        functools.partial(body, n_q_lat=nql, nb=nb),
        out_shape=jax.ShapeDtypeStruct((n_rows, qw), BF16),
        grid=(lay.b, n_kv, nql + (nqc if with_ctx_queries else 0)),
        in_specs=[
            qspec,
            pl.BlockSpec((lay.t, hd), lambda b, kv, j: (b, kv)),
            pl.BlockSpec((lay.lc, hd), lambda b, kv, j: (ctx_k0 + b, kv)),
            pl.BlockSpec((lay.t // ch, hd, ch), lambda b, kv, j: (b, kv, 0)),
            pl.BlockSpec((lay.lc // ch, hd, ch), lambda b, kv, j: (ctx_k0 + b, kv, 0)),
        ],
        out_specs=qspec,
        scratch_shapes=scratch,
        compiler_params=pltpu.CompilerParams(vmem_limit_bytes=V7X_VMEM_LIMIT),
        name="dense_attn" if shifted else "dense_attn_unshifted",
    )(q, k, k, vt, vt)


def _split_pairs(w, hd, dup=False):
    d = w.shape[0]
    pairs = w.reshape(d, -1, hd // 2, 2)
    if dup:
        pairs = jnp.concatenate([pairs[:, :, None], pairs[:, :, None]], axis=2).reshape(d, -1, hd // 2, 2)
    per_group = LANES // hd
    groups = pairs.reshape(d, -1, per_group, hd // 2, 2)
    return jnp.moveaxis(groups, 4, 2).reshape(d, -1)


def _swap_halves(w):
    return jnp.flip(w.reshape(w.shape[0], -1, 2, LANES // 2), axis=2).reshape(w.shape)


def _block_diag(w):
    h, n, _ = w.shape
    eye = jnp.eye(h, dtype=w.dtype)
    return (eye[:, None, :, None] * w[:, :, None, :]).reshape(h * n, h * n)


def _rope_tables(t, hd, ident_rows):
    n_freq = hd // 4
    freq = ROPE_THETA ** (-jnp.arange(n_freq, dtype=F32) / n_freq)
    pos = jnp.arange(t)
    row = (pos // GRID_W).astype(F32)
    col = (pos % GRID_W).astype(F32)
    ang = jnp.concatenate([row[:, None] * freq, col[:, None] * freq], axis=-1)
    reps = LANES // hd
    cos = jnp.tile(jnp.cos(ang), (1, reps))
    sin = jnp.tile(jnp.sin(ang), (1, reps))
    cos = jnp.concatenate([cos, jnp.ones((ident_rows, LANES // 2), F32)], axis=0)
    sin = jnp.concatenate([sin, jnp.zeros((ident_rows, LANES // 2), F32)], axis=0)
    return jnp.concatenate([cos, cos], axis=1), jnp.concatenate([-sin, sin], axis=1)


def kernel(x, c, ctx, c_ctx, w_ada, b_ada, norm_pre, norm_post, ffn_w_gate, ffn_w_up, ffn_w_down, even_w_in,
           even_conv_w, even_conv_b, lru_w_a, lru_b_a, lru_w_x, lru_b_x, lru_lambda, attn_sink, even_w_out, odd_w_in,
           odd_q_norm, odd_k_norm, odd_w_out):
    b, t, d = x.shape
    lc = ctx.shape[1]
    depth = w_ada.shape[0]
    lay = _Layout(b, t, lc)
    lay.d = d
    assert b + 1 <= MOD_ROWS and w_ada.shape[2] == N_MOD * d

    lw = even_conv_w.shape[2]
    win_heads = attn_sink.shape[1]
    win_hd = (d - lw) // win_heads
    win_kv = win_heads // GQA_GROUP
    glb_hd = odd_q_norm.shape[1]
    glb_heads = d // glb_hd
    glb_kv = glb_heads // GQA_GROUP
    qw_e, kw_e = win_heads * win_hd, win_kv * win_hd
    qw_o, kw_o = glb_heads * glb_hd, glb_kv * glb_hd

    x_all = (x.reshape(b * t, d), ctx.reshape(b * lc, d))
    c_rows = jnp.concatenate([c, c_ctx[None, :], jnp.zeros((MOD_ROWS - b - 1, d), F32)], axis=0)
    mod = ada_mod(c_rows, w_ada, b_ada).reshape(depth, MOD_ROWS, N_MOD, d)

    tm = lay.tile((512, 256, 128))
    cos_e, sin_e = _rope_tables(t, win_hd, tm)
    cos_o, sin_o = _rope_tables(t, glb_hd, tm)

    wg16, wu16, wd16 = ffn_w_gate.astype(BF16), ffn_w_up.astype(BF16), ffn_w_down.astype(BF16)

    for l in range(depth):
        last = l == depth - 1
        i = l // 2
        mod_l = mod[l]
        x_all = ffn_sublayer(lay, x_all, lay.nt, mod_l, 0, norm_pre[l, 0], norm_post[l, 0], wg16, wu16, wd16, (l, 0))
        rows_out = lay.n_lat if last else lay.nt
        if l % 2 == 0:
            w = even_w_in[i]
            o1, o2, o3, o4 = 2 * lw, 2 * lw + qw_e, 2 * lw + qw_e + kw_e, 2 * lw + qw_e + 2 * kw_e
            wq = w[:, o1:o2]
            wk = w[:, o2:o3]
            w_cat = jnp.concatenate([
                w[:, :o1], _split_pairs(wq, win_hd), _split_pairs(wk, win_hd, dup=True), w[:, o3:o4]], axis=1).astype(BF16)
            xa, gg, q, kd, vt = even_in(lay, x_all, mod_l, norm_pre[l, 1], w_cat, cos_e, sin_e, lw, qw_e, 2 * kw_e, kw_e,
                                        win_hd ** -0.5 * LOG2_E)
            spl_args = []
            for direction in range(2):
                w_gate = jnp.concatenate([_block_diag(lru_w_a[i, direction]), _block_diag(lru_w_x[i, direction])], axis=1).astype(BF16)
                b_gate = jnp.concatenate([lru_b_a[i, direction], lru_b_x[i, direction]])
                spl_args.append((w_gate, b_gate, lru_lambda[i, direction]))
            hf = lru_scan(lay, xa, even_conv_w[i], even_conv_b[i], *spl_args[0], reverse=False)
            ya = lru_scan(lay, xa, even_conv_w[i], even_conv_b[i], *spl_args[1], reverse=True, hf=hf, gg=gg)
            sink_rows = jnp.broadcast_to(attn_sink[i][:, None], (win_heads, 128)).astype(F32)
            ob = win_attn(lay, q, kd, vt, sink_rows, win_kv, win_hd)
            mix = (norm_post[l, 1], even_w_out[i].astype(BF16), [ya, ob])
        else:
            w = odd_w_in[i]
            wq, wk, wv = w[:, :qw_o], w[:, qw_o:qw_o + kw_o], w[:, qw_o + kw_o:]
            wq, wk = _split_pairs(wq, glb_hd), _split_pairs(wk, glb_hd)
            w_cat = jnp.concatenate([wq, _swap_halves(wq), wk, _swap_halves(wk), wv], axis=1).astype(BF16)
            gq, gk = (_split_pairs(g[None, :], glb_hd) for g in (odd_q_norm[i], odd_k_norm[i]))
            gq, gk = (jnp.concatenate([g, jnp.roll(g, LANES // 2, axis=1)], axis=0) for g in (gq, gk))
            q, k, vt = odd_in(lay, x_all, mod_l, norm_pre[l, 1], w_cat, cos_o, sin_o, gq, gk, qw_o, kw_o, glb_hd)
            bound = LOG2_E * glb_hd ** 0.5 * jnp.max(jnp.abs(odd_q_norm[i])) * jnp.max(jnp.abs(odd_k_norm[i]))
            o = lax.cond(
                bound <= MAX_UNSHIFTED_SCORE,
                functools.partial(dense_attn, lay, hd=glb_hd, with_ctx_queries=not last, shifted=False),
                functools.partial(dense_attn, lay, hd=glb_hd, with_ctx_queries=not last, shifted=True),
                q, k, vt)
            mix = (norm_post[l, 1], odd_w_out[i].astype(BF16), [o])
        x_all = ffn_sublayer(lay, x_all, rows_out, mod_l, 6, norm_pre[l, 2], norm_post[l, 2], wg16, wu16, wd16, (l, 1),
                             mix=mix)
    return x_all[:b * t].reshape(b, t, d)
```

```python
import functools

import jax
import jax.numpy as jnp
from jax import lax
from jax.experimental import pallas as pl
from jax.experimental.pallas import tpu as pltpu

F32 = jnp.float32
BF16 = jnp.bfloat16

EPS = 1e-6
FFN_RES = 0.5
LRU_C = 8.0
WINDOW = 128
GRID_W = 64
ROPE_THETA = 10000.0
GQA_GROUP = 4
N_MOD = 9
MOD_ROWS = 16

V7X_VMEM_LIMIT = 56 * 1024 * 1024
Q_TILE_WIN = 256
IN_TILES = (1024, 512, 256, 128)
WIN_SUB = 256
VT_CHUNK = 128
LRU_CHUNK = 256
LANES = 128
SEG_PAD = 4
HALO = 8
FFN_SUBTILES = 4
KEY_CHUNK = 256
LOG2_E = 1.4426950408889634
MAX_UNSHIFTED_SCORE = 60.0


def _const_spec(shape):
    zeros = (0,) * len(shape)
    return pl.BlockSpec(shape, lambda *_: zeros)


def _resident_spec(shape):
    zeros = (0,) * len(shape)
    return pl.BlockSpec(shape, lambda *_: zeros, pipeline_mode=pl.Buffered(1))


def _dot(a, b):
    return jnp.dot(a, b, preferred_element_type=F32)


def _dot_t(a, b):
    return lax.dot_general(a, b, (((1,), (1,)), ((), ())), preferred_element_type=F32)


def _rope(x, cos, sin_signed):
    return x * cos + pltpu.roll(x, LANES // 2, 1) * sin_signed


def _rms(x, g):
    return x * lax.rsqrt(jnp.mean(x * x, axis=-1, keepdims=True) + EPS) * g


def _pre(x, g, shift, scale):
    return _rms(x, g) * (1.0 + scale) + shift


def _ada_kernel(c_ref, w_ref, b_ref, o_ref):
    c = c_ref[...]
    s = c * jax.nn.sigmoid(c)
    o_ref[...] = _dot(s.astype(BF16), w_ref[...].astype(BF16)) + b_ref[...]


def ada_mod(c_rows, w_ada, b_ada):
    depth, d, n = w_ada.shape
    tn = 1024
    return pl.pallas_call(
        _ada_kernel,
        out_shape=jax.ShapeDtypeStruct((depth, MOD_ROWS, n), F32),
        grid=(depth, n // tn),
        in_specs=[
            pl.BlockSpec((MOD_ROWS, d), lambda l, j: (0, 0)),
            pl.BlockSpec((None, d, tn), lambda l, j: (l, 0, j)),
            pl.BlockSpec((None, 1, tn), lambda l, j: (l, 0, j)),
        ],
        out_specs=pl.BlockSpec((None, MOD_ROWS, tn), lambda l, j: (l, 0, j)),
        name="ada_mod",
    )(c_rows, w_ada, b_ada.reshape(depth, 1, n))


class _Layout:
    def __init__(self, b, t, lc, d):
        self.b, self.t, self.lc, self.d = b, t, lc, d
        self.n_lat = b * t
        self.nt = b * (t + lc)

    def tile(self, candidates):
        for tm in candidates:
            if self.t % tm == 0 and (self.b * self.lc) % tm == 0:
                return tm
        raise ValueError("no token tile divides both the latent and the context stream")

    def mod_row(self, tm):
        n_lat_tiles, per_b, b = self.n_lat // tm, self.t // tm, self.b
        return lambda i: jnp.where(i < n_lat_tiles, i // per_b, b)

    def rope_block(self, tm):
        n_lat_tiles, per_b = self.n_lat // tm, self.t // tm
        return lambda i: jnp.where(i < n_lat_tiles, i % per_b, per_b)


def _mod_spec(lay, tm):
    row = lay.mod_row(tm)
    return pl.BlockSpec((None, N_MOD, lay.d), lambda i: (row(i), 0, 0))


def _ffn_kernel(x_ref, *rest, m0, ff_chunk, n_lat_tiles, subtiles):
    ctx_ref = None
    if n_lat_tiles is not None:
        ctx_ref, rest = rest[0], rest[1:]
    mod_ref, gpre_ref, gpost_ref, wg_ref, wu_ref, wd_ref = rest[:6]
    rest = rest[6:]
    o_ref, h_ref = rest[-2:]
    tm = x_ref.shape[0]
    d_ff = wg_ref.shape[1]
    sub_rows = [slice(r0, r0 + tm // subtiles) for r0 in range(0, tm, tm // subtiles)]
    chunks = list(range(0, d_ff, ff_chunk))
    xs, us, ys = {}, {}, {}

    def before(k):
        rows = sub_rows[k]
        x = x_ref[rows, :]
        if ctx_ref is not None:
            x = jnp.where(pl.program_id(0) < n_lat_tiles, x, ctx_ref[rows, :])
        if len(rest) > 2:
            gmix_ref, wmix_ref = rest[:2]
            parts = [r[rows, :] for r in rest[2:-2]]
            y = _dot(jnp.concatenate(parts, axis=1) if len(parts) > 1 else parts[0], wmix_ref[...])
            x = x + mod_ref[5:6, :] * _rms(y, gmix_ref[...])
        xs[k] = x
        us[k] = _pre(x, gpre_ref[...], mod_ref[m0:m0 + 1, :], mod_ref[m0 + 1:m0 + 2, :]).astype(BF16)

    def expand(k, some_chunks):
        for c0 in some_chunks:
            g = _dot(us[k], wg_ref[:, c0:c0 + ff_chunk])
            up = _dot(us[k], wu_ref[:, c0:c0 + ff_chunk])
            h_ref[sub_rows[k], c0:c0 + ff_chunk] = (g * jax.nn.sigmoid(g) * up).astype(BF16)

    def after(k):
        o_ref[sub_rows[k], :] = xs[k] + FFN_RES * mod_ref[m0 + 2:m0 + 3, :] * _rms(ys[k], gpost_ref[...])

    before(0)
    for k in range(subtiles):
        expand(k, chunks[:len(chunks) // 2])
        if k > 0:
            after(k - 1)
        if k + 1 < subtiles:
            before(k + 1)
        expand(k, chunks[len(chunks) // 2:])
        ys[k] = _dot(h_ref[sub_rows[k], :], wd_ref[...])
    after(subtiles - 1)


def ffn_sublayer(lay, x_all, n_rows, mod_l, m0, g_pre, g_post, wg, wu, wd, which, mix=None):
    d = lay.d
    d_ff = wg.shape[-1]
    picked = lambda r, c: pl.BlockSpec((None, None, r, c), lambda i: which + (0, 0), pipeline_mode=pl.Buffered(1))
    tm = lay.tile((1024, 512, 256, 128))
    row = lambda i: (i, 0)
    n_lat_tiles = None
    if isinstance(x_all, tuple):
        n_lat_tiles = lay.n_lat // tm
        in_specs = [pl.BlockSpec((tm, d), lambda i: (jnp.minimum(i, n_lat_tiles - 1), 0)),
                    pl.BlockSpec((tm, d), lambda i: (jnp.maximum(i - n_lat_tiles, 0), 0))]
        args = list(x_all)
    else:
        in_specs = [pl.BlockSpec((tm, d), row)]
        args = [x_all]
    in_specs += [
        _mod_spec(lay, tm),
        _const_spec((1, d)),
        _const_spec((1, d)),
        picked(d, d_ff),
        picked(d, d_ff),
        picked(d_ff, d),
    ]
    args += [mod_l, g_pre.reshape(1, d), g_post.reshape(1, d), wg, wu, wd]
    if mix is not None:
        g_mix, w_mix, parts = mix
        in_specs += [_const_spec((1, d)), _resident_spec(w_mix.shape)] + [pl.BlockSpec((tm, p.shape[1]), row) for p in parts]
        args += [g_mix.reshape(1, d), w_mix] + list(parts)
    return pl.pallas_call(
        functools.partial(_ffn_kernel, m0=m0, ff_chunk=256, n_lat_tiles=n_lat_tiles,
                          subtiles=FFN_SUBTILES if mix is None and tm % (8 * FFN_SUBTILES) == 0 else 1),
        out_shape=jax.ShapeDtypeStruct((n_rows, d), F32),
        grid=(n_rows // tm,),
        in_specs=in_specs,
        out_specs=pl.BlockSpec((tm, d), row),
        scratch_shapes=[pltpu.VMEM((tm, d_ff), BF16)],
        compiler_params=pltpu.CompilerParams(vmem_limit_bytes=V7X_VMEM_LIMIT),
        name="ffn_sublayer" if mix is None else "mix_ffn_sublayer",
    )(*args)


def _even_in_kernel(x_ref, mod_ref, g_ref, w_ref, cos_ref, sin_ref, xa_ref, gg_ref, q_ref, k_ref, vt_ref, *, lw, qw, kw, vw,
                    q_scale):
    u = _pre(x_ref[...], g_ref[...], mod_ref[3:4, :], mod_ref[4:5, :]).astype(BF16)
    cos, sin = cos_ref[...], sin_ref[...]
    o = 0
    xa_ref[...] = _dot(u, w_ref[:, o:o + lw])
    o += lw
    gg_ref[...] = jax.nn.gelu(_dot(u, w_ref[:, o:o + lw]))
    o += lw
    q = _dot(u, w_ref[:, o:o + qw])
    o += qw
    for c0 in range(0, qw, LANES):
        q_ref[:, c0:c0 + LANES] = (_rope(q[:, c0:c0 + LANES], cos, sin) * q_scale).astype(BF16)
    k = _dot(u, w_ref[:, o:o + kw])
    o += kw
    for c0 in range(0, kw, LANES):
        k_ref[:, c0:c0 + LANES] = _rope(k[:, c0:c0 + LANES], cos, sin).astype(BF16)
    v = _dot(u, w_ref[:, o:o + vw])
    ch = vt_ref.shape[2]
    for c in range(vt_ref.shape[0]):
        vt_ref[c] = v[c * ch:(c + 1) * ch, :].T.astype(BF16)


def even_in(lay, x_all, mod_l, g_pre, w_cat, cos_t, sin_t, lw, qw, kw, vw, q_scale):
    d = lay.d
    tm = lay.tile(IN_TILES)
    rb = lay.rope_block(tm)
    nt = lay.nt
    row = lambda i: (i, 0)
    ch = VT_CHUNK
    assert tm % ch == 0
    return pl.pallas_call(
        functools.partial(_even_in_kernel, lw=lw, qw=qw, kw=kw, vw=vw, q_scale=q_scale),
        out_shape=(
            jax.ShapeDtypeStruct((nt, lw), F32),
            jax.ShapeDtypeStruct((nt, lw), F32),
            jax.ShapeDtypeStruct((nt, qw), BF16),
            jax.ShapeDtypeStruct((nt, kw), BF16),
            jax.ShapeDtypeStruct((nt // ch, vw, ch), BF16),
        ),
        grid=(nt // tm,),
        in_specs=[
            pl.BlockSpec((tm, d), row),
            _mod_spec(lay, tm),
            _const_spec((1, d)),
            _resident_spec(w_cat.shape),
            pl.BlockSpec((tm, cos_t.shape[1]), lambda i: (rb(i), 0)),
            pl.BlockSpec((tm, cos_t.shape[1]), lambda i: (rb(i), 0)),
        ],
        out_specs=(
            pl.BlockSpec((tm, lw), row),
            pl.BlockSpec((tm, lw), row),
            pl.BlockSpec((tm, qw), row),
            pl.BlockSpec((tm, kw), row),
            pl.BlockSpec((tm // ch, vw, ch), lambda i: (i, 0, 0)),
        ),
        compiler_params=pltpu.CompilerParams(vmem_limit_bytes=V7X_VMEM_LIMIT),
        name="even_in",
    )(x_all, mod_l, g_pre.reshape(1, d), w_cat, cos_t, sin_t)


def _shift_rows(x, dist, fill, reverse):
    n = x.shape[0]
    rows = lax.broadcasted_iota(jnp.int32, x.shape, 0)
    if reverse:
        return jnp.where(rows < n - dist, pltpu.roll(x, n - dist, 0), fill)
    return jnp.where(rows >= dist, pltpu.roll(x, dist, 0), fill)


def _lru_kernel(xm_ref, xl_ref, xr_ref, cw_ref, cb_ref, wg_ref, bg_ref, lam_ref, *rest,
                reverse, n_ctx_chunks, n_lat_chunks):
    if reverse:
        hf_ref, gg_ref, o_ref, xbuf, carry, a_s, b_s, h_s = rest
    else:
        o_ref, xbuf, carry, a_s, b_s, h_s = rest
    s = pl.program_id(1)
    cl, lw = xm_ref.shape
    in_ctx = s < n_ctx_chunks
    pos = jnp.where(in_ctx, s, s - n_ctx_chunks)
    n_seq = jnp.where(in_ctx, n_ctx_chunks, n_lat_chunks)
    chunk = (n_seq - 1 - pos) if reverse else pos

    @pl.when(s == 0)
    def _():
        carry[...] = jnp.zeros_like(carry)

    xbuf[0:HALO, :] = jnp.where(chunk > 0, xl_ref[...], 0.0)
    xbuf[HALO:HALO + cl, :] = xm_ref[...]
    xbuf[HALO + cl:, :] = jnp.where(chunk < n_seq - 1, xr_ref[...], 0.0)
    taps = cw_ref.shape[0]
    left = taps // 2
    xin = cb_ref[...] + xbuf[pl.ds(HALO - left, cl), :] * cw_ref[0:1, :]
    for t in range(1, taps):
        xin = xin + xbuf[pl.ds(HALO - left + t, cl), :] * cw_ref[t:t + 1, :]

    gates = _dot(xin.astype(BF16), wg_ref[...]) + bg_ref[...]
    r = 0.5 * jnp.tanh(0.5 * gates[:, :lw]) + 0.5
    gate_i = 0.5 * jnp.tanh(0.5 * gates[:, lw:]) + 0.5
    log_a = (-LRU_C) * r * jax.nn.softplus(-lam_ref[...])
    a = jnp.exp(log_a)
    b = jnp.sqrt(-jnp.tanh(log_a) * (a * a + 1.0)) * gate_i * xin

    seg = cl // 8
    pitch = a_s.shape[1] // 8
    order = range(seg - 1, -1, -1) if reverse else range(seg)
    for g in range(lw // LANES):
        cs = slice(g * LANES, (g + 1) * LANES)
        for i in range(8):
            a_s[g, i * pitch:i * pitch + seg, :] = a[i * seg:(i + 1) * seg, cs]
            b_s[g, i * pitch:i * pitch + seg, :] = b[i * seg:(i + 1) * seg, cs]
        step = lambda ref, j: ref[g, pl.ds(j, 8, stride=pitch), :]
        for n, j in enumerate(order):
            aj, bj = step(a_s, j), step(b_s, j)
            f, p = (bj, aj) if n == 0 else (aj * f + bj, aj * p)
        first = lax.broadcasted_iota(jnp.int32, f.shape, 0) == (7 if reverse else 0)
        f = f + jnp.where(first, p * carry[:, cs], 0.0)
        dist = 1
        while dist < 8:
            f = f + p * _shift_rows(f, dist, 0.0, reverse)
            p = p * _shift_rows(p, dist, 1.0, reverse)
            dist *= 2
        h = jnp.where(first, carry[:, cs], _shift_rows(f, 1, 0.0, reverse))
        carry[:, cs] = f[0:1, :] if reverse else f[7:8, :]
        for j in order:
            h = step(a_s, j) * h + step(b_s, j)
            h_s[g, pl.ds(j, 8, stride=pitch), :] = h
    h = jnp.concatenate(
        [jnp.concatenate([h_s[g, i * pitch:i * pitch + seg, :] for i in range(8)], axis=0) for g in range(lw // LANES)],
        axis=1)
    if reverse:
        o_ref[...] = ((hf_ref[...] + h) * gg_ref[...]).astype(o_ref.dtype)
    else:
        o_ref[...] = h


def lru_scan(lay, xa, conv_w, conv_b, w_gate, b_gate, lam, reverse, hf=None, gg=None):
    nt, lw = xa.shape
    cl = LRU_CHUNK
    assert lay.t % cl == 0 and lay.lc % cl == 0
    ncc, nlc = lay.lc // cl, lay.t // cl
    n_lat_blocks = lay.n_lat // cl

    def chunk_block(b, s):
        in_ctx = s < ncc
        pos = jnp.where(in_ctx, s, s - ncc)
        if reverse:
            pos = jnp.where(in_ctx, ncc - 1 - pos, nlc - 1 - pos)
        return jnp.where(in_ctx, n_lat_blocks + b * ncc + pos, b * nlc + pos)

    per = cl // HALO
    last_halo = nt // HALO - 1
    main = pl.BlockSpec((cl, lw), lambda b, s: (chunk_block(b, s), 0))
    in_specs = [
        main,
        pl.BlockSpec((HALO, lw), lambda b, s: (jnp.maximum(chunk_block(b, s) * per - 1, 0), 0)),
        pl.BlockSpec((HALO, lw), lambda b, s: (jnp.minimum((chunk_block(b, s) + 1) * per, last_halo), 0)),
        _const_spec(conv_w.shape),
        _const_spec((1, lw)),
        _const_spec(w_gate.shape),
        _const_spec((1, 2 * lw)),
        _const_spec((1, lw)),
    ]
    args = [xa, xa, xa, conv_w, conv_b.reshape(1, lw), w_gate, b_gate.reshape(1, 2 * lw), lam.reshape(1, lw)]
    if reverse:
        in_specs += [main, main]
        args += [hf, gg]
    return pl.pallas_call(
        functools.partial(_lru_kernel, reverse=reverse, n_ctx_chunks=ncc, n_lat_chunks=nlc),
        out_shape=jax.ShapeDtypeStruct((nt, lw), BF16 if reverse else F32),
        grid=(lay.b, ncc + nlc),
        in_specs=in_specs,
        out_specs=main,
        scratch_shapes=[pltpu.VMEM((cl + 2 * HALO, lw), F32), pltpu.VMEM((1, lw), F32)]
        + [pltpu.VMEM((lw // LANES, cl + 8 * SEG_PAD, LANES), F32)] * 3,
        compiler_params=pltpu.CompilerParams(dimension_semantics=("arbitrary", "arbitrary")),
        name="lru_bwd" if reverse else "lru_fwd",
    )(*args)


def _win_attn_kernel(q_ref, kl_ref, kx_ref, vl_ref, vx_ref, sink_ref, o_ref, *, n_q_lat, n_kv, hd):
    j = pl.program_id(1)
    tq = q_ref.shape[0]
    t = kl_ref.shape[0]
    lc = kx_ref.shape[0]
    vch = vl_ref.shape[2]
    sq = WIN_SUB
    band = sq + 2 * WINDOW
    cols = GQA_GROUP * sq
    low = lax.broadcasted_iota(jnp.int32, (sq, 2 * hd), 1) % hd < hd // 2

    def colsum(x):
        return jnp.sum(jnp.sum(x.reshape(-1, 8, cols), axis=0), axis=0, keepdims=True)

    def colmax(x):
        return jnp.max(jnp.max(x.reshape(-1, 8, cols), axis=0), axis=0, keepdims=True)

    def attend(kv, r0, k_parts, vt_parts, bias):
        qs = []
        for g in range(GQA_GROUP):
            h = kv * GQA_GROUP + g
            qp = q_ref[r0:r0 + sq, 2 * hd * (h // 2):2 * hd * (h // 2 + 1)]
            qs.append(jnp.where(low if h % 2 == 0 else jnp.logical_not(low), qp, jnp.zeros_like(qp)))
        qs = jnp.concatenate(qs, axis=0)
        sink = jnp.concatenate(
            [jnp.concatenate([sink_ref[kv * GQA_GROUP + g:kv * GQA_GROUP + g + 1, :]] * (sq // LANES), axis=1)
             for g in range(GQA_GROUP)], axis=1) * LOG2_E
        s = [_dot_t(kp, qs) for kp in k_parts]
        if bias is not None:
            s[0] = s[0] + bias
        m = sink
        for sp in s:
            m = jnp.maximum(m, colmax(sp))
        den = jnp.exp2(sink - m)
        p = []
        for sp in s:
            pp = jnp.exp2(sp - m)
            den = den + colsum(pp)
            p.append(pp.astype(BF16))
        o_t = _dot(jnp.concatenate(vt_parts, axis=1), jnp.concatenate(p, axis=0)) * (1.0 / den)
        for pair in range(GQA_GROUP // 2):
            both = jnp.concatenate([o_t[:, (2 * pair + i) * sq:(2 * pair + i + 1) * sq] for i in range(2)], axis=0)
            c0 = 2 * hd * (kv * GQA_GROUP // 2 + pair)
            o_ref[r0:r0 + sq, c0:c0 + 2 * hd] = both.T.astype(o_ref.dtype)

    def ctx_parts(kv):
        ks = kx_ref[:, 2 * hd * kv:2 * hd * (kv + 1)]
        vts = [vx_ref[c][hd * kv:hd * (kv + 1), :] for c in range(lc // vch)]
        return ks, vts

    @pl.when(j < n_q_lat)
    def _():
        for r0 in range(0, tq, sq):
            q0 = j * tq + r0
            start = pl.multiple_of(jnp.clip(q0 - WINDOW, 0, t - band), LANES)
            kpos = start + lax.broadcasted_iota(jnp.int32, (band, sq), 0)
            qpos = q0 + lax.broadcasted_iota(jnp.int32, (band, sq), 1)
            bias = jnp.where(jnp.abs(kpos - qpos) > WINDOW, -jnp.inf, 0.0).astype(F32)
            bias = jnp.concatenate([bias] * GQA_GROUP, axis=1)
            c0 = start // vch
            for kv in range(n_kv):
                kx, vxs = ctx_parts(kv)
                kb = kl_ref[pl.ds(start, band), 2 * hd * kv:2 * hd * (kv + 1)]
                vbs = [vl_ref[c0 + c][hd * kv:hd * (kv + 1), :] for c in range(band // vch)]
                attend(kv, r0, [kb, kx], vbs + vxs, bias)

    @pl.when(j >= n_q_lat)
    def _():
        for r0 in range(0, tq, sq):
            for kv in range(n_kv):
                kx, vxs = ctx_parts(kv)
                attend(kv, r0, [kx], vxs, None)


def win_attn(lay, q, kd, vt, sink_rows, n_kv, hd):
    nt, qw = q.shape
    kw = kd.shape[1]
    tq = Q_TILE_WIN
    vch = vt.shape[2]
    assert lay.t % tq == 0 and lay.lc % tq == 0 and lay.n_lat % lay.lc == 0 and tq % 128 == 0
    assert vch == LANES and WINDOW % vch == 0 and lay.lc % vch == 0 and tq % WIN_SUB == 0 and lay.t >= WIN_SUB + 2 * WINDOW
    nb, ncq = lay.t // tq, lay.lc // tq
    ctx_q0 = lay.n_lat // tq
    ctx_k0 = lay.n_lat // lay.lc

    def qblock(b, j):
        return jnp.where(j < nb, b * nb + j, ctx_q0 + b * ncq + (j - nb))

    qspec = pl.BlockSpec((tq, qw), lambda b, j: (qblock(b, j), 0))
    return pl.pallas_call(
        functools.partial(_win_attn_kernel, n_q_lat=nb, n_kv=n_kv, hd=hd),
        out_shape=jax.ShapeDtypeStruct((nt, qw), BF16),
        grid=(lay.b, nb + ncq),
        in_specs=[
            qspec,
            pl.BlockSpec((lay.t, kw), lambda b, j: (b, 0)),
            pl.BlockSpec((lay.lc, kw), lambda b, j: (ctx_k0 + b, 0)),
            pl.BlockSpec((lay.t // vch, vt.shape[1], vch), lambda b, j: (b, 0, 0)),
            pl.BlockSpec((lay.lc // vch, vt.shape[1], vch), lambda b, j: (ctx_k0 + b, 0, 0)),
            _const_spec(sink_rows.shape),
        ],
        out_specs=qspec,
        compiler_params=pltpu.CompilerParams(vmem_limit_bytes=V7X_VMEM_LIMIT),
        name="win_attn",
    )(q, kd, kd, vt, vt, sink_rows)


def _odd_in_kernel(x_ref, mod_ref, g_ref, w_ref, cos_ref, sin_ref, gq_ref, gk_ref, q_ref, k_ref, vt_ref, *, qw, kw, hd):
    u = _pre(x_ref[...], g_ref[...], mod_ref[3:4, :], mod_ref[4:5, :]).astype(BF16)
    cos, sin = cos_ref[...], sin_ref[...]

    def normed_rope(o, width, gains, out_ref, post):
        cg = cos * gains[0:1, :]
        sg = sin * gains[1:2, :]
        step = 2 * hd
        for c0 in range(0, width, step):
            x2 = _dot(u, w_ref[:, o + c0:o + c0 + step])
            xp2 = _dot(u, w_ref[:, o + width + c0:o + width + c0 + step])
            for h0 in range(0, step, hd):
                x, xp = x2[:, h0:h0 + hd], xp2[:, h0:h0 + hd]
                r = lax.rsqrt(jnp.mean(x * x, axis=-1, keepdims=True) + EPS) * post
                out_ref[:, c0 + h0:c0 + h0 + hd] = (r * (x * cg + xp * sg)).astype(out_ref.dtype)

    normed_rope(0, qw, gq_ref[...], q_ref, hd ** -0.5 * LOG2_E)
    normed_rope(2 * qw, kw, gk_ref[...], k_ref, 1.0)
    o = 2 * qw + 2 * kw
    v = _dot(u, w_ref[:, o:o + kw])
    ch = vt_ref.shape[2]
    for c in range(vt_ref.shape[0]):
        vt_ref[c] = v[c * ch:(c + 1) * ch, :].T.astype(BF16)


def odd_in(lay, x_all, mod_l, g_pre, w_cat, cos_t, sin_t, gq, gk, qw, kw, hd):
    d = lay.d
    tm = lay.tile(IN_TILES)
    rb = lay.rope_block(tm)
    nt = lay.nt
    row = lambda i: (i, 0)
    ch = KEY_CHUNK
    assert tm % ch == 0
    return pl.pallas_call(
        functools.partial(_odd_in_kernel, qw=qw, kw=kw, hd=hd),
        out_shape=(
            jax.ShapeDtypeStruct((nt, qw), BF16),
            jax.ShapeDtypeStruct((nt, kw), BF16),
            jax.ShapeDtypeStruct((nt // ch, kw, ch), BF16),
        ),
        grid=(nt // tm,),
        in_specs=[
            pl.BlockSpec((tm, d), row),
            _mod_spec(lay, tm),
            _const_spec((1, d)),
            _resident_spec(w_cat.shape),
            pl.BlockSpec((tm, hd), lambda i: (rb(i), 0)),
            pl.BlockSpec((tm, hd), lambda i: (rb(i), 0)),
            _const_spec(gq.shape),
            _const_spec(gk.shape),
        ],
        out_specs=(pl.BlockSpec((tm, qw), row), pl.BlockSpec((tm, kw), row),
                   pl.BlockSpec((tm // ch, kw, ch), lambda i: (i, 0, 0))),
        compiler_params=pltpu.CompilerParams(vmem_limit_bytes=V7X_VMEM_LIMIT),
        name="odd_in",
    )(x_all, mod_l, g_pre.reshape(1, d), w_cat, cos_t, sin_t, gq, gk)


def _dense_attn_kernel(q_ref, kl_ref, kx_ref, vl_ref, vx_ref, o_ref, s_ref, m_ref, l_ref, acc_ref, *, n_q_lat, nb):
    j = pl.program_id(2)
    tq = q_ref.shape[0]
    t, hd = kl_ref.shape
    lc = kx_ref.shape[0]
    ch = vl_ref.shape[2]
    cols = GQA_GROUP * tq
    qs = jnp.concatenate([q_ref[:, g * hd:(g + 1) * hd] for g in range(GQA_GROUP)], axis=0)

    def sum8(x):
        return jnp.sum(x.reshape(-1, 8, cols), axis=0)

    def colmax(x):
        return jnp.max(jnp.max(x.reshape(-1, 8, cols), axis=0), axis=0, keepdims=True)

    sx = _dot_t(kx_ref[...], qs)
    m0 = colmax(sx)
    p = jnp.exp2(sx - m0)
    m_ref[...] = m0
    l_ref[...] = sum8(p)
    acc_ref[...] = _dot(jnp.concatenate([vx_ref[c] for c in range(lc // ch)], axis=1), p.astype(BF16))

    @pl.when(j < n_q_lat)
    def _():
        n_blocks = t // nb

        def produce(i):
            s = _dot_t(kl_ref[i * nb:(i + 1) * nb, :], qs)
            s_ref[i % 2] = s
            return colmax(s)

        m = m_ref[...]
        block_max = produce(0)
        for i in range(n_blocks):
            m_new = jnp.maximum(m, block_max)
            if i + 1 < n_blocks:
                block_max = produce(i + 1)
            alpha = jnp.exp2(m - m_new)
            p = jnp.exp2(s_ref[i % 2] - m_new)
            vt = jnp.concatenate([vl_ref[i * (nb // ch) + c] for c in range(nb // ch)], axis=1)
            l_ref[...] = l_ref[...] * alpha + sum8(p)
            acc_ref[...] = acc_ref[...] * alpha + _dot(vt, p.astype(BF16))
            m = m_new

    o_t = acc_ref[...] * (1.0 / jnp.sum(l_ref[...], axis=0, keepdims=True))
    for g in range(GQA_GROUP):
        o_ref[:, g * hd:(g + 1) * hd] = o_t[:, g * tq:(g + 1) * tq].T.astype(o_ref.dtype)


def _dense_attn_unshifted_kernel(q_ref, kl_ref, kx_ref, vl_ref, vx_ref, o_ref, *, n_q_lat, nb):
    j = pl.program_id(2)
    tq = q_ref.shape[0]
    t, hd = kl_ref.shape
    lc = kx_ref.shape[0]
    ch = vl_ref.shape[2]
    cols = GQA_GROUP * tq
    qs = jnp.concatenate([q_ref[:, g * hd:(g + 1) * hd] for g in range(GQA_GROUP)], axis=0)

    def block(k, vts):
        p = jnp.exp2(_dot_t(k, qs))
        return jnp.sum(p.reshape(-1, 8, cols), axis=0), _dot(jnp.concatenate(vts, axis=1), p.astype(BF16))

    def finish(l8, acc):
        o_t = acc * (1.0 / jnp.sum(l8, axis=0, keepdims=True))
        for g in range(GQA_GROUP):
            o_ref[:, g * hd:(g + 1) * hd] = o_t[:, g * tq:(g + 1) * tq].T.astype(o_ref.dtype)

    @pl.when(j < n_q_lat)
    def _():
        l8, acc = block(kx_ref[...], [vx_ref[c] for c in range(lc // ch)])
        for i in range(t // nb):
            dl, da = block(kl_ref[i * nb:(i + 1) * nb, :], [vl_ref[i * (nb // ch) + c] for c in range(nb // ch)])
            l8, acc = l8 + dl, acc + da
        finish(l8, acc)

    @pl.when(j >= n_q_lat)
    def _():
        finish(*block(kx_ref[...], [vx_ref[c] for c in range(lc // ch)]))


def dense_attn(lay, q, k, vt, hd, with_ctx_queries, shifted):
    nt, qw = q.shape
    tq = 128 if shifted else 256
    ch = vt.shape[2]
    assert lay.t % tq == 0 and lay.lc % tq == 0 and lay.n_lat % lay.lc == 0 and lay.t % ch == 0 and lay.lc % ch == 0
    n_kv = k.shape[1] // hd
    gw = GQA_GROUP * hd
    nql, nqc = lay.t // tq, lay.lc // tq
    ctx_q0 = lay.n_lat // tq
    ctx_k0 = lay.n_lat // lay.lc
    n_rows = nt if with_ctx_queries else lay.n_lat

    def qblock(b, kv, j):
        return (jnp.where(j < nql, b * nql + j, ctx_q0 + b * nqc + (j - nql)), kv)

    qspec = pl.BlockSpec((tq, gw), qblock)
    cols = GQA_GROUP * tq
    nb = 1024 if lay.t % 1024 == 0 else ch
    assert nb % ch == 0 and lay.t % nb == 0
    if shifted:
        body = _dense_attn_kernel
        scratch = [pltpu.VMEM((2, nb, cols), F32), pltpu.VMEM((1, cols), F32), pltpu.VMEM((8, cols), F32),
                   pltpu.VMEM((hd, cols), F32)]
    else:
        body = _dense_attn_unshifted_kernel
        scratch = []
    return pl.pallas_call(
        functools.partial(body, n_q_lat=nql, nb=nb),
        out_shape=jax.ShapeDtypeStruct((n_rows, qw), BF16),
        grid=(lay.b, n_kv, nql + (nqc if with_ctx_queries else 0)),
        in_specs=[
            qspec,
            pl.BlockSpec((lay.t, hd), lambda b, kv, j: (b, kv)),
            pl.BlockSpec((lay.lc, hd), lambda b, kv, j: (ctx_k0 + b, kv)),
            pl.BlockSpec((lay.t // ch, hd, ch), lambda b, kv, j: (b, kv, 0)),
            pl.BlockSpec((lay.lc // ch, hd, ch), lambda b, kv, j: (ctx_k0 + b, kv, 0)),
        ],
        out_specs=qspec,
        scratch_shapes=scratch,
        compiler_params=pltpu.CompilerParams(vmem_limit_bytes=V7X_VMEM_LIMIT),
        name="dense_attn" if shifted else "dense_attn_unshifted",
    )(q, k, k, vt, vt)


def _split_pairs(w, hd, dup=False):
    d = w.shape[0]
    pairs = w.reshape(d, -1, hd // 2, 2)
    if dup:
        pairs = jnp.concatenate([pairs[:, :, None], pairs[:, :, None]], axis=2).reshape(d, -1, hd // 2, 2)
    per_group = LANES // hd
    groups = pairs.reshape(d, -1, per_group, hd // 2, 2)
    return jnp.moveaxis(groups, 4, 2).reshape(d, -1)


def _swap_halves(w):
    halves = w.reshape(w.shape[0], -1, 2, LANES // 2)
    return jnp.concatenate([halves[:, :, 1:], halves[:, :, :1]], axis=2).reshape(w.shape)


def _block_diag(w):
    h, n, _ = w.shape
    eye = jnp.eye(h, dtype=w.dtype)
    return (eye[:, None, :, None] * w[:, :, None, :]).reshape(h * n, h * n)


def _rope_tables(t, hd, ident_rows):
    n_freq = hd // 4
    freq = ROPE_THETA ** (-jnp.arange(n_freq, dtype=F32) / n_freq)
    pos = jnp.arange(t)
    row = (pos // GRID_W).astype(F32)
    col = (pos % GRID_W).astype(F32)
    ang = jnp.concatenate([row[:, None] * freq, col[:, None] * freq], axis=-1)
    reps = LANES // hd
    cos = jnp.tile(jnp.cos(ang), (1, reps))
    sin = jnp.tile(jnp.sin(ang), (1, reps))
    cos = jnp.concatenate([cos, jnp.ones((ident_rows, LANES // 2), F32)], axis=0)
    sin = jnp.concatenate([sin, jnp.zeros((ident_rows, LANES // 2), F32)], axis=0)
    return jnp.concatenate([cos, cos], axis=1), jnp.concatenate([-sin, sin], axis=1)


def kernel(x, c, ctx, c_ctx, w_ada, b_ada, norm_pre, norm_post, ffn_w_gate, ffn_w_up, ffn_w_down, even_w_in,
           even_conv_w, even_conv_b, lru_w_a, lru_b_a, lru_w_x, lru_b_x, lru_lambda, attn_sink, even_w_out, odd_w_in,
           odd_q_norm, odd_k_norm, odd_w_out):
    b, t, d = x.shape
    lc = ctx.shape[1]
    depth = w_ada.shape[0]
    lay = _Layout(b, t, lc, d)
    assert b + 1 <= MOD_ROWS and w_ada.shape[2] == N_MOD * d

    lw = even_conv_w.shape[2]
    win_heads = attn_sink.shape[1]
    win_hd = (d - lw) // win_heads
    win_kv = win_heads // GQA_GROUP
    glb_hd = odd_q_norm.shape[1]
    glb_heads = d // glb_hd
    glb_kv = glb_heads // GQA_GROUP
    qw_e, kw_e = win_heads * win_hd, win_kv * win_hd
    qw_o, kw_o = glb_heads * glb_hd, glb_kv * glb_hd

    x_all = (x.reshape(b * t, d), ctx.reshape(b * lc, d))
    c_rows = jnp.concatenate([c, c_ctx[None, :], jnp.zeros((MOD_ROWS - b - 1, d), F32)], axis=0)
    mod = ada_mod(c_rows, w_ada, b_ada).reshape(depth, MOD_ROWS, N_MOD, d)

    tm = lay.tile(IN_TILES)
    cos_e, sin_e = _rope_tables(t, win_hd, tm)
    cos_o, sin_o = _rope_tables(t, glb_hd, tm)

    wg16, wu16, wd16 = ffn_w_gate.astype(BF16), ffn_w_up.astype(BF16), ffn_w_down.astype(BF16)

    for l in range(depth):
        last = l == depth - 1
        i = l // 2
        mod_l = mod[l]
        x_all = ffn_sublayer(lay, x_all, lay.nt, mod_l, 0, norm_pre[l, 0], norm_post[l, 0], wg16, wu16, wd16, (l, 0))
        rows_out = lay.n_lat if last else lay.nt
        if l % 2 == 0:
            w = even_w_in[i]
            o1, o2, o3, o4 = 2 * lw, 2 * lw + qw_e, 2 * lw + qw_e + kw_e, 2 * lw + qw_e + 2 * kw_e
            wq = w[:, o1:o2]
            wk = w[:, o2:o3]
            w_cat = jnp.concatenate([
                w[:, :o1], _split_pairs(wq, win_hd), _split_pairs(wk, win_hd, dup=True), w[:, o3:o4]], axis=1).astype(BF16)
            xa, gg, q, kd, vt = even_in(lay, x_all, mod_l, norm_pre[l, 1], w_cat, cos_e, sin_e, lw, qw_e, 2 * kw_e, kw_e,
                                        win_hd ** -0.5 * LOG2_E)
            spl_args = []
            for direction in range(2):
                w_gate = jnp.concatenate([_block_diag(lru_w_a[i, direction]), _block_diag(lru_w_x[i, direction])], axis=1).astype(BF16)
                b_gate = jnp.concatenate([lru_b_a[i, direction], lru_b_x[i, direction]])
                spl_args.append((w_gate, b_gate, lru_lambda[i, direction]))
            hf = lru_scan(lay, xa, even_conv_w[i], even_conv_b[i], *spl_args[0], reverse=False)
            ya = lru_scan(lay, xa, even_conv_w[i], even_conv_b[i], *spl_args[1], reverse=True, hf=hf, gg=gg)
            sink_rows = jnp.broadcast_to(attn_sink[i][:, None], (win_heads, 128)).astype(F32)
            ob = win_attn(lay, q, kd, vt, sink_rows, win_kv, win_hd)
            mix = (norm_post[l, 1], even_w_out[i].astype(BF16), [ya, ob])
        else:
            w = odd_w_in[i]
            wq, wk, wv = w[:, :qw_o], w[:, qw_o:qw_o + kw_o], w[:, qw_o + kw_o:]
            wq, wk = _split_pairs(wq, glb_hd), _split_pairs(wk, glb_hd)
            w_cat = jnp.concatenate([wq, _swap_halves(wq), wk, _swap_halves(wk), wv], axis=1).astype(BF16)
            gq, gk = (_split_pairs(g[None, :], glb_hd) for g in (odd_q_norm[i], odd_k_norm[i]))
            gq, gk = (jnp.concatenate([g, jnp.roll(g, LANES // 2, axis=1)], axis=0) for g in (gq, gk))
            q, k, vt = odd_in(lay, x_all, mod_l, norm_pre[l, 1], w_cat, cos_o, sin_o, gq, gk, qw_o, kw_o, glb_hd)
            bound = LOG2_E * glb_hd ** 0.5 * jnp.max(jnp.abs(odd_q_norm[i])) * jnp.max(jnp.abs(odd_k_norm[i]))
            o = lax.cond(
                bound <= MAX_UNSHIFTED_SCORE,
                functools.partial(dense_attn, lay, hd=glb_hd, with_ctx_queries=not last, shifted=False),
                functools.partial(dense_attn, lay, hd=glb_hd, with_ctx_queries=not last, shifted=True),
                q, k, vt)
            mix = (norm_post[l, 1], odd_w_out[i].astype(BF16), [o])
        x_all = ffn_sublayer(lay, x_all, rows_out, mod_l, 6, norm_pre[l, 2], norm_post[l, 2], wg16, wu16, wd16, (l, 1),
                             mix=mix)
    return x_all[:b * t].reshape(b, t, d)
```

```python
import functools

import jax
import jax.numpy as jnp
from jax import lax
from jax.experimental import pallas as pl
from jax.experimental.pallas import tpu as pltpu

F32 = jnp.float32
BF16 = jnp.bfloat16

EPS = 1e-6
FFN_RES = 0.5
LRU_C = 8.0
WINDOW = 128
GRID_W = 64
ROPE_THETA = 10000.0
GQA_GROUP = 4
N_MOD = 9
MOD_ROWS = 16

V7X_VMEM_LIMIT = 56 * 1024 * 1024
Q_TILE_WIN = 256
IN_TILES = (1024, 512, 256, 128)
WIN_SUB = 256
VT_CHUNK = 128
LRU_CHUNK = 256
LANES = 128
SEG_PAD = 4
HALO = 8
FFN_SUBTILES = 4
KEY_CHUNK = 256
LOG2_E = 1.4426950408889634
MAX_UNSHIFTED_SCORE = 60.0


def _const_spec(shape):
    zeros = (0,) * len(shape)
    return pl.BlockSpec(shape, lambda *_: zeros)


def _resident_spec(shape):
    zeros = (0,) * len(shape)
    return pl.BlockSpec(shape, lambda *_: zeros, pipeline_mode=pl.Buffered(1))


def _dot(a, b):
    return jnp.dot(a, b, preferred_element_type=F32)


def _dot_t(a, b):
    return lax.dot_general(a, b, (((1,), (1,)), ((), ())), preferred_element_type=F32)


def _rope(x, cos, sin_signed):
    return x * cos + pltpu.roll(x, LANES // 2, 1) * sin_signed


def _rms(x, g):
    return x * lax.rsqrt(jnp.mean(x * x, axis=-1, keepdims=True) + EPS) * g


def _pre(x, g, shift, scale):
    return _rms(x, g) * (1.0 + scale) + shift


def _ada_kernel(c_ref, w_ref, b_ref, o_ref):
    c = c_ref[...]
    s = c * jax.nn.sigmoid(c)
    o_ref[...] = _dot(s.astype(BF16), w_ref[...].astype(BF16)) + b_ref[...]


def ada_mod(c_rows, w_ada, b_ada):
    depth, d, n = w_ada.shape
    tn = 1024
    return pl.pallas_call(
        _ada_kernel,
        out_shape=jax.ShapeDtypeStruct((depth, MOD_ROWS, n), F32),
        grid=(depth, n // tn),
        in_specs=[
            pl.BlockSpec((MOD_ROWS, d), lambda l, j: (0, 0)),
            pl.BlockSpec((None, d, tn), lambda l, j: (l, 0, j)),
            pl.BlockSpec((None, 1, tn), lambda l, j: (l, 0, j)),
        ],
        out_specs=pl.BlockSpec((None, MOD_ROWS, tn), lambda l, j: (l, 0, j)),
        name="ada_mod",
    )(c_rows, w_ada, b_ada.reshape(depth, 1, n))


class _Layout:
    def __init__(self, b, t, lc, d):
        self.b, self.t, self.lc, self.d = b, t, lc, d
        self.n_lat = b * t
        self.nt = b * (t + lc)

    def tile(self, candidates):
        for tm in candidates:
            if self.t % tm == 0 and (self.b * self.lc) % tm == 0:
                return tm
        raise ValueError("no token tile divides both the latent and the context stream")

    def mod_row(self, tm):
        n_lat_tiles, per_b, b = self.n_lat // tm, self.t // tm, self.b
        return lambda i: jnp.where(i < n_lat_tiles, i // per_b, b)

    def rope_block(self, tm):
        n_lat_tiles, per_b = self.n_lat // tm, self.t // tm
        return lambda i: jnp.where(i < n_lat_tiles, i % per_b, per_b)


def _mod_spec(lay, tm):
    row = lay.mod_row(tm)
    return pl.BlockSpec((None, N_MOD, lay.d), lambda i: (row(i), 0, 0))


def _ffn_kernel(x_ref, *rest, m0, ff_chunk, n_lat_tiles, subtiles):
    ctx_ref = None
    if n_lat_tiles is not None:
        ctx_ref, rest = rest[0], rest[1:]
    mod_ref, gpre_ref, gpost_ref, wg_ref, wu_ref, wd_ref = rest[:6]
    rest = rest[6:]
    o_ref, h_ref = rest[-2:]
    tm = x_ref.shape[0]
    d_ff = wg_ref.shape[1]
    sub_rows = [slice(r0, r0 + tm // subtiles) for r0 in range(0, tm, tm // subtiles)]
    chunks = list(range(0, d_ff, ff_chunk))
    xs, us, ys = {}, {}, {}

    def before(k):
        rows = sub_rows[k]
        x = x_ref[rows, :]
        if ctx_ref is not None:
            x = jnp.where(pl.program_id(0) < n_lat_tiles, x, ctx_ref[rows, :])
        if len(rest) > 2:
            gmix_ref, wmix_ref = rest[:2]
            parts = [r[rows, :] for r in rest[2:-2]]
            y = _dot(jnp.concatenate(parts, axis=1) if len(parts) > 1 else parts[0], wmix_ref[...])
            x = x + mod_ref[5:6, :] * _rms(y, gmix_ref[...])
        xs[k] = x
        us[k] = _pre(x, gpre_ref[...], mod_ref[m0:m0 + 1, :], mod_ref[m0 + 1:m0 + 2, :]).astype(BF16)

    def expand(k, some_chunks):
        for c0 in some_chunks:
            g = _dot(us[k], wg_ref[:, c0:c0 + ff_chunk])
            up = _dot(us[k], wu_ref[:, c0:c0 + ff_chunk])
            h_ref[sub_rows[k], c0:c0 + ff_chunk] = (g * jax.nn.sigmoid(g) * up).astype(BF16)

    def after(k):
        o_ref[sub_rows[k], :] = xs[k] + FFN_RES * mod_ref[m0 + 2:m0 + 3, :] * _rms(ys[k], gpost_ref[...])

    before(0)
    for k in range(subtiles):
        expand(k, chunks[:len(chunks) // 2])
        if k > 0:
            after(k - 1)
        if k + 1 < subtiles:
            before(k + 1)
        expand(k, chunks[len(chunks) // 2:])
        ys[k] = _dot(h_ref[sub_rows[k], :], wd_ref[...])
    after(subtiles - 1)


def ffn_sublayer(lay, x_all, n_rows, mod_l, m0, g_pre, g_post, wg, wu, wd, which, mix=None):
    d = lay.d
    d_ff = wg.shape[-1]
    picked = lambda r, c: pl.BlockSpec((None, None, r, c), lambda i: which + (0, 0), pipeline_mode=pl.Buffered(1))
    tm = lay.tile((1024, 512, 256, 128))
    row = lambda i: (i, 0)
    n_lat_tiles = None
    if isinstance(x_all, tuple):
        n_lat_tiles = lay.n_lat // tm
        in_specs = [pl.BlockSpec((tm, d), lambda i: (jnp.minimum(i, n_lat_tiles - 1), 0)),
                    pl.BlockSpec((tm, d), lambda i: (jnp.maximum(i - n_lat_tiles, 0), 0))]
        args = list(x_all)
    else:
        in_specs = [pl.BlockSpec((tm, d), row)]
        args = [x_all]
    in_specs += [
        _mod_spec(lay, tm),
        _const_spec((1, d)),
        _const_spec((1, d)),
        picked(d, d_ff),
        picked(d, d_ff),
        picked(d_ff, d),
    ]
    args += [mod_l, g_pre.reshape(1, d), g_post.reshape(1, d), wg, wu, wd]
    if mix is not None:
        g_mix, w_mix, parts = mix
        in_specs += [_const_spec((1, d)), _resident_spec(w_mix.shape)] + [pl.BlockSpec((tm, p.shape[1]), row) for p in parts]
        args += [g_mix.reshape(1, d), w_mix] + list(parts)
    return pl.pallas_call(
        functools.partial(_ffn_kernel, m0=m0, ff_chunk=256, n_lat_tiles=n_lat_tiles,
                          subtiles=FFN_SUBTILES if mix is None and tm % (8 * FFN_SUBTILES) == 0 else 1),
        out_shape=jax.ShapeDtypeStruct((n_rows, d), F32),
        grid=(n_rows // tm,),
        in_specs=in_specs,
        out_specs=pl.BlockSpec((tm, d), row),
        scratch_shapes=[pltpu.VMEM((tm, d_ff), BF16)],
        compiler_params=pltpu.CompilerParams(vmem_limit_bytes=V7X_VMEM_LIMIT),
        name="ffn_sublayer" if mix is None else "mix_ffn_sublayer",
    )(*args)


def _even_in_kernel(x_ref, mod_ref, g_ref, w_ref, cos_ref, sin_ref, xa_ref, gg_ref, q_ref, k_ref, vt_ref, *, lw, qw, kw, vw,
                    q_scale):
    u = _pre(x_ref[...], g_ref[...], mod_ref[3:4, :], mod_ref[4:5, :]).astype(BF16)
    cos, sin = cos_ref[...], sin_ref[...]
    o = 0
    xa_ref[...] = _dot(u, w_ref[:, o:o + lw])
    o += lw
    gg_ref[...] = jax.nn.gelu(_dot(u, w_ref[:, o:o + lw]))
    o += lw
    q = _dot(u, w_ref[:, o:o + qw])
    o += qw
    for c0 in range(0, qw, LANES):
        q_ref[:, c0:c0 + LANES] = (_rope(q[:, c0:c0 + LANES], cos, sin) * q_scale).astype(BF16)
    k = _dot(u, w_ref[:, o:o + kw])
    o += kw
    for c0 in range(0, kw, LANES):
        k_ref[:, c0:c0 + LANES] = _rope(k[:, c0:c0 + LANES], cos, sin).astype(BF16)
    v = _dot(u, w_ref[:, o:o + vw])
    ch = vt_ref.shape[2]
    for c in range(vt_ref.shape[0]):
        vt_ref[c] = v[c * ch:(c + 1) * ch, :].T.astype(BF16)


def even_in(lay, x_all, mod_l, g_pre, w_cat, cos_t, sin_t, lw, qw, kw, vw, q_scale):
    d = lay.d
    tm = lay.tile(IN_TILES)
    rb = lay.rope_block(tm)
    nt = lay.nt
    row = lambda i: (i, 0)
    ch = VT_CHUNK
    assert tm % ch == 0
    return pl.pallas_call(
        functools.partial(_even_in_kernel, lw=lw, qw=qw, kw=kw, vw=vw, q_scale=q_scale),
        out_shape=(
            jax.ShapeDtypeStruct((nt, lw), F32),
            jax.ShapeDtypeStruct((nt, lw), F32),
            jax.ShapeDtypeStruct((nt, qw), BF16),
            jax.ShapeDtypeStruct((nt, kw), BF16),
            jax.ShapeDtypeStruct((nt // ch, vw, ch), BF16),
        ),
        grid=(nt // tm,),
        in_specs=[
            pl.BlockSpec((tm, d), row),
            _mod_spec(lay, tm),
            _const_spec((1, d)),
            _resident_spec(w_cat.shape),
            pl.BlockSpec((tm, cos_t.shape[1]), lambda i: (rb(i), 0)),
            pl.BlockSpec((tm, cos_t.shape[1]), lambda i: (rb(i), 0)),
        ],
        out_specs=(
            pl.BlockSpec((tm, lw), row),
            pl.BlockSpec((tm, lw), row),
            pl.BlockSpec((tm, qw), row),
            pl.BlockSpec((tm, kw), row),
            pl.BlockSpec((tm // ch, vw, ch), lambda i: (i, 0, 0)),
        ),
        compiler_params=pltpu.CompilerParams(vmem_limit_bytes=V7X_VMEM_LIMIT),
        name="even_in",
    )(x_all, mod_l, g_pre.reshape(1, d), w_cat, cos_t, sin_t)


def _shift_rows(x, dist, fill, reverse):
    n = x.shape[0]
    rows = lax.broadcasted_iota(jnp.int32, x.shape, 0)
    if reverse:
        return jnp.where(rows < n - dist, pltpu.roll(x, n - dist, 0), fill)
    return jnp.where(rows >= dist, pltpu.roll(x, dist, 0), fill)


def _lru_kernel(xm_ref, xl_ref, xr_ref, cw_ref, cb_ref, wg_ref, bg_ref, lam_ref, *rest,
                reverse, n_ctx_chunks, n_lat_chunks):
    if reverse:
        hf_ref, gg_ref, o_ref, xbuf, carry, a_s, b_s, h_s = rest
    else:
        o_ref, xbuf, carry, a_s, b_s, h_s = rest
    s = pl.program_id(1)
    cl, lw = xm_ref.shape
    in_ctx = s < n_ctx_chunks
    pos = jnp.where(in_ctx, s, s - n_ctx_chunks)
    n_seq = jnp.where(in_ctx, n_ctx_chunks, n_lat_chunks)
    chunk = (n_seq - 1 - pos) if reverse else pos

    @pl.when(s == 0)
    def _():
        carry[...] = jnp.zeros_like(carry)

    xbuf[0:HALO, :] = jnp.where(chunk > 0, xl_ref[...], 0.0)
    xbuf[HALO:HALO + cl, :] = xm_ref[...]
    xbuf[HALO + cl:, :] = jnp.where(chunk < n_seq - 1, xr_ref[...], 0.0)
    taps = cw_ref.shape[0]
    left = taps // 2
    xin = cb_ref[...] + xbuf[pl.ds(HALO - left, cl), :] * cw_ref[0:1, :]
    for t in range(1, taps):
        xin = xin + xbuf[pl.ds(HALO - left + t, cl), :] * cw_ref[t:t + 1, :]

    gates = _dot(xin.astype(BF16), wg_ref[...]) + bg_ref[...]
    r = 0.5 * jnp.tanh(0.5 * gates[:, :lw]) + 0.5
    gate_i = 0.5 * jnp.tanh(0.5 * gates[:, lw:]) + 0.5
    log_a = (-LRU_C) * r * jax.nn.softplus(-lam_ref[...])
    a = jnp.exp(log_a)
    b = jnp.sqrt(-jnp.tanh(log_a) * (a * a + 1.0)) * gate_i * xin

    seg = cl // 8
    pitch = a_s.shape[1] // 8
    order = range(seg - 1, -1, -1) if reverse else range(seg)
    for g in range(lw // LANES):
        cs = slice(g * LANES, (g + 1) * LANES)
        for i in range(8):
            a_s[g, i * pitch:i * pitch + seg, :] = a[i * seg:(i + 1) * seg, cs]
            b_s[g, i * pitch:i * pitch + seg, :] = b[i * seg:(i + 1) * seg, cs]
        step = lambda ref, j: ref[g, pl.ds(j, 8, stride=pitch), :]
        for n, j in enumerate(order):
            aj, bj = step(a_s, j), step(b_s, j)
            f, p = (bj, aj) if n == 0 else (aj * f + bj, aj * p)
        first = lax.broadcasted_iota(jnp.int32, f.shape, 0) == (7 if reverse else 0)
        f = f + jnp.where(first, p * carry[:, cs], 0.0)
        dist = 1
        while dist < 8:
            f = f + p * _shift_rows(f, dist, 0.0, reverse)
            p = p * _shift_rows(p, dist, 1.0, reverse)
            dist *= 2
        h = jnp.where(first, carry[:, cs], _shift_rows(f, 1, 0.0, reverse))
        carry[:, cs] = f[0:1, :] if reverse else f[7:8, :]
        for j in order:
            h = step(a_s, j) * h + step(b_s, j)
            h_s[g, pl.ds(j, 8, stride=pitch), :] = h
    h = jnp.concatenate(
        [jnp.concatenate([h_s[g, i * pitch:i * pitch + seg, :] for i in range(8)], axis=0) for g in range(lw // LANES)],
        axis=1)
    if reverse:
        o_ref[...] = ((hf_ref[...] + h) * gg_ref[...]).astype(o_ref.dtype)
    else:
        o_ref[...] = h


def lru_scan(lay, xa, conv_w, conv_b, w_gate, b_gate, lam, reverse, hf=None, gg=None):
    nt, lw = xa.shape
    cl = LRU_CHUNK
    assert lay.t % cl == 0 and lay.lc % cl == 0
    ncc, nlc = lay.lc // cl, lay.t // cl
    n_lat_blocks = lay.n_lat // cl

    def chunk_block(b, s):
        in_ctx = s < ncc
        pos = jnp.where(in_ctx, s, s - ncc)
        if reverse:
            pos = jnp.where(in_ctx, ncc - 1 - pos, nlc - 1 - pos)
        return jnp.where(in_ctx, n_lat_blocks + b * ncc + pos, b * nlc + pos)

    per = cl // HALO
    last_halo = nt // HALO - 1
    main = pl.BlockSpec((cl, lw), lambda b, s: (chunk_block(b, s), 0))
    in_specs = [
        main,
        pl.BlockSpec((HALO, lw), lambda b, s: (jnp.maximum(chunk_block(b, s) * per - 1, 0), 0)),
        pl.BlockSpec((HALO, lw), lambda b, s: (jnp.minimum((chunk_block(b, s) + 1) * per, last_halo), 0)),
        _const_spec(conv_w.shape),
        _const_spec((1, lw)),
        _const_spec(w_gate.shape),
        _const_spec((1, 2 * lw)),
        _const_spec((1, lw)),
    ]
    args = [xa, xa, xa, conv_w, conv_b.reshape(1, lw), w_gate, b_gate.reshape(1, 2 * lw), lam.reshape(1, lw)]
    if reverse:
        in_specs += [main, main]
        args += [hf, gg]
    return pl.pallas_call(
        functools.partial(_lru_kernel, reverse=reverse, n_ctx_chunks=ncc, n_lat_chunks=nlc),
        out_shape=jax.ShapeDtypeStruct((nt, lw), BF16 if reverse else F32),
        grid=(lay.b, ncc + nlc),
        in_specs=in_specs,
        out_specs=main,
        scratch_shapes=[pltpu.VMEM((cl + 2 * HALO, lw), F32), pltpu.VMEM((1, lw), F32)]
        + [pltpu.VMEM((lw // LANES, cl + 8 * SEG_PAD, LANES), F32)] * 3,
        compiler_params=pltpu.CompilerParams(dimension_semantics=("arbitrary", "arbitrary")),
        name="lru_bwd" if reverse else "lru_fwd",
    )(*args)


def _win_attn_kernel(q_ref, kl_ref, kx_ref, vl_ref, vx_ref, sink_ref, o_ref, *, n_q_lat, n_kv, hd):
    j = pl.program_id(1)
    tq = q_ref.shape[0]
    t = kl_ref.shape[0]
    lc = kx_ref.shape[0]
    vch = vl_ref.shape[2]
    sq = WIN_SUB
    band = sq + 2 * WINDOW
    cols = GQA_GROUP * sq
    low = lax.broadcasted_iota(jnp.int32, (sq, 2 * hd), 1) % hd < hd // 2

    def colsum(x):
        return jnp.sum(jnp.sum(x.reshape(-1, 8, cols), axis=0), axis=0, keepdims=True)

    def colmax(x):
        return jnp.max(jnp.max(x.reshape(-1, 8, cols), axis=0), axis=0, keepdims=True)

    def attend(kv, r0, k_parts, vt_parts, bias):
        qs = []
        for g in range(GQA_GROUP):
            h = kv * GQA_GROUP + g
            qp = q_ref[r0:r0 + sq, 2 * hd * (h // 2):2 * hd * (h // 2 + 1)]
            qs.append(jnp.where(low if h % 2 == 0 else jnp.logical_not(low), qp, jnp.zeros_like(qp)))
        qs = jnp.concatenate(qs, axis=0)
        sink = jnp.concatenate(
            [jnp.concatenate([sink_ref[kv * GQA_GROUP + g:kv * GQA_GROUP + g + 1, :]] * (sq // LANES), axis=1)
             for g in range(GQA_GROUP)], axis=1) * LOG2_E
        s = [_dot_t(kp, qs) for kp in k_parts]
        if bias is not None:
            s[0] = s[0] + bias
        m = sink
        for sp in s:
            m = jnp.maximum(m, colmax(sp))
        den = jnp.exp2(sink - m)
        p = []
        for sp in s:
            pp = jnp.exp2(sp - m)
            den = den + colsum(pp)
            p.append(pp.astype(BF16))
        o_t = _dot(jnp.concatenate(vt_parts, axis=1), jnp.concatenate(p, axis=0)) * (1.0 / den)
        for pair in range(GQA_GROUP // 2):
            both = jnp.concatenate([o_t[:, (2 * pair + i) * sq:(2 * pair + i + 1) * sq] for i in range(2)], axis=0)
            c0 = 2 * hd * (kv * GQA_GROUP // 2 + pair)
            o_ref[r0:r0 + sq, c0:c0 + 2 * hd] = both.T.astype(o_ref.dtype)

    def ctx_parts(kv):
        ks = kx_ref[:, 2 * hd * kv:2 * hd * (kv + 1)]
        vts = [vx_ref[c][hd * kv:hd * (kv + 1), :] for c in range(lc // vch)]
        return ks, vts

    @pl.when(j < n_q_lat)
    def _():
        for r0 in range(0, tq, sq):
            q0 = j * tq + r0
            start = pl.multiple_of(jnp.clip(q0 - WINDOW, 0, t - band), LANES)
            kpos = start + lax.broadcasted_iota(jnp.int32, (band, sq), 0)
            qpos = q0 + lax.broadcasted_iota(jnp.int32, (band, sq), 1)
            bias = jnp.where(jnp.abs(kpos - qpos) > WINDOW, -jnp.inf, 0.0).astype(F32)
            bias = jnp.concatenate([bias] * GQA_GROUP, axis=1)
            c0 = start // vch
            for kv in range(n_kv):
                kx, vxs = ctx_parts(kv)
                kb = kl_ref[pl.ds(start, band), 2 * hd * kv:2 * hd * (kv + 1)]
                vbs = [vl_ref[c0 + c][hd * kv:hd * (kv + 1), :] for c in range(band // vch)]
                attend(kv, r0, [kb, kx], vbs + vxs, bias)

    @pl.when(j >= n_q_lat)
    def _():
        for r0 in range(0, tq, sq):
            for kv in range(n_kv):
                kx, vxs = ctx_parts(kv)
                attend(kv, r0, [kx], vxs, None)


def win_attn(lay, q, kd, vt, sink_rows, n_kv, hd):
    nt, qw = q.shape
    kw = kd.shape[1]
    tq = Q_TILE_WIN
    vch = vt.shape[2]
    assert lay.t % tq == 0 and lay.lc % tq == 0 and lay.n_lat % lay.lc == 0 and tq % 128 == 0
    assert vch == LANES and WINDOW % vch == 0 and lay.lc % vch == 0 and tq % WIN_SUB == 0 and lay.t >= WIN_SUB + 2 * WINDOW
    nb, ncq = lay.t // tq, lay.lc // tq
    ctx_q0 = lay.n_lat // tq
    ctx_k0 = lay.n_lat // lay.lc

    def qblock(b, j):
        return jnp.where(j < nb, b * nb + j, ctx_q0 + b * ncq + (j - nb))

    qspec = pl.BlockSpec((tq, qw), lambda b, j: (qblock(b, j), 0))
    return pl.pallas_call(
        functools.partial(_win_attn_kernel, n_q_lat=nb, n_kv=n_kv, hd=hd),
        out_shape=jax.ShapeDtypeStruct((nt, qw), BF16),
        grid=(lay.b, nb + ncq),
        in_specs=[
            qspec,
            pl.BlockSpec((lay.t, kw), lambda b, j: (b, 0)),
            pl.BlockSpec((lay.lc, kw), lambda b, j: (ctx_k0 + b, 0)),
            pl.BlockSpec((lay.t // vch, vt.shape[1], vch), lambda b, j: (b, 0, 0)),
            pl.BlockSpec((lay.lc // vch, vt.shape[1], vch), lambda b, j: (ctx_k0 + b, 0, 0)),
            _const_spec(sink_rows.shape),
        ],
        out_specs=qspec,
        compiler_params=pltpu.CompilerParams(vmem_limit_bytes=V7X_VMEM_LIMIT),
        name="win_attn",
    )(q, kd, kd, vt, vt, sink_rows)


def _odd_in_kernel(x_ref, mod_ref, g_ref, w_ref, cos_ref, sin_ref, gq_ref, gk_ref, q_ref, k_ref, vt_ref, *, qw, kw, hd):
    u = _pre(x_ref[...], g_ref[...], mod_ref[3:4, :], mod_ref[4:5, :]).astype(BF16)
    cos, sin = cos_ref[...], sin_ref[...]

    def normed_rope(o, width, gains, out_ref, post):
        cg = cos * gains[0:1, :]
        sg = sin * gains[1:2, :]
        step = 2 * hd
        for c0 in range(0, width, step):
            x2 = _dot(u, w_ref[:, o + c0:o + c0 + step])
            xp2 = _dot(u, w_ref[:, o + width + c0:o + width + c0 + step])
            for h0 in range(0, step, hd):
                x, xp = x2[:, h0:h0 + hd], xp2[:, h0:h0 + hd]
                r = lax.rsqrt(jnp.mean(x * x, axis=-1, keepdims=True) + EPS) * post
                out_ref[:, c0 + h0:c0 + h0 + hd] = (r * (x * cg + xp * sg)).astype(out_ref.dtype)

    normed_rope(0, qw, gq_ref[...], q_ref, hd ** -0.5 * LOG2_E)
    normed_rope(2 * qw, kw, gk_ref[...], k_ref, 1.0)
    o = 2 * qw + 2 * kw
    v = _dot(u, w_ref[:, o:o + kw])
    ch = vt_ref.shape[2]
    for c in range(vt_ref.shape[0]):
        vt_ref[c] = v[c * ch:(c + 1) * ch, :].T.astype(BF16)


def odd_in(lay, x_all, mod_l, g_pre, w_cat, cos_t, sin_t, gq, gk, qw, kw, hd):
    d = lay.d
    tm = lay.tile(IN_TILES)
    rb = lay.rope_block(tm)
    nt = lay.nt
    row = lambda i: (i, 0)
    ch = KEY_CHUNK
    assert tm % ch == 0
    return pl.pallas_call(
        functools.partial(_odd_in_kernel, qw=qw, kw=kw, hd=hd),
        out_shape=(
            jax.ShapeDtypeStruct((nt, qw), BF16),
            jax.ShapeDtypeStruct((nt, kw), BF16),
            jax.ShapeDtypeStruct((nt // ch, kw, ch), BF16),
        ),
        grid=(nt // tm,),
        in_specs=[
            pl.BlockSpec((tm, d), row),
            _mod_spec(lay, tm),
            _const_spec((1, d)),
            _resident_spec(w_cat.shape),
            pl.BlockSpec((tm, hd), lambda i: (rb(i), 0)),
            pl.BlockSpec((tm, hd), lambda i: (rb(i), 0)),
            _const_spec(gq.shape),
            _const_spec(gk.shape),
        ],
        out_specs=(pl.BlockSpec((tm, qw), row), pl.BlockSpec((tm, kw), row),
                   pl.BlockSpec((tm // ch, kw, ch), lambda i: (i, 0, 0))),
        compiler_params=pltpu.CompilerParams(vmem_limit_bytes=V7X_VMEM_LIMIT),
        name="odd_in",
    )(x_all, mod_l, g_pre.reshape(1, d), w_cat, cos_t, sin_t, gq, gk)


def _dense_attn_kernel(q_ref, kl_ref, kx_ref, vl_ref, vx_ref, o_ref, s_ref, m_ref, l_ref, acc_ref, *, n_q_lat, nb):
    j = pl.program_id(2)
    tq = q_ref.shape[0]
    t, hd = kl_ref.shape
    lc = kx_ref.shape[0]
    ch = vl_ref.shape[2]
    cols = GQA_GROUP * tq
    qs = jnp.concatenate([q_ref[:, g * hd:(g + 1) * hd] for g in range(GQA_GROUP)], axis=0)

    def sum8(x):
        return jnp.sum(x.reshape(-1, 8, cols), axis=0)

    def colmax(x):
        return jnp.max(jnp.max(x.reshape(-1, 8, cols), axis=0), axis=0, keepdims=True)

    sx = _dot_t(kx_ref[...], qs)
    m0 = colmax(sx)
    p = jnp.exp2(sx - m0)
    m_ref[...] = m0
    l_ref[...] = sum8(p)
    acc_ref[...] = _dot(jnp.concatenate([vx_ref[c] for c in range(lc // ch)], axis=1), p.astype(BF16))

    @pl.when(j < n_q_lat)
    def _():
        n_blocks = t // nb

        def produce(i):
            s = _dot_t(kl_ref[i * nb:(i + 1) * nb, :], qs)
            s_ref[i % 2] = s
            return colmax(s)

        m = m_ref[...]
        block_max = produce(0)
        for i in range(n_blocks):
            m_new = jnp.maximum(m, block_max)
            if i + 1 < n_blocks:
                block_max = produce(i + 1)
            alpha = jnp.exp2(m - m_new)
            p = jnp.exp2(s_ref[i % 2] - m_new)
            vt = jnp.concatenate([vl_ref[i * (nb // ch) + c] for c in range(nb // ch)], axis=1)
            l_ref[...] = l_ref[...] * alpha + sum8(p)
            acc_ref[...] = acc_ref[...] * alpha + _dot(vt, p.astype(BF16))
            m = m_new

    o_t = acc_ref[...] * (1.0 / jnp.sum(l_ref[...], axis=0, keepdims=True))
    for g in range(GQA_GROUP):
        o_ref[:, g * hd:(g + 1) * hd] = o_t[:, g * tq:(g + 1) * tq].T.astype(o_ref.dtype)


def _dense_attn_unshifted_kernel(q_ref, kl_ref, kx_ref, vl_ref, vx_ref, o_ref, *, n_q_lat, nb):
    j = pl.program_id(2)
    tq = q_ref.shape[0]
    t, hd = kl_ref.shape
    lc = kx_ref.shape[0]
    ch = vl_ref.shape[2]
    cols = GQA_GROUP * tq
    qs = jnp.concatenate([q_ref[:, g * hd:(g + 1) * hd] for g in range(GQA_GROUP)], axis=0)

    def block(k, vts):
        p = jnp.exp2(_dot_t(k, qs))
        return jnp.sum(p.reshape(-1, 8, cols), axis=0), _dot(jnp.concatenate(vts, axis=1), p.astype(BF16))

    def finish(l8, acc):
        o_t = acc * (1.0 / jnp.sum(l8, axis=0, keepdims=True))
        for g in range(GQA_GROUP):
            o_ref[:, g * hd:(g + 1) * hd] = o_t[:, g * tq:(g + 1) * tq].T.astype(o_ref.dtype)

    @pl.when(j < n_q_lat)
    def _():
        l8, acc = block(kx_ref[...], [vx_ref[c] for c in range(lc // ch)])
        for i in range(t // nb):
            dl, da = block(kl_ref[i * nb:(i + 1) * nb, :], [vl_ref[i * (nb // ch) + c] for c in range(nb // ch)])
            l8, acc = l8 + dl, acc + da
        finish(l8, acc)

    @pl.when(j >= n_q_lat)
    def _():
        finish(*block(kx_ref[...], [vx_ref[c] for c in range(lc // ch)]))


def dense_attn(lay, q, k, vt, hd, with_ctx_queries, shifted):
    nt, qw = q.shape
    tq = 128 if shifted else 256
    ch = vt.shape[2]
    assert lay.t % tq == 0 and lay.lc % tq == 0 and lay.n_lat % lay.lc == 0 and lay.t % ch == 0 and lay.lc % ch == 0
    n_kv = k.shape[1] // hd
    gw = GQA_GROUP * hd
    nql, nqc = lay.t // tq, lay.lc // tq
    ctx_q0 = lay.n_lat // tq
    ctx_k0 = lay.n_lat // lay.lc
    n_rows = nt if with_ctx_queries else lay.n_lat

    def qblock(b, kv, j):
        return (jnp.where(j < nql, b * nql + j, ctx_q0 + b * nqc + (j - nql)), kv)

    qspec = pl.BlockSpec((tq, gw), qblock)
    cols = GQA_GROUP * tq
    nb = next((n for n in (4096, 1024) if lay.t % n == 0), ch) if not shifted else (1024 if lay.t % 1024 == 0 else ch)
    assert nb % ch == 0 and lay.t % nb == 0
    if shifted:
        body = _dense_attn_kernel
        scratch = [pltpu.VMEM((2, nb, cols), F32), pltpu.VMEM((1, cols), F32), pltpu.VMEM((8, cols), F32),
                   pltpu.VMEM((hd, cols), F32)]
    else:
        body = _dense_attn_unshifted_kernel
        scratch = []
    return pl.pallas_call(
        functools.partial(body, n_q_lat=nql, nb=nb),
        out_shape=jax.ShapeDtypeStruct((n_rows, qw), BF16),
        grid=(lay.b, n_kv, nql + (nqc if with_ctx_queries else 0)),
        in_specs=[
            qspec,
            pl.BlockSpec((lay.t, hd), lambda b, kv, j: (b, kv)),
            pl.BlockSpec((lay.lc, hd), lambda b, kv, j: (ctx_k0 + b, kv)),
            pl.BlockSpec((lay.t // ch, hd, ch), lambda b, kv, j: (b, kv, 0)),
            pl.BlockSpec((lay.lc // ch, hd, ch), lambda b, kv, j: (ctx_k0 + b, kv, 0)),
        ],
        out_specs=qspec,
        scratch_shapes=scratch,
        compiler_params=pltpu.CompilerParams(vmem_limit_bytes=V7X_VMEM_LIMIT),
        name="dense_attn" if shifted else "dense_attn_unshifted",
    )(q, k, k, vt, vt)


def _split_pairs(w, hd, dup=False):
    d = w.shape[0]
    pairs = w.reshape(d, -1, hd // 2, 2)
    if dup:
        pairs = jnp.concatenate([pairs[:, :, None], pairs[:, :, None]], axis=2).reshape(d, -1, hd // 2, 2)
    per_group = LANES // hd
    groups = pairs.reshape(d, -1, per_group, hd // 2, 2)
    return jnp.moveaxis(groups, 4, 2).reshape(d, -1)


def _swap_halves(w):
    halves = w.reshape(w.shape[0], -1, 2, LANES // 2)
    return jnp.concatenate([halves[:, :, 1:], halves[:, :, :1]], axis=2).reshape(w.shape)


def _block_diag(w):
    h, n, _ = w.shape
    eye = jnp.eye(h, dtype=w.dtype)
    return (eye[:, None, :, None] * w[:, :, None, :]).reshape(h * n, h * n)


def _rope_tables(t, hd, ident_rows):
    n_freq = hd // 4
    freq = ROPE_THETA ** (-jnp.arange(n_freq, dtype=F32) / n_freq)
    pos = jnp.arange(t)
    row = (pos // GRID_W).astype(F32)
    col = (pos % GRID_W).astype(F32)
    ang = jnp.concatenate([row[:, None] * freq, col[:, None] * freq], axis=-1)
    reps = LANES // hd
    cos = jnp.tile(jnp.cos(ang), (1, reps))
    sin = jnp.tile(jnp.sin(ang), (1, reps))
    cos = jnp.concatenate([cos, jnp.ones((ident_rows, LANES // 2), F32)], axis=0)
    sin = jnp.concatenate([sin, jnp.zeros((ident_rows, LANES // 2), F32)], axis=0)
    return jnp.concatenate([cos, cos], axis=1), jnp.concatenate([-sin, sin], axis=1)


def kernel(x, c, ctx, c_ctx, w_ada, b_ada, norm_pre, norm_post, ffn_w_gate, ffn_w_up, ffn_w_down, even_w_in,
           even_conv_w, even_conv_b, lru_w_a, lru_b_a, lru_w_x, lru_b_x, lru_lambda, attn_sink, even_w_out, odd_w_in,
           odd_q_norm, odd_k_norm, odd_w_out):
    b, t, d = x.shape
    lc = ctx.shape[1]
    depth = w_ada.shape[0]
    lay = _Layout(b, t, lc, d)
    assert b + 1 <= MOD_ROWS and w_ada.shape[2] == N_MOD * d

    lw = even_conv_w.shape[2]
    win_heads = attn_sink.shape[1]
    win_hd = (d - lw) // win_heads
    win_kv = win_heads // GQA_GROUP
    glb_hd = odd_q_norm.shape[1]
    glb_heads = d // glb_hd
    glb_kv = glb_heads // GQA_GROUP
    qw_e, kw_e = win_heads * win_hd, win_kv * win_hd
    qw_o, kw_o = glb_heads * glb_hd, glb_kv * glb_hd

    x_all = (x.reshape(b * t, d), ctx.reshape(b * lc, d))
    c_rows = jnp.concatenate([c, c_ctx[None, :], jnp.zeros((MOD_ROWS - b - 1, d), F32)], axis=0)
    mod = ada_mod(c_rows, w_ada, b_ada).reshape(depth, MOD_ROWS, N_MOD, d)

    tm = lay.tile(IN_TILES)
    cos_e, sin_e = _rope_tables(t, win_hd, tm)
    cos_o, sin_o = _rope_tables(t, glb_hd, tm)

    wg16, wu16, wd16 = ffn_w_gate.astype(BF16), ffn_w_up.astype(BF16), ffn_w_down.astype(BF16)

    for l in range(depth):
        last = l == depth - 1
        i = l // 2
        mod_l = mod[l]
        x_all = ffn_sublayer(lay, x_all, lay.nt, mod_l, 0, norm_pre[l, 0], norm_post[l, 0], wg16, wu16, wd16, (l, 0))
        rows_out = lay.n_lat if last else lay.nt
        if l % 2 == 0:
            w = even_w_in[i]
            o1, o2, o3, o4 = 2 * lw, 2 * lw + qw_e, 2 * lw + qw_e + kw_e, 2 * lw + qw_e + 2 * kw_e
            wq = w[:, o1:o2]
            wk = w[:, o2:o3]
            w_cat = jnp.concatenate([
                w[:, :o1], _split_pairs(wq, win_hd), _split_pairs(wk, win_hd, dup=True), w[:, o3:o4]], axis=1).astype(BF16)
            xa, gg, q, kd, vt = even_in(lay, x_all, mod_l, norm_pre[l, 1], w_cat, cos_e, sin_e, lw, qw_e, 2 * kw_e, kw_e,
                                        win_hd ** -0.5 * LOG2_E)
            spl_args = []
            for direction in range(2):
                w_gate = jnp.concatenate([_block_diag(lru_w_a[i, direction]), _block_diag(lru_w_x[i, direction])], axis=1).astype(BF16)
                b_gate = jnp.concatenate([lru_b_a[i, direction], lru_b_x[i, direction]])
                spl_args.append((w_gate, b_gate, lru_lambda[i, direction]))
            hf = lru_scan(lay, xa, even_conv_w[i], even_conv_b[i], *spl_args[0], reverse=False)
            ya = lru_scan(lay, xa, even_conv_w[i], even_conv_b[i], *spl_args[1], reverse=True, hf=hf, gg=gg)
            sink_rows = jnp.broadcast_to(attn_sink[i][:, None], (win_heads, 128)).astype(F32)
            ob = win_attn(lay, q, kd, vt, sink_rows, win_kv, win_hd)
            mix = (norm_post[l, 1], even_w_out[i].astype(BF16), [ya, ob])
        else:
            w = odd_w_in[i]
            wq, wk, wv = w[:, :qw_o], w[:, qw_o:qw_o + kw_o], w[:, qw_o + kw_o:]
            wq, wk = _split_pairs(wq, glb_hd), _split_pairs(wk, glb_hd)
            w_cat = jnp.concatenate([wq, _swap_halves(wq), wk, _swap_halves(wk), wv], axis=1).astype(BF16)
            gq, gk = (_split_pairs(g[None, :], glb_hd) for g in (odd_q_norm[i], odd_k_norm[i]))
            gq, gk = (jnp.concatenate([g, jnp.roll(g, LANES // 2, axis=1)], axis=0) for g in (gq, gk))
            q, k, vt = odd_in(lay, x_all, mod_l, norm_pre[l, 1], w_cat, cos_o, sin_o, gq, gk, qw_o, kw_o, glb_hd)
            bound = LOG2_E * glb_hd ** 0.5 * jnp.max(jnp.abs(odd_q_norm[i])) * jnp.max(jnp.abs(odd_k_norm[i]))
            o = lax.cond(
                bound <= MAX_UNSHIFTED_SCORE,
                functools.partial(dense_attn, lay, hd=glb_hd, with_ctx_queries=not last, shifted=False),
                functools.partial(dense_attn, lay, hd=glb_hd, with_ctx_queries=not last, shifted=True),
                q, k, vt)
            mix = (norm_post[l, 1], odd_w_out[i].astype(BF16), [o])
        x_all = ffn_sublayer(lay, x_all, rows_out, mod_l, 6, norm_pre[l, 2], norm_post[l, 2], wg16, wu16, wd16, (l, 1),
                             mix=mix)
    return x_all[:b * t].reshape(b, t, d)
```

```python
import functools

import jax
import jax.numpy as jnp
from jax import lax
from jax.experimental import pallas as pl
from jax.experimental.pallas import tpu as pltpu

F32 = jnp.float32
BF16 = jnp.bfloat16

EPS = 1e-6
FFN_RES = 0.5
LRU_C = 8.0
WINDOW = 128
GRID_W = 64
ROPE_THETA = 10000.0
GQA_GROUP = 4
N_MOD = 9
MOD_ROWS = 16

V7X_VMEM_LIMIT = 56 * 1024 * 1024
Q_TILE_WIN = 256
IN_TILES = (1024, 512, 256, 128)
WIN_SUB = 256
VT_CHUNK = 128
LRU_CHUNK = 256
LANES = 128
SEG_PAD = 4
HALO = 8
FFN_SUBTILES = 4
KEY_CHUNK = 256
LOG2_E = 1.4426950408889634
MAX_UNSHIFTED_SCORE = 60.0


def _const_spec(shape):
    zeros = (0,) * len(shape)
    return pl.BlockSpec(shape, lambda *_: zeros)


def _resident_spec(shape):
    zeros = (0,) * len(shape)
    return pl.BlockSpec(shape, lambda *_: zeros, pipeline_mode=pl.Buffered(1))


def _dot(a, b):
    return jnp.dot(a, b, preferred_element_type=F32)


def _dot_t(a, b):
    return lax.dot_general(a, b, (((1,), (1,)), ((), ())), preferred_element_type=F32)


def _rope(x, cos, sin_signed):
    return x * cos + pltpu.roll(x, LANES // 2, 1) * sin_signed


def _rms(x, g):
    return x * lax.rsqrt(jnp.mean(x * x, axis=-1, keepdims=True) + EPS) * g


def _pre(x, g, shift, scale):
    return _rms(x, g) * (1.0 + scale) + shift


def _ada_kernel(c_ref, w_ref, b_ref, o_ref):
    c = c_ref[...]
    s = c * jax.nn.sigmoid(c)
    o_ref[...] = _dot(s.astype(BF16), w_ref[...].astype(BF16)) + b_ref[...]


def ada_mod(c_rows, w_ada, b_ada):
    depth, d, n = w_ada.shape
    tn = 1024
    return pl.pallas_call(
        _ada_kernel,
        out_shape=jax.ShapeDtypeStruct((depth, MOD_ROWS, n), F32),
        grid=(depth, n // tn),
        in_specs=[
            pl.BlockSpec((MOD_ROWS, d), lambda l, j: (0, 0)),
            pl.BlockSpec((None, d, tn), lambda l, j: (l, 0, j)),
            pl.BlockSpec((None, 1, tn), lambda l, j: (l, 0, j)),
        ],
        out_specs=pl.BlockSpec((None, MOD_ROWS, tn), lambda l, j: (l, 0, j)),
        name="ada_mod",
    )(c_rows, w_ada, b_ada.reshape(depth, 1, n))


class _Layout:
    def __init__(self, b, t, lc, d):
        self.b, self.t, self.lc, self.d = b, t, lc, d
        self.n_lat = b * t
        self.nt = b * (t + lc)

    def tile(self, candidates):
        for tm in candidates:
            if self.t % tm == 0 and (self.b * self.lc) % tm == 0:
                return tm
        raise ValueError("no token tile divides both the latent and the context stream")

    def mod_row(self, tm):
        n_lat_tiles, per_b, b = self.n_lat // tm, self.t // tm, self.b
        return lambda i: jnp.where(i < n_lat_tiles, i // per_b, b)

    def rope_block(self, tm):
        n_lat_tiles, per_b = self.n_lat // tm, self.t // tm
        return lambda i: jnp.where(i < n_lat_tiles, i % per_b, per_b)


def _mod_spec(lay, tm):
    row = lay.mod_row(tm)
    return pl.BlockSpec((None, N_MOD, lay.d), lambda i: (row(i), 0, 0))


def _ffn_kernel(x_ref, *rest, m0, ff_chunk, n_lat_tiles, subtiles):
    ctx_ref = None
    if n_lat_tiles is not None:
        ctx_ref, rest = rest[0], rest[1:]
    mod_ref, gpre_ref, gpost_ref, wg_ref, wu_ref, wd_ref = rest[:6]
    rest = rest[6:]
    o_ref, h_ref = rest[-2:]
    tm = x_ref.shape[0]
    d_ff = wg_ref.shape[1]
    sub_rows = [slice(r0, r0 + tm // subtiles) for r0 in range(0, tm, tm // subtiles)]
    chunks = list(range(0, d_ff, ff_chunk))
    xs, us, ys = {}, {}, {}

    def before(k):
        rows = sub_rows[k]
        x = x_ref[rows, :]
        if ctx_ref is not None:
            x = jnp.where(pl.program_id(0) < n_lat_tiles, x, ctx_ref[rows, :])
        if len(rest) > 2:
            gmix_ref, wmix_ref = rest[:2]
            parts = [r[rows, :] for r in rest[2:-2]]
            y = _dot(jnp.concatenate(parts, axis=1) if len(parts) > 1 else parts[0], wmix_ref[...])
            x = x + mod_ref[5:6, :] * _rms(y, gmix_ref[...])
        xs[k] = x
        us[k] = _pre(x, gpre_ref[...], mod_ref[m0:m0 + 1, :], mod_ref[m0 + 1:m0 + 2, :]).astype(BF16)

    def expand(k, some_chunks):
        for c0 in some_chunks:
            g = _dot(us[k], wg_ref[:, c0:c0 + ff_chunk])
            up = _dot(us[k], wu_ref[:, c0:c0 + ff_chunk])
            h_ref[sub_rows[k], c0:c0 + ff_chunk] = (g * jax.nn.sigmoid(g) * up).astype(BF16)

    def after(k):
        o_ref[sub_rows[k], :] = xs[k] + FFN_RES * mod_ref[m0 + 2:m0 + 3, :] * _rms(ys[k], gpost_ref[...])

    before(0)
    for k in range(subtiles):
        expand(k, chunks[:len(chunks) // 2])
        if k > 0:
            after(k - 1)
        if k + 1 < subtiles:
            before(k + 1)
        expand(k, chunks[len(chunks) // 2:])
        ys[k] = _dot(h_ref[sub_rows[k], :], wd_ref[...])
    after(subtiles - 1)


def ffn_sublayer(lay, x_all, n_rows, mod_l, m0, g_pre, g_post, wg, wu, wd, which, mix=None):
    d = lay.d
    d_ff = wg.shape[-1]
    picked = lambda r, c: pl.BlockSpec((None, None, r, c), lambda i: which + (0, 0), pipeline_mode=pl.Buffered(1))
    tm = lay.tile((1024, 512, 256, 128))
    row = lambda i: (i, 0)
    n_lat_tiles = None
    if isinstance(x_all, tuple):
        n_lat_tiles = lay.n_lat // tm
        in_specs = [pl.BlockSpec((tm, d), lambda i: (jnp.minimum(i, n_lat_tiles - 1), 0)),
                    pl.BlockSpec((tm, d), lambda i: (jnp.maximum(i - n_lat_tiles, 0), 0))]
        args = list(x_all)
    else:
        in_specs = [pl.BlockSpec((tm, d), row)]
        args = [x_all]
    in_specs += [
        _mod_spec(lay, tm),
        _const_spec((1, d)),
        _const_spec((1, d)),
        picked(d, d_ff),
        picked(d, d_ff),
        picked(d_ff, d),
    ]
    args += [mod_l, g_pre.reshape(1, d), g_post.reshape(1, d), wg, wu, wd]
    if mix is not None:
        g_mix, w_mix, parts = mix
        in_specs += [_const_spec((1, d)), _resident_spec(w_mix.shape)] + [pl.BlockSpec((tm, p.shape[1]), row) for p in parts]
        args += [g_mix.reshape(1, d), w_mix] + list(parts)
    return pl.pallas_call(
        functools.partial(_ffn_kernel, m0=m0, ff_chunk=256, n_lat_tiles=n_lat_tiles,
                          subtiles=FFN_SUBTILES if mix is None and tm % (8 * FFN_SUBTILES) == 0 else 1),
        out_shape=jax.ShapeDtypeStruct((n_rows, d), F32),
        grid=(n_rows // tm,),
        in_specs=in_specs,
        out_specs=pl.BlockSpec((tm, d), row),
        scratch_shapes=[pltpu.VMEM((tm, d_ff), BF16)],
        compiler_params=pltpu.CompilerParams(vmem_limit_bytes=V7X_VMEM_LIMIT),
        name="ffn_sublayer" if mix is None else "mix_ffn_sublayer",
    )(*args)


def _even_in_kernel(x_ref, mod_ref, g_ref, w_ref, cos_ref, sin_ref, xa_ref, gg_ref, q_ref, k_ref, vt_ref, *, lw, qw, kw, vw,
                    q_scale):
    u = _pre(x_ref[...], g_ref[...], mod_ref[3:4, :], mod_ref[4:5, :]).astype(BF16)
    cos, sin = cos_ref[...], sin_ref[...]
    o = 0
    xa_ref[...] = _dot(u, w_ref[:, o:o + lw])
    o += lw
    gg_ref[...] = jax.nn.gelu(_dot(u, w_ref[:, o:o + lw]))
    o += lw
    q = _dot(u, w_ref[:, o:o + qw])
    o += qw
    for c0 in range(0, qw, LANES):
        q_ref[:, c0:c0 + LANES] = (_rope(q[:, c0:c0 + LANES], cos, sin) * q_scale).astype(BF16)
    k = _dot(u, w_ref[:, o:o + kw])
    o += kw
    for c0 in range(0, kw, LANES):
        k_ref[:, c0:c0 + LANES] = _rope(k[:, c0:c0 + LANES], cos, sin).astype(BF16)
    v = _dot(u, w_ref[:, o:o + vw])
    ch = vt_ref.shape[2]
    for c in range(vt_ref.shape[0]):
        vt_ref[c] = v[c * ch:(c + 1) * ch, :].T.astype(BF16)


def even_in(lay, x_all, mod_l, g_pre, w_cat, cos_t, sin_t, lw, qw, kw, vw, q_scale):
    d = lay.d
    tm = lay.tile(IN_TILES)
    rb = lay.rope_block(tm)
    nt = lay.nt
    row = lambda i: (i, 0)
    ch = VT_CHUNK
    assert tm % ch == 0
    return pl.pallas_call(
        functools.partial(_even_in_kernel, lw=lw, qw=qw, kw=kw, vw=vw, q_scale=q_scale),
        out_shape=(
            jax.ShapeDtypeStruct((nt, lw), F32),
            jax.ShapeDtypeStruct((nt, lw), F32),
            jax.ShapeDtypeStruct((nt, qw), BF16),
            jax.ShapeDtypeStruct((nt, kw), BF16),
            jax.ShapeDtypeStruct((nt // ch, vw, ch), BF16),
        ),
        grid=(nt // tm,),
        in_specs=[
            pl.BlockSpec((tm, d), row),
            _mod_spec(lay, tm),
            _const_spec((1, d)),
            _resident_spec(w_cat.shape),
            pl.BlockSpec((tm, cos_t.shape[1]), lambda i: (rb(i), 0)),
            pl.BlockSpec((tm, cos_t.shape[1]), lambda i: (rb(i), 0)),
        ],
        out_specs=(
            pl.BlockSpec((tm, lw), row),
            pl.BlockSpec((tm, lw), row),
            pl.BlockSpec((tm, qw), row),
            pl.BlockSpec((tm, kw), row),
            pl.BlockSpec((tm // ch, vw, ch), lambda i: (i, 0, 0)),
        ),
        compiler_params=pltpu.CompilerParams(vmem_limit_bytes=V7X_VMEM_LIMIT),
        name="even_in",
    )(x_all, mod_l, g_pre.reshape(1, d), w_cat, cos_t, sin_t)


def _shift_rows(x, dist, fill, reverse):
    n = x.shape[0]
    rows = lax.broadcasted_iota(jnp.int32, x.shape, 0)
    if reverse:
        return jnp.where(rows < n - dist, pltpu.roll(x, n - dist, 0), fill)
    return jnp.where(rows >= dist, pltpu.roll(x, dist, 0), fill)


def _lru_kernel(xm_ref, xl_ref, xr_ref, cw_ref, cb_ref, wg_ref, bg_ref, lam_ref, *rest,
                reverse, n_ctx_chunks, n_lat_chunks):
    if reverse:
        hf_ref, gg_ref, o_ref, xbuf, carry, a_s, b_s, h_s = rest
    else:
        o_ref, xbuf, carry, a_s, b_s, h_s = rest
    s = pl.program_id(1)
    cl, lw = xm_ref.shape
    in_ctx = s < n_ctx_chunks
    pos = jnp.where(in_ctx, s, s - n_ctx_chunks)
    n_seq = jnp.where(in_ctx, n_ctx_chunks, n_lat_chunks)
    chunk = (n_seq - 1 - pos) if reverse else pos

    @pl.when(s == 0)
    def _():
        carry[...] = jnp.zeros_like(carry)

    xbuf[0:HALO, :] = jnp.where(chunk > 0, xl_ref[...], 0.0)
    xbuf[HALO:HALO + cl, :] = xm_ref[...]
    xbuf[HALO + cl:, :] = jnp.where(chunk < n_seq - 1, xr_ref[...], 0.0)
    taps = cw_ref.shape[0]
    left = taps // 2
    xin = cb_ref[...] + xbuf[pl.ds(HALO - left, cl), :] * cw_ref[0:1, :]
    for t in range(1, taps):
        xin = xin + xbuf[pl.ds(HALO - left + t, cl), :] * cw_ref[t:t + 1, :]

    gates = _dot(xin.astype(BF16), wg_ref[...]) + bg_ref[...]
    r = 0.5 * jnp.tanh(0.5 * gates[:, :lw]) + 0.5
    gate_i = 0.5 * jnp.tanh(0.5 * gates[:, lw:]) + 0.5
    log_a = (-LRU_C) * r * jax.nn.softplus(-lam_ref[...])
    a = jnp.exp(log_a)
    b = jnp.sqrt(-jnp.tanh(log_a) * (a * a + 1.0)) * gate_i * xin

    seg = cl // 8
    pitch = a_s.shape[1] // 8
    order = range(seg - 1, -1, -1) if reverse else range(seg)
    for g in range(lw // LANES):
        cs = slice(g * LANES, (g + 1) * LANES)
        for i in range(8):
            a_s[g, i * pitch:i * pitch + seg, :] = a[i * seg:(i + 1) * seg, cs]
            b_s[g, i * pitch:i * pitch + seg, :] = b[i * seg:(i + 1) * seg, cs]
        step = lambda ref, j: ref[g, pl.ds(j, 8, stride=pitch), :]
        for n, j in enumerate(order):
            aj, bj = step(a_s, j), step(b_s, j)
            f, p = (bj, aj) if n == 0 else (aj * f + bj, aj * p)
        first = lax.broadcasted_iota(jnp.int32, f.shape, 0) == (7 if reverse else 0)
        f = f + jnp.where(first, p * carry[:, cs], 0.0)
        dist = 1
        while dist < 8:
            f = f + p * _shift_rows(f, dist, 0.0, reverse)
            p = p * _shift_rows(p, dist, 1.0, reverse)
            dist *= 2
        h = jnp.where(first, carry[:, cs], _shift_rows(f, 1, 0.0, reverse))
        carry[:, cs] = f[0:1, :] if reverse else f[7:8, :]
        for j in order:
            h = step(a_s, j) * h + step(b_s, j)
            h_s[g, pl.ds(j, 8, stride=pitch), :] = h
    h = jnp.concatenate(
        [jnp.concatenate([h_s[g, i * pitch:i * pitch + seg, :] for i in range(8)], axis=0) for g in range(lw // LANES)],
        axis=1)
    if reverse:
        o_ref[...] = ((hf_ref[...] + h) * gg_ref[...]).astype(o_ref.dtype)
    else:
        o_ref[...] = h


def lru_scan(lay, xa, conv_w, conv_b, w_gate, b_gate, lam, reverse, hf=None, gg=None):
    nt, lw = xa.shape
    cl = LRU_CHUNK
    assert lay.t % cl == 0 and lay.lc % cl == 0
    ncc, nlc = lay.lc // cl, lay.t // cl
    n_lat_blocks = lay.n_lat // cl

    def chunk_block(b, s):
        in_ctx = s < ncc
        pos = jnp.where(in_ctx, s, s - ncc)
        if reverse:
            pos = jnp.where(in_ctx, ncc - 1 - pos, nlc - 1 - pos)
        return jnp.where(in_ctx, n_lat_blocks + b * ncc + pos, b * nlc + pos)

    per = cl // HALO
    last_halo = nt // HALO - 1
    main = pl.BlockSpec((cl, lw), lambda b, s: (chunk_block(b, s), 0))
    in_specs = [
        main,
        pl.BlockSpec((HALO, lw), lambda b, s: (jnp.maximum(chunk_block(b, s) * per - 1, 0), 0)),
        pl.BlockSpec((HALO, lw), lambda b, s: (jnp.minimum((chunk_block(b, s) + 1) * per, last_halo), 0)),
        _const_spec(conv_w.shape),
        _const_spec((1, lw)),
        _const_spec(w_gate.shape),
        _const_spec((1, 2 * lw)),
        _const_spec((1, lw)),
    ]
    args = [xa, xa, xa, conv_w, conv_b.reshape(1, lw), w_gate, b_gate.reshape(1, 2 * lw), lam.reshape(1, lw)]
    if reverse:
        in_specs += [main, main]
        args += [hf, gg]
    return pl.pallas_call(
        functools.partial(_lru_kernel, reverse=reverse, n_ctx_chunks=ncc, n_lat_chunks=nlc),
        out_shape=jax.ShapeDtypeStruct((nt, lw), BF16 if reverse else F32),
        grid=(lay.b, ncc + nlc),
        in_specs=in_specs,
        out_specs=main,
        scratch_shapes=[pltpu.VMEM((cl + 2 * HALO, lw), F32), pltpu.VMEM((1, lw), F32)]
        + [pltpu.VMEM((lw // LANES, cl + 8 * SEG_PAD, LANES), F32)] * 3,
        compiler_params=pltpu.CompilerParams(dimension_semantics=("arbitrary", "arbitrary")),
        name="lru_bwd" if reverse else "lru_fwd",
    )(*args)


def _win_attn_kernel(q_ref, kl_ref, kx_ref, vl_ref, vx_ref, sink_ref, o_ref, *, n_q_lat, n_kv, hd):
    j = pl.program_id(1)
    tq = q_ref.shape[0]
    t = kl_ref.shape[0]
    lc = kx_ref.shape[0]
    vch = vl_ref.shape[2]
    sq = WIN_SUB
    band = sq + 2 * WINDOW
    cols = GQA_GROUP * sq
    low = lax.broadcasted_iota(jnp.int32, (sq, 2 * hd), 1) % hd < hd // 2

    def colsum(x):
        return jnp.sum(jnp.sum(x.reshape(-1, 8, cols), axis=0), axis=0, keepdims=True)

    def colmax(x):
        return jnp.max(jnp.max(x.reshape(-1, 8, cols), axis=0), axis=0, keepdims=True)

    def attend(kv, r0, k_parts, vt_parts, bias):
        qs = []
        for g in range(GQA_GROUP):
            h = kv * GQA_GROUP + g
            qp = q_ref[r0:r0 + sq, 2 * hd * (h // 2):2 * hd * (h // 2 + 1)]
            qs.append(jnp.where(low if h % 2 == 0 else jnp.logical_not(low), qp, jnp.zeros_like(qp)))
        qs = jnp.concatenate(qs, axis=0)
        sink = jnp.concatenate(
            [jnp.concatenate([sink_ref[kv * GQA_GROUP + g:kv * GQA_GROUP + g + 1, :]] * (sq // LANES), axis=1)
             for g in range(GQA_GROUP)], axis=1) * LOG2_E
        s = [_dot_t(kp, qs) for kp in k_parts]
        if bias is not None:
            s[0] = s[0] + bias
        m = sink
        for sp in s:
            m = jnp.maximum(m, colmax(sp))
        den = jnp.exp2(sink - m)
        p = []
        for sp in s:
            pp = jnp.exp2(sp - m)
            den = den + colsum(pp)
            p.append(pp.astype(BF16))
        o_t = _dot(jnp.concatenate(vt_parts, axis=1), jnp.concatenate(p, axis=0)) * (1.0 / den)
        for pair in range(GQA_GROUP // 2):
            both = jnp.concatenate([o_t[:, (2 * pair + i) * sq:(2 * pair + i + 1) * sq] for i in range(2)], axis=0)
            c0 = 2 * hd * (kv * GQA_GROUP // 2 + pair)
            o_ref[r0:r0 + sq, c0:c0 + 2 * hd] = both.T.astype(o_ref.dtype)

    def ctx_parts(kv):
        ks = kx_ref[:, 2 * hd * kv:2 * hd * (kv + 1)]
        vts = [vx_ref[c][hd * kv:hd * (kv + 1), :] for c in range(lc // vch)]
        return ks, vts

    @pl.when(j < n_q_lat)
    def _():
        for r0 in range(0, tq, sq):
            q0 = j * tq + r0
            start = pl.multiple_of(jnp.clip(q0 - WINDOW, 0, t - band), LANES)
            kpos = start + lax.broadcasted_iota(jnp.int32, (band, sq), 0)
            qpos = q0 + lax.broadcasted_iota(jnp.int32, (band, sq), 1)
            bias = jnp.where(jnp.abs(kpos - qpos) > WINDOW, -jnp.inf, 0.0).astype(F32)
            bias = jnp.concatenate([bias] * GQA_GROUP, axis=1)
            c0 = start // vch
            for kv in range(n_kv):
                kx, vxs = ctx_parts(kv)
                kb = kl_ref[pl.ds(start, band), 2 * hd * kv:2 * hd * (kv + 1)]
                vbs = [vl_ref[c0 + c][hd * kv:hd * (kv + 1), :] for c in range(band // vch)]
                attend(kv, r0, [kb, kx], vbs + vxs, bias)

    @pl.when(j >= n_q_lat)
    def _():
        for r0 in range(0, tq, sq):
            for kv in range(n_kv):
                kx, vxs = ctx_parts(kv)
                attend(kv, r0, [kx], vxs, None)


def win_attn(lay, q, kd, vt, sink_rows, n_kv, hd):
    nt, qw = q.shape
    kw = kd.shape[1]
    tq = Q_TILE_WIN
    vch = vt.shape[2]
    assert lay.t % tq == 0 and lay.lc % tq == 0 and lay.n_lat % lay.lc == 0 and tq % 128 == 0
    assert vch == LANES and WINDOW % vch == 0 and lay.lc % vch == 0 and tq % WIN_SUB == 0 and lay.t >= WIN_SUB + 2 * WINDOW
    nb, ncq = lay.t // tq, lay.lc // tq
    ctx_q0 = lay.n_lat // tq
    ctx_k0 = lay.n_lat // lay.lc

    def qblock(b, j):
        return jnp.where(j < nb, b * nb + j, ctx_q0 + b * ncq + (j - nb))

    qspec = pl.BlockSpec((tq, qw), lambda b, j: (qblock(b, j), 0))
    return pl.pallas_call(
        functools.partial(_win_attn_kernel, n_q_lat=nb, n_kv=n_kv, hd=hd),
        out_shape=jax.ShapeDtypeStruct((nt, qw), BF16),
        grid=(lay.b, nb + ncq),
        in_specs=[
            qspec,
            pl.BlockSpec((lay.t, kw), lambda b, j: (b, 0)),
            pl.BlockSpec((lay.lc, kw), lambda b, j: (ctx_k0 + b, 0)),
            pl.BlockSpec((lay.t // vch, vt.shape[1], vch), lambda b, j: (b, 0, 0)),
            pl.BlockSpec((lay.lc // vch, vt.shape[1], vch), lambda b, j: (ctx_k0 + b, 0, 0)),
            _const_spec(sink_rows.shape),
        ],
        out_specs=qspec,
        compiler_params=pltpu.CompilerParams(vmem_limit_bytes=V7X_VMEM_LIMIT),
        name="win_attn",
    )(q, kd, kd, vt, vt, sink_rows)


def _odd_in_kernel(x_ref, mod_ref, g_ref, w_ref, cos_ref, sin_ref, gq_ref, gk_ref, q_ref, k_ref, vt_ref, *, qw, kw, hd):
    u = _pre(x_ref[...], g_ref[...], mod_ref[3:4, :], mod_ref[4:5, :]).astype(BF16)
    cos, sin = cos_ref[...], sin_ref[...]

    def normed_rope(o, width, gains, out_ref, post):
        cg = cos * gains[0:1, :]
        sg = sin * gains[1:2, :]
        step = 2 * hd
        for c0 in range(0, width, step):
            x2 = _dot(u, w_ref[:, o + c0:o + c0 + step])
            xp2 = _dot(u, w_ref[:, o + width + c0:o + width + c0 + step])
            for h0 in range(0, step, hd):
                x, xp = x2[:, h0:h0 + hd], xp2[:, h0:h0 + hd]
                r = lax.rsqrt(jnp.mean(x * x, axis=-1, keepdims=True) + EPS) * post
                out_ref[:, c0 + h0:c0 + h0 + hd] = (r * (x * cg + xp * sg)).astype(out_ref.dtype)

    normed_rope(0, qw, gq_ref[...], q_ref, hd ** -0.5 * LOG2_E)
    normed_rope(2 * qw, kw, gk_ref[...], k_ref, 1.0)
    o = 2 * qw + 2 * kw
    v = _dot(u, w_ref[:, o:o + kw])
    ch = vt_ref.shape[2]
    for c in range(vt_ref.shape[0]):
        vt_ref[c] = v[c * ch:(c + 1) * ch, :].T.astype(BF16)


def odd_in(lay, x_all, mod_l, g_pre, w_cat, cos_t, sin_t, gq, gk, qw, kw, hd):
    d = lay.d
    tm = lay.tile(IN_TILES)
    rb = lay.rope_block(tm)
    nt = lay.nt
    row = lambda i: (i, 0)
    ch = KEY_CHUNK
    assert tm % ch == 0
    return pl.pallas_call(
        functools.partial(_odd_in_kernel, qw=qw, kw=kw, hd=hd),
        out_shape=(
            jax.ShapeDtypeStruct((nt, qw), BF16),
            jax.ShapeDtypeStruct((nt, kw), BF16),
            jax.ShapeDtypeStruct((nt // ch, kw, ch), BF16),
        ),
        grid=(nt // tm,),
        in_specs=[
            pl.BlockSpec((tm, d), row),
            _mod_spec(lay, tm),
            _const_spec((1, d)),
            _resident_spec(w_cat.shape),
            pl.BlockSpec((tm, hd), lambda i: (rb(i), 0)),
            pl.BlockSpec((tm, hd), lambda i: (rb(i), 0)),
            _const_spec(gq.shape),
            _const_spec(gk.shape),
        ],
        out_specs=(pl.BlockSpec((tm, qw), row), pl.BlockSpec((tm, kw), row),
                   pl.BlockSpec((tm // ch, kw, ch), lambda i: (i, 0, 0))),
        compiler_params=pltpu.CompilerParams(vmem_limit_bytes=V7X_VMEM_LIMIT),
        name="odd_in",
    )(x_all, mod_l, g_pre.reshape(1, d), w_cat, cos_t, sin_t, gq, gk)


def _dense_attn_kernel(q_ref, kl_ref, kx_ref, vl_ref, vx_ref, o_ref, s_ref, m_ref, l_ref, acc_ref, *, n_q_lat, nb):
    j = pl.program_id(2)
    tq = q_ref.shape[0]
    t, hd = kl_ref.shape
    lc = kx_ref.shape[0]
    ch = vl_ref.shape[2]
    cols = GQA_GROUP * tq
    qs = jnp.concatenate([q_ref[:, g * hd:(g + 1) * hd] for g in range(GQA_GROUP)], axis=0)

    def sum8(x):
        return jnp.sum(x.reshape(-1, 8, cols), axis=0)

    def colmax(x):
        return jnp.max(jnp.max(x.reshape(-1, 8, cols), axis=0), axis=0, keepdims=True)

    sx = _dot_t(kx_ref[...], qs)
    m0 = colmax(sx)
    p = jnp.exp2(sx - m0)
    m_ref[...] = m0
    l_ref[...] = sum8(p)
    acc_ref[...] = _dot(jnp.concatenate([vx_ref[c] for c in range(lc // ch)], axis=1), p.astype(BF16))

    @pl.when(j < n_q_lat)
    def _():
        n_blocks = t // nb

        def produce(i):
            s = _dot_t(kl_ref[i * nb:(i + 1) * nb, :], qs)
            s_ref[i % 2] = s
            return colmax(s)

        m = m_ref[...]
        block_max = produce(0)
        for i in range(n_blocks):
            m_new = jnp.maximum(m, block_max)
            if i + 1 < n_blocks:
                block_max = produce(i + 1)
            alpha = jnp.exp2(m - m_new)
            p = jnp.exp2(s_ref[i % 2] - m_new)
            vt = jnp.concatenate([vl_ref[i * (nb // ch) + c] for c in range(nb // ch)], axis=1)
            l_ref[...] = l_ref[...] * alpha + sum8(p)
            acc_ref[...] = acc_ref[...] * alpha + _dot(vt, p.astype(BF16))
            m = m_new

    o_t = acc_ref[...] * (1.0 / jnp.sum(l_ref[...], axis=0, keepdims=True))
    for g in range(GQA_GROUP):
        o_ref[:, g * hd:(g + 1) * hd] = o_t[:, g * tq:(g + 1) * tq].T.astype(o_ref.dtype)


def _dense_attn_unshifted_kernel(q_ref, kl_ref, kx_ref, vl_ref, vx_ref, o_ref, *, n_q_lat, nb):
    j = pl.program_id(2)
    tq = q_ref.shape[0]
    t, hd = kl_ref.shape
    lc = kx_ref.shape[0]
    ch = vl_ref.shape[2]
    cols = GQA_GROUP * tq
    qs = jnp.concatenate([q_ref[:, g * hd:(g + 1) * hd] for g in range(GQA_GROUP)], axis=0)

    def block(k, vts):
        p = jnp.exp2(_dot_t(k, qs))
        return jnp.sum(p.reshape(-1, 8, cols), axis=0), _dot(jnp.concatenate(vts, axis=1), p.astype(BF16))

    def finish(l8, acc):
        o_t = acc * (1.0 / jnp.sum(l8, axis=0, keepdims=True))
        for g in range(GQA_GROUP):
            o_ref[:, g * hd:(g + 1) * hd] = o_t[:, g * tq:(g + 1) * tq].T.astype(o_ref.dtype)

    @pl.when(j < n_q_lat)
    def _():
        l8, acc = block(kx_ref[...], [vx_ref[c] for c in range(lc // ch)])
        for i in range(t // nb):
            dl, da = block(kl_ref[i * nb:(i + 1) * nb, :], [vl_ref[i * (nb // ch) + c] for c in range(nb // ch)])
            l8, acc = l8 + dl, acc + da
        finish(l8, acc)

    @pl.when(j >= n_q_lat)
    def _():
        finish(*block(kx_ref[...], [vx_ref[c] for c in range(lc // ch)]))


def dense_attn(lay, q, k, vt, hd, with_ctx_queries, shifted):
    nt, qw = q.shape
    tq = 128 if shifted else 256
    ch = vt.shape[2]
    assert lay.t % tq == 0 and lay.lc % tq == 0 and lay.n_lat % lay.lc == 0 and lay.t % ch == 0 and lay.lc % ch == 0
    n_kv = k.shape[1] // hd
    gw = GQA_GROUP * hd
    nql, nqc = lay.t // tq, lay.lc // tq
    ctx_q0 = lay.n_lat // tq
    ctx_k0 = lay.n_lat // lay.lc
    n_rows = nt if with_ctx_queries else lay.n_lat

    def qblock(b, kv, j):
        return (jnp.where(j < nql, b * nql + j, ctx_q0 + b * nqc + (j - nql)), kv)

    qspec = pl.BlockSpec((tq, gw), qblock)
    cols = GQA_GROUP * tq
    nb = next((n for n in (8192, 4096, 1024) if lay.t % n == 0), ch) if not shifted else (1024 if lay.t % 1024 == 0 else ch)
    assert nb % ch == 0 and lay.t % nb == 0
    if shifted:
        body = _dense_attn_kernel
        scratch = [pltpu.VMEM((2, nb, cols), F32), pltpu.VMEM((1, cols), F32), pltpu.VMEM((8, cols), F32),
                   pltpu.VMEM((hd, cols), F32)]
    else:
        body = _dense_attn_unshifted_kernel
        scratch = []
    return pl.pallas_call(
        functools.partial(body, n_q_lat=nql, nb=nb),
        out_shape=jax.ShapeDtypeStruct((n_rows, qw), BF16),
        grid=(lay.b, n_kv, nql + (nqc if with_ctx_queries else 0)),
        in_specs=[
            qspec,
            pl.BlockSpec((lay.t, hd), lambda b, kv, j: (b, kv)),
            pl.BlockSpec((lay.lc, hd), lambda b, kv, j: (ctx_k0 + b, kv)),
            pl.BlockSpec((lay.t // ch, hd, ch), lambda b, kv, j: (b, kv, 0)),
            pl.BlockSpec((lay.lc // ch, hd, ch), lambda b, kv, j: (ctx_k0 + b, kv, 0)),
        ],
        out_specs=qspec,
        scratch_shapes=scratch,
        compiler_params=pltpu.CompilerParams(vmem_limit_bytes=V7X_VMEM_LIMIT),
        name="dense_attn" if shifted else "dense_attn_unshifted",
    )(q, k, k, vt, vt)


def _split_pairs(w, hd, dup=False):
    d = w.shape[0]
    pairs = w.reshape(d, -1, hd // 2, 2)
    if dup:
        pairs = jnp.concatenate([pairs[:, :, None], pairs[:, :, None]], axis=2).reshape(d, -1, hd // 2, 2)
    per_group = LANES // hd
    groups = pairs.reshape(d, -1, per_group, hd // 2, 2)
    return jnp.moveaxis(groups, 4, 2).reshape(d, -1)


def _swap_halves(w):
    halves = w.reshape(w.shape[0], -1, 2, LANES // 2)
    return jnp.concatenate([halves[:, :, 1:], halves[:, :, :1]], axis=2).reshape(w.shape)


def _block_diag(w):
    h, n, _ = w.shape
    eye = jnp.eye(h, dtype=w.dtype)
    return (eye[:, None, :, None] * w[:, :, None, :]).reshape(h * n, h * n)


def _rope_tables(t, hd, ident_rows):
    n_freq = hd // 4
    freq = ROPE_THETA ** (-jnp.arange(n_freq, dtype=F32) / n_freq)
    pos = jnp.arange(t)
    row = (pos // GRID_W).astype(F32)
    col = (pos % GRID_W).astype(F32)
    ang = jnp.concatenate([row[:, None] * freq, col[:, None] * freq], axis=-1)
    reps = LANES // hd
    cos = jnp.tile(jnp.cos(ang), (1, reps))
    sin = jnp.tile(jnp.sin(ang), (1, reps))
    cos = jnp.concatenate([cos, jnp.ones((ident_rows, LANES // 2), F32)], axis=0)
    sin = jnp.concatenate([sin, jnp.zeros((ident_rows, LANES // 2), F32)], axis=0)
    return jnp.concatenate([cos, cos], axis=1), jnp.concatenate([-sin, sin], axis=1)


def kernel(x, c, ctx, c_ctx, w_ada, b_ada, norm_pre, norm_post, ffn_w_gate, ffn_w_up, ffn_w_down, even_w_in,
           even_conv_w, even_conv_b, lru_w_a, lru_b_a, lru_w_x, lru_b_x, lru_lambda, attn_sink, even_w_out, odd_w_in,
           odd_q_norm, odd_k_norm, odd_w_out):
    b, t, d = x.shape
    lc = ctx.shape[1]
    depth = w_ada.shape[0]
    lay = _Layout(b, t, lc, d)
    assert b + 1 <= MOD_ROWS and w_ada.shape[2] == N_MOD * d

    lw = even_conv_w.shape[2]
    win_heads = attn_sink.shape[1]
    win_hd = (d - lw) // win_heads
    win_kv = win_heads // GQA_GROUP
    glb_hd = odd_q_norm.shape[1]
    glb_heads = d // glb_hd
    glb_kv = glb_heads // GQA_GROUP
    qw_e, kw_e = win_heads * win_hd, win_kv * win_hd
    qw_o, kw_o = glb_heads * glb_hd, glb_kv * glb_hd

    x_all = (x.reshape(b * t, d), ctx.reshape(b * lc, d))
    c_rows = jnp.concatenate([c, c_ctx[None, :], jnp.zeros((MOD_ROWS - b - 1, d), F32)], axis=0)
    mod = ada_mod(c_rows, w_ada, b_ada).reshape(depth, MOD_ROWS, N_MOD, d)

    tm = lay.tile(IN_TILES)
    cos_e, sin_e = _rope_tables(t, win_hd, tm)
    cos_o, sin_o = _rope_tables(t, glb_hd, tm)

    wg16, wu16, wd16 = ffn_w_gate.astype(BF16), ffn_w_up.astype(BF16), ffn_w_down.astype(BF16)

    for l in range(depth):
        last = l == depth - 1
        i = l // 2
        mod_l = mod[l]
        x_all = ffn_sublayer(lay, x_all, lay.nt, mod_l, 0, norm_pre[l, 0], norm_post[l, 0], wg16, wu16, wd16, (l, 0))
        rows_out = lay.n_lat if last else lay.nt
        if l % 2 == 0:
            w = even_w_in[i]
            o1, o2, o3, o4 = 2 * lw, 2 * lw + qw_e, 2 * lw + qw_e + kw_e, 2 * lw + qw_e + 2 * kw_e
            wq = w[:, o1:o2]
            wk = w[:, o2:o3]
            w_cat = jnp.concatenate([
                w[:, :o1], _split_pairs(wq, win_hd), _split_pairs(wk, win_hd, dup=True), w[:, o3:o4]], axis=1).astype(BF16)
            xa, gg, q, kd, vt = even_in(lay, x_all, mod_l, norm_pre[l, 1], w_cat, cos_e, sin_e, lw, qw_e, 2 * kw_e, kw_e,
                                        win_hd ** -0.5 * LOG2_E)
            spl_args = []
            for direction in range(2):
                w_gate = jnp.concatenate([_block_diag(lru_w_a[i, direction]), _block_diag(lru_w_x[i, direction])], axis=1).astype(BF16)
                b_gate = jnp.concatenate([lru_b_a[i, direction], lru_b_x[i, direction]])
                spl_args.append((w_gate, b_gate, lru_lambda[i, direction]))
            hf = lru_scan(lay, xa, even_conv_w[i], even_conv_b[i], *spl_args[0], reverse=False)
            ya = lru_scan(lay, xa, even_conv_w[i], even_conv_b[i], *spl_args[1], reverse=True, hf=hf, gg=gg)
            sink_rows = jnp.broadcast_to(attn_sink[i][:, None], (win_heads, 128)).astype(F32)
            ob = win_attn(lay, q, kd, vt, sink_rows, win_kv, win_hd)
            mix = (norm_post[l, 1], even_w_out[i].astype(BF16), [ya, ob])
        else:
            w = odd_w_in[i]
            wq, wk, wv = w[:, :qw_o], w[:, qw_o:qw_o + kw_o], w[:, qw_o + kw_o:]
            wq, wk = _split_pairs(wq, glb_hd), _split_pairs(wk, glb_hd)
            w_cat = jnp.concatenate([wq, _swap_halves(wq), wk, _swap_halves(wk), wv], axis=1).astype(BF16)
            gq, gk = (_split_pairs(g[None, :], glb_hd) for g in (odd_q_norm[i], odd_k_norm[i]))
            gq, gk = (jnp.concatenate([g, jnp.roll(g, LANES // 2, axis=1)], axis=0) for g in (gq, gk))
            q, k, vt = odd_in(lay, x_all, mod_l, norm_pre[l, 1], w_cat, cos_o, sin_o, gq, gk, qw_o, kw_o, glb_hd)
            bound = LOG2_E * glb_hd ** 0.5 * jnp.max(jnp.abs(odd_q_norm[i])) * jnp.max(jnp.abs(odd_k_norm[i]))
            o = lax.cond(
                bound <= MAX_UNSHIFTED_SCORE,
                functools.partial(dense_attn, lay, hd=glb_hd, with_ctx_queries=not last, shifted=False),
                functools.partial(dense_attn, lay, hd=glb_hd, with_ctx_queries=not last, shifted=True),
                q, k, vt)
            mix = (norm_post[l, 1], odd_w_out[i].astype(BF16), [o])
        x_all = ffn_sublayer(lay, x_all, rows_out, mod_l, 6, norm_pre[l, 2], norm_post[l, 2], wg16, wu16, wd16, (l, 1),
                             mix=mix)
    return x_all[:b * t].reshape(b, t, d)
```

```python
import functools

import jax
import jax.numpy as jnp
from jax import lax
from jax.experimental import pallas as pl
from jax.experimental.pallas import tpu as pltpu

F32 = jnp.float32
BF16 = jnp.bfloat16

EPS = 1e-6
FFN_RES = 0.5
LRU_C = 8.0
WINDOW = 128
GRID_W = 64
ROPE_THETA = 10000.0
GQA_GROUP = 4
N_MOD = 9
MOD_ROWS = 16

V7X_VMEM_LIMIT = 56 * 1024 * 1024
Q_TILE_WIN = 256
IN_TILES = (1024, 512, 256, 128)
WIN_SUB = 256
VT_CHUNK = 128
LRU_CHUNK = 256
LANES = 128
SEG_PAD = 4
HALO = 8
FFN_CHUNK = 256
FFN_SUBTILES = 4
KEY_CHUNK = 256
LOG2_E = 1.4426950408889634
MAX_UNSHIFTED_SCORE = 60.0


def _const_spec(shape):
    zeros = (0,) * len(shape)
    return pl.BlockSpec(shape, lambda *_: zeros)


def _resident_spec(shape):
    zeros = (0,) * len(shape)
    return pl.BlockSpec(shape, lambda *_: zeros, pipeline_mode=pl.Buffered(1))


def _dot(a, b):
    return jnp.dot(a, b, preferred_element_type=F32)


def _dot_t(a, b):
    return lax.dot_general(a, b, (((1,), (1,)), ((), ())), preferred_element_type=F32)


def _rope(x, cos, sin_signed):
    return x * cos + pltpu.roll(x, LANES // 2, 1) * sin_signed


def _rms(x, g):
    return x * lax.rsqrt(jnp.mean(x * x, axis=-1, keepdims=True) + EPS) * g


def _pre(x, g, shift, scale):
    return _rms(x, g) * (1.0 + scale) + shift


def _ada_kernel(c_ref, w_ref, b_ref, o_ref):
    c = c_ref[...]
    s = c * jax.nn.sigmoid(c)
    o_ref[...] = _dot(s.astype(BF16), w_ref[...].astype(BF16)) + b_ref[...]


def ada_mod(c_rows, w_ada, b_ada):
    depth, d, n = w_ada.shape
    tn = 1024
    return pl.pallas_call(
        _ada_kernel,
        out_shape=jax.ShapeDtypeStruct((depth, MOD_ROWS, n), F32),
        grid=(depth, n // tn),
        in_specs=[
            pl.BlockSpec((MOD_ROWS, d), lambda l, j: (0, 0)),
            pl.BlockSpec((None, d, tn), lambda l, j: (l, 0, j)),
            pl.BlockSpec((None, 1, tn), lambda l, j: (l, 0, j)),
        ],
        out_specs=pl.BlockSpec((None, MOD_ROWS, tn), lambda l, j: (l, 0, j)),
        name="ada_mod",
    )(c_rows, w_ada, b_ada.reshape(depth, 1, n))


class _Layout:
    def __init__(self, b, t, lc, d):
        self.b, self.t, self.lc, self.d = b, t, lc, d
        self.n_lat = b * t
        self.nt = b * (t + lc)

    def tile(self, candidates):
        for tm in candidates:
            if self.t % tm == 0 and (self.b * self.lc) % tm == 0:
                return tm
        raise ValueError("no token tile divides both the latent and the context stream")

    def mod_row(self, tm):
        n_lat_tiles, per_b, b = self.n_lat // tm, self.t // tm, self.b
        return lambda i: jnp.where(i < n_lat_tiles, i // per_b, b)

    def rope_block(self, tm):
        n_lat_tiles, per_b = self.n_lat // tm, self.t // tm
        return lambda i: jnp.where(i < n_lat_tiles, i % per_b, per_b)


def _mod_spec(lay, tm):
    row = lay.mod_row(tm)
    return pl.BlockSpec((None, N_MOD, lay.d), lambda i: (row(i), 0, 0))


def _ffn_kernel(x_ref, *rest, m0, ff_chunk, n_lat_tiles, subtiles):
    ctx_ref = None
    if n_lat_tiles is not None:
        ctx_ref, rest = rest[0], rest[1:]
    mod_ref, gpre_ref, gpost_ref, wgu_ref, wd_ref = rest[:5]
    rest = rest[5:]
    o_ref, h_ref = rest[-2:]
    tm = x_ref.shape[0]
    d_ff = wd_ref.shape[0]
    sub_rows = [slice(r0, r0 + tm // subtiles) for r0 in range(0, tm, tm // subtiles)]
    chunks = list(range(0, d_ff, ff_chunk))
    xs, us, ys = {}, {}, {}

    def before(k):
        rows = sub_rows[k]
        x = x_ref[rows, :]
        if ctx_ref is not None:
            x = jnp.where(pl.program_id(0) < n_lat_tiles, x, ctx_ref[rows, :])
        if len(rest) > 2:
            gmix_ref, wmix_ref = rest[:2]
            parts = [r[rows, :] for r in rest[2:-2]]
            y = _dot(jnp.concatenate(parts, axis=1) if len(parts) > 1 else parts[0], wmix_ref[...])
            x = x + mod_ref[5:6, :] * _rms(y, gmix_ref[...])
        xs[k] = x
        us[k] = _pre(x, gpre_ref[...], mod_ref[m0:m0 + 1, :], mod_ref[m0 + 1:m0 + 2, :]).astype(BF16)

    def expand(k, some_chunks):
        for c0 in some_chunks:
            gu = _dot(us[k], wgu_ref[:, 2 * c0:2 * (c0 + ff_chunk)])
            g, up = gu[:, :ff_chunk], gu[:, ff_chunk:]
            h_ref[sub_rows[k], c0:c0 + ff_chunk] = (g * jax.nn.sigmoid(g) * up).astype(BF16)

    def after(k):
        o_ref[sub_rows[k], :] = xs[k] + FFN_RES * mod_ref[m0 + 2:m0 + 3, :] * _rms(ys[k], gpost_ref[...])

    before(0)
    for k in range(subtiles):
        expand(k, chunks[:len(chunks) // 2])
        if k > 0:
            after(k - 1)
        if k + 1 < subtiles:
            before(k + 1)
        expand(k, chunks[len(chunks) // 2:])
        ys[k] = _dot(h_ref[sub_rows[k], :], wd_ref[...])
    after(subtiles - 1)


def ffn_sublayer(lay, x_all, n_rows, mod_l, m0, g_pre, g_post, wgu, wd, which, mix=None):
    d = lay.d
    d_ff = wd.shape[-2]
    picked = lambda r, c: pl.BlockSpec((None, None, r, c), lambda i: which + (0, 0), pipeline_mode=pl.Buffered(1))
    tm = lay.tile((1024, 512, 256, 128))
    row = lambda i: (i, 0)
    n_lat_tiles = None
    if isinstance(x_all, tuple):
        n_lat_tiles = lay.n_lat // tm
        in_specs = [pl.BlockSpec((tm, d), lambda i: (jnp.minimum(i, n_lat_tiles - 1), 0)),
                    pl.BlockSpec((tm, d), lambda i: (jnp.maximum(i - n_lat_tiles, 0), 0))]
        args = list(x_all)
    else:
        in_specs = [pl.BlockSpec((tm, d), row)]
        args = [x_all]
    in_specs += [
        _mod_spec(lay, tm),
        _const_spec((1, d)),
        _const_spec((1, d)),
        picked(d, 2 * d_ff),
        picked(d_ff, d),
    ]
    args += [mod_l, g_pre.reshape(1, d), g_post.reshape(1, d), wgu, wd]
    if mix is not None:
        g_mix, w_mix, parts = mix
        in_specs += [_const_spec((1, d)), _resident_spec(w_mix.shape)] + [pl.BlockSpec((tm, p.shape[1]), row) for p in parts]
        args += [g_mix.reshape(1, d), w_mix] + list(parts)
    return pl.pallas_call(
        functools.partial(_ffn_kernel, m0=m0, ff_chunk=FFN_CHUNK, n_lat_tiles=n_lat_tiles,
                          subtiles=FFN_SUBTILES if mix is None and tm % (8 * FFN_SUBTILES) == 0 else 1),
        out_shape=jax.ShapeDtypeStruct((n_rows, d), F32),
        grid=(n_rows // tm,),
        in_specs=in_specs,
        out_specs=pl.BlockSpec((tm, d), row),
        scratch_shapes=[pltpu.VMEM((tm, d_ff), BF16)],
        compiler_params=pltpu.CompilerParams(vmem_limit_bytes=V7X_VMEM_LIMIT),
        name="ffn_sublayer" if mix is None else "mix_ffn_sublayer",
    )(*args)


def _even_in_kernel(x_ref, mod_ref, g_ref, w_ref, cos_ref, sin_ref, xa_ref, gg_ref, q_ref, k_ref, vt_ref, *, lw, qw, kw, vw,
                    q_scale):
    u = _pre(x_ref[...], g_ref[...], mod_ref[3:4, :], mod_ref[4:5, :]).astype(BF16)
    cos, sin = cos_ref[...], sin_ref[...]
    o = 0
    xa_ref[...] = _dot(u, w_ref[:, o:o + lw])
    o += lw
    gg_ref[...] = jax.nn.gelu(_dot(u, w_ref[:, o:o + lw]))
    o += lw
    q = _dot(u, w_ref[:, o:o + qw])
    o += qw
    for c0 in range(0, qw, LANES):
        q_ref[:, c0:c0 + LANES] = (_rope(q[:, c0:c0 + LANES], cos, sin) * q_scale).astype(BF16)
    k = _dot(u, w_ref[:, o:o + kw])
    o += kw
    for c0 in range(0, kw, LANES):
        k_ref[:, c0:c0 + LANES] = _rope(k[:, c0:c0 + LANES], cos, sin).astype(BF16)
    v = _dot(u, w_ref[:, o:o + vw])
    ch = vt_ref.shape[2]
    for c in range(vt_ref.shape[0]):
        vt_ref[c] = v[c * ch:(c + 1) * ch, :].T.astype(BF16)


def even_in(lay, x_all, mod_l, g_pre, w_cat, cos_t, sin_t, lw, qw, kw, vw, q_scale):
    d = lay.d
    tm = lay.tile(IN_TILES)
    rb = lay.rope_block(tm)
    nt = lay.nt
    row = lambda i: (i, 0)
    ch = VT_CHUNK
    assert tm % ch == 0
    return pl.pallas_call(
        functools.partial(_even_in_kernel, lw=lw, qw=qw, kw=kw, vw=vw, q_scale=q_scale),
        out_shape=(
            jax.ShapeDtypeStruct((nt, lw), F32),
            jax.ShapeDtypeStruct((nt, lw), F32),
            jax.ShapeDtypeStruct((nt, qw), BF16),
            jax.ShapeDtypeStruct((nt, kw), BF16),
            jax.ShapeDtypeStruct((nt // ch, vw, ch), BF16),
        ),
        grid=(nt // tm,),
        in_specs=[
            pl.BlockSpec((tm, d), row),
            _mod_spec(lay, tm),
            _const_spec((1, d)),
            _resident_spec(w_cat.shape),
            pl.BlockSpec((tm, cos_t.shape[1]), lambda i: (rb(i), 0)),
            pl.BlockSpec((tm, cos_t.shape[1]), lambda i: (rb(i), 0)),
        ],
        out_specs=(
            pl.BlockSpec((tm, lw), row),
            pl.BlockSpec((tm, lw), row),
            pl.BlockSpec((tm, qw), row),
            pl.BlockSpec((tm, kw), row),
            pl.BlockSpec((tm // ch, vw, ch), lambda i: (i, 0, 0)),
        ),
        compiler_params=pltpu.CompilerParams(vmem_limit_bytes=V7X_VMEM_LIMIT),
        name="even_in",
    )(x_all, mod_l, g_pre.reshape(1, d), w_cat, cos_t, sin_t)


def _shift_rows(x, dist, fill, reverse):
    n = x.shape[0]
    rows = lax.broadcasted_iota(jnp.int32, x.shape, 0)
    if reverse:
        return jnp.where(rows < n - dist, pltpu.roll(x, n - dist, 0), fill)
    return jnp.where(rows >= dist, pltpu.roll(x, dist, 0), fill)


def _lru_kernel(xm_ref, xl_ref, xr_ref, cw_ref, cb_ref, wg_ref, bg_ref, lam_ref, *rest,
                reverse, n_ctx_chunks, n_lat_chunks):
    if reverse:
        hf_ref, gg_ref, o_ref, xbuf, carry, a_s, b_s, h_s = rest
    else:
        o_ref, xbuf, carry, a_s, b_s, h_s = rest
    s = pl.program_id(1)
    cl, lw = xm_ref.shape
    in_ctx = s < n_ctx_chunks
    pos = jnp.where(in_ctx, s, s - n_ctx_chunks)
    n_seq = jnp.where(in_ctx, n_ctx_chunks, n_lat_chunks)
    chunk = (n_seq - 1 - pos) if reverse else pos

    @pl.when(s == 0)
    def _():
        carry[...] = jnp.zeros_like(carry)

    xbuf[0:HALO, :] = jnp.where(chunk > 0, xl_ref[...], 0.0)
    xbuf[HALO:HALO + cl, :] = xm_ref[...]
    xbuf[HALO + cl:, :] = jnp.where(chunk < n_seq - 1, xr_ref[...], 0.0)
    taps = cw_ref.shape[0]
    left = taps // 2
    xin = cb_ref[...] + xbuf[pl.ds(HALO - left, cl), :] * cw_ref[0:1, :]
    for t in range(1, taps):
        xin = xin + xbuf[pl.ds(HALO - left + t, cl), :] * cw_ref[t:t + 1, :]

    gates = _dot(xin.astype(BF16), wg_ref[...]) + bg_ref[...]
    r = 0.5 * jnp.tanh(0.5 * gates[:, :lw]) + 0.5
    gate_i = 0.5 * jnp.tanh(0.5 * gates[:, lw:]) + 0.5
    log_a = (-LRU_C) * r * jax.nn.softplus(-lam_ref[...])
    a = jnp.exp(log_a)
    b = jnp.sqrt(-jnp.tanh(log_a) * (a * a + 1.0)) * gate_i * xin

    seg = cl // 8
    pitch = a_s.shape[1] // 8
    order = range(seg - 1, -1, -1) if reverse else range(seg)
    for g in range(lw // LANES):
        cs = slice(g * LANES, (g + 1) * LANES)
        for i in range(8):
            a_s[g, i * pitch:i * pitch + seg, :] = a[i * seg:(i + 1) * seg, cs]
            b_s[g, i * pitch:i * pitch + seg, :] = b[i * seg:(i + 1) * seg, cs]
        step = lambda ref, j: ref[g, pl.ds(j, 8, stride=pitch), :]
        for n, j in enumerate(order):
            aj, bj = step(a_s, j), step(b_s, j)
            f, p = (bj, aj) if n == 0 else (aj * f + bj, aj * p)
        first = lax.broadcasted_iota(jnp.int32, f.shape, 0) == (7 if reverse else 0)
        f = f + jnp.where(first, p * carry[:, cs], 0.0)
        dist = 1
        while dist < 8:
            f = f + p * _shift_rows(f, dist, 0.0, reverse)
            p = p * _shift_rows(p, dist, 1.0, reverse)
            dist *= 2
        h = jnp.where(first, carry[:, cs], _shift_rows(f, 1, 0.0, reverse))
        carry[:, cs] = f[0:1, :] if reverse else f[7:8, :]
        for j in order:
            h = step(a_s, j) * h + step(b_s, j)
            h_s[g, pl.ds(j, 8, stride=pitch), :] = h
    h = jnp.concatenate(
        [jnp.concatenate([h_s[g, i * pitch:i * pitch + seg, :] for i in range(8)], axis=0) for g in range(lw // LANES)],
        axis=1)
    if reverse:
        o_ref[...] = ((hf_ref[...] + h) * gg_ref[...]).astype(o_ref.dtype)
    else:
        o_ref[...] = h


def lru_scan(lay, xa, conv_w, conv_b, w_gate, b_gate, lam, reverse, hf=None, gg=None):
    nt, lw = xa.shape
    cl = LRU_CHUNK
    assert lay.t % cl == 0 and lay.lc % cl == 0
    ncc, nlc = lay.lc // cl, lay.t // cl
    n_lat_blocks = lay.n_lat // cl

    def chunk_block(b, s):
        in_ctx = s < ncc
        pos = jnp.where(in_ctx, s, s - ncc)
        if reverse:
            pos = jnp.where(in_ctx, ncc - 1 - pos, nlc - 1 - pos)
        return jnp.where(in_ctx, n_lat_blocks + b * ncc + pos, b * nlc + pos)

    per = cl // HALO
    last_halo = nt // HALO - 1
    main = pl.BlockSpec((cl, lw), lambda b, s: (chunk_block(b, s), 0))
    in_specs = [
        main,
        pl.BlockSpec((HALO, lw), lambda b, s: (jnp.maximum(chunk_block(b, s) * per - 1, 0), 0)),
        pl.BlockSpec((HALO, lw), lambda b, s: (jnp.minimum((chunk_block(b, s) + 1) * per, last_halo), 0)),
        _const_spec(conv_w.shape),
        _const_spec((1, lw)),
        _const_spec(w_gate.shape),
        _const_spec((1, 2 * lw)),
        _const_spec((1, lw)),
    ]
    args = [xa, xa, xa, conv_w, conv_b.reshape(1, lw), w_gate, b_gate.reshape(1, 2 * lw), lam.reshape(1, lw)]
    if reverse:
        in_specs += [main, main]
        args += [hf, gg]
    return pl.pallas_call(
        functools.partial(_lru_kernel, reverse=reverse, n_ctx_chunks=ncc, n_lat_chunks=nlc),
        out_shape=jax.ShapeDtypeStruct((nt, lw), BF16 if reverse else F32),
        grid=(lay.b, ncc + nlc),
        in_specs=in_specs,
        out_specs=main,
        scratch_shapes=[pltpu.VMEM((cl + 2 * HALO, lw), F32), pltpu.VMEM((1, lw), F32)]
        + [pltpu.VMEM((lw // LANES, cl + 8 * SEG_PAD, LANES), F32)] * 3,
        compiler_params=pltpu.CompilerParams(dimension_semantics=("arbitrary", "arbitrary")),
        name="lru_bwd" if reverse else "lru_fwd",
    )(*args)


def _win_attn_kernel(q_ref, kl_ref, kx_ref, vl_ref, vx_ref, sink_ref, o_ref, *, n_q_lat, n_kv, hd):
    j = pl.program_id(1)
    tq = q_ref.shape[0]
    t = kl_ref.shape[0]
    lc = kx_ref.shape[0]
    vch = vl_ref.shape[2]
    sq = WIN_SUB
    band = sq + 2 * WINDOW
    cols = GQA_GROUP * sq
    low = lax.broadcasted_iota(jnp.int32, (sq, 2 * hd), 1) % hd < hd // 2

    def colsum(x):
        return jnp.sum(jnp.sum(x.reshape(-1, 8, cols), axis=0), axis=0, keepdims=True)

    def colmax(x):
        return jnp.max(jnp.max(x.reshape(-1, 8, cols), axis=0), axis=0, keepdims=True)

    def attend(kv, r0, k_parts, vt_parts, bias):
        qs = []
        for g in range(GQA_GROUP):
            h = kv * GQA_GROUP + g
            qp = q_ref[r0:r0 + sq, 2 * hd * (h // 2):2 * hd * (h // 2 + 1)]
            qs.append(jnp.where(low if h % 2 == 0 else jnp.logical_not(low), qp, jnp.zeros_like(qp)))
        qs = jnp.concatenate(qs, axis=0)
        sink = jnp.concatenate(
            [jnp.concatenate([sink_ref[kv * GQA_GROUP + g:kv * GQA_GROUP + g + 1, :]] * (sq // LANES), axis=1)
             for g in range(GQA_GROUP)], axis=1) * LOG2_E
        s = [_dot_t(kp, qs) for kp in k_parts]
        if bias is not None:
            s[0] = s[0] + bias
        m = sink
        for sp in s:
            m = jnp.maximum(m, colmax(sp))
        den = jnp.exp2(sink - m)
        p = []
        for sp in s:
            pp = jnp.exp2(sp - m)
            den = den + colsum(pp)
            p.append(pp.astype(BF16))
        o_t = _dot(jnp.concatenate(vt_parts, axis=1), jnp.concatenate(p, axis=0)) * (1.0 / den)
        for pair in range(GQA_GROUP // 2):
            both = jnp.concatenate([o_t[:, (2 * pair + i) * sq:(2 * pair + i + 1) * sq] for i in range(2)], axis=0)
            c0 = 2 * hd * (kv * GQA_GROUP // 2 + pair)
            o_ref[r0:r0 + sq, c0:c0 + 2 * hd] = both.T.astype(o_ref.dtype)

    def ctx_parts(kv):
        ks = kx_ref[:, 2 * hd * kv:2 * hd * (kv + 1)]
        vts = [vx_ref[c][hd * kv:hd * (kv + 1), :] for c in range(lc // vch)]
        return ks, vts

    @pl.when(j < n_q_lat)
    def _():
        for r0 in range(0, tq, sq):
            q0 = j * tq + r0
            start = pl.multiple_of(jnp.clip(q0 - WINDOW, 0, t - band), LANES)
            kpos = start + lax.broadcasted_iota(jnp.int32, (band, sq), 0)
            qpos = q0 + lax.broadcasted_iota(jnp.int32, (band, sq), 1)
            bias = jnp.where(jnp.abs(kpos - qpos) > WINDOW, -jnp.inf, 0.0).astype(F32)
            bias = jnp.concatenate([bias] * GQA_GROUP, axis=1)
            c0 = start // vch
            for kv in range(n_kv):
                kx, vxs = ctx_parts(kv)
                kb = kl_ref[pl.ds(start, band), 2 * hd * kv:2 * hd * (kv + 1)]
                vbs = [vl_ref[c0 + c][hd * kv:hd * (kv + 1), :] for c in range(band // vch)]
                attend(kv, r0, [kb, kx], vbs + vxs, bias)

    @pl.when(j >= n_q_lat)
    def _():
        for r0 in range(0, tq, sq):
            for kv in range(n_kv):
                kx, vxs = ctx_parts(kv)
                attend(kv, r0, [kx], vxs, None)


def win_attn(lay, q, kd, vt, sink_rows, n_kv, hd):
    nt, qw = q.shape
    kw = kd.shape[1]
    tq = Q_TILE_WIN
    vch = vt.shape[2]
    assert lay.t % tq == 0 and lay.lc % tq == 0 and lay.n_lat % lay.lc == 0 and tq % 128 == 0
    assert vch == LANES and WINDOW % vch == 0 and lay.lc % vch == 0 and tq % WIN_SUB == 0 and lay.t >= WIN_SUB + 2 * WINDOW
    nb, ncq = lay.t // tq, lay.lc // tq
    ctx_q0 = lay.n_lat // tq
    ctx_k0 = lay.n_lat // lay.lc

    def qblock(b, j):
        return jnp.where(j < nb, b * nb + j, ctx_q0 + b * ncq + (j - nb))

    qspec = pl.BlockSpec((tq, qw), lambda b, j: (qblock(b, j), 0))
    return pl.pallas_call(
        functools.partial(_win_attn_kernel, n_q_lat=nb, n_kv=n_kv, hd=hd),
        out_shape=jax.ShapeDtypeStruct((nt, qw), BF16),
        grid=(lay.b, nb + ncq),
        in_specs=[
            qspec,
            pl.BlockSpec((lay.t, kw), lambda b, j: (b, 0)),
            pl.BlockSpec((lay.lc, kw), lambda b, j: (ctx_k0 + b, 0)),
            pl.BlockSpec((lay.t // vch, vt.shape[1], vch), lambda b, j: (b, 0, 0)),
            pl.BlockSpec((lay.lc // vch, vt.shape[1], vch), lambda b, j: (ctx_k0 + b, 0, 0)),
            _const_spec(sink_rows.shape),
        ],
        out_specs=qspec,
        compiler_params=pltpu.CompilerParams(vmem_limit_bytes=V7X_VMEM_LIMIT),
        name="win_attn",
    )(q, kd, kd, vt, vt, sink_rows)


def _odd_in_kernel(x_ref, mod_ref, g_ref, w_ref, cos_ref, sin_ref, gq_ref, gk_ref, q_ref, k_ref, vt_ref, *, qw, kw, hd):
    u = _pre(x_ref[...], g_ref[...], mod_ref[3:4, :], mod_ref[4:5, :]).astype(BF16)
    cos, sin = cos_ref[...], sin_ref[...]

    def normed_rope(o, width, gains, out_ref, post):
        cg = cos * gains[0:1, :]
        sg = sin * gains[1:2, :]
        step = 2 * hd
        for c0 in range(0, width, step):
            x2 = _dot(u, w_ref[:, o + c0:o + c0 + step])
            xp2 = _dot(u, w_ref[:, o + width + c0:o + width + c0 + step])
            for h0 in range(0, step, hd):
                x, xp = x2[:, h0:h0 + hd], xp2[:, h0:h0 + hd]
                r = lax.rsqrt(jnp.mean(x * x, axis=-1, keepdims=True) + EPS) * post
                out_ref[:, c0 + h0:c0 + h0 + hd] = (r * (x * cg + xp * sg)).astype(out_ref.dtype)

    normed_rope(0, qw, gq_ref[...], q_ref, hd ** -0.5 * LOG2_E)
    normed_rope(2 * qw, kw, gk_ref[...], k_ref, 1.0)
    o = 2 * qw + 2 * kw
    v = _dot(u, w_ref[:, o:o + kw])
    ch = vt_ref.shape[2]
    for c in range(vt_ref.shape[0]):
        vt_ref[c] = v[c * ch:(c + 1) * ch, :].T.astype(BF16)


def odd_in(lay, x_all, mod_l, g_pre, w_cat, cos_t, sin_t, gq, gk, qw, kw, hd):
    d = lay.d
    tm = lay.tile(IN_TILES)
    rb = lay.rope_block(tm)
    nt = lay.nt
    row = lambda i: (i, 0)
    ch = KEY_CHUNK
    assert tm % ch == 0
    return pl.pallas_call(
        functools.partial(_odd_in_kernel, qw=qw, kw=kw, hd=hd),
        out_shape=(
            jax.ShapeDtypeStruct((nt, qw), BF16),
            jax.ShapeDtypeStruct((nt, kw), BF16),
            jax.ShapeDtypeStruct((nt // ch, kw, ch), BF16),
        ),
        grid=(nt // tm,),
        in_specs=[
            pl.BlockSpec((tm, d), row),
            _mod_spec(lay, tm),
            _const_spec((1, d)),
            _resident_spec(w_cat.shape),
            pl.BlockSpec((tm, hd), lambda i: (rb(i), 0)),
            pl.BlockSpec((tm, hd), lambda i: (rb(i), 0)),
            _const_spec(gq.shape),
            _const_spec(gk.shape),
        ],
        out_specs=(pl.BlockSpec((tm, qw), row), pl.BlockSpec((tm, kw), row),
                   pl.BlockSpec((tm // ch, kw, ch), lambda i: (i, 0, 0))),
        compiler_params=pltpu.CompilerParams(vmem_limit_bytes=V7X_VMEM_LIMIT),
        name="odd_in",
    )(x_all, mod_l, g_pre.reshape(1, d), w_cat, cos_t, sin_t, gq, gk)


def _dense_attn_kernel(q_ref, kl_ref, kx_ref, vl_ref, vx_ref, o_ref, s_ref, m_ref, l_ref, acc_ref, *, n_q_lat, nb):
    j = pl.program_id(2)
    tq = q_ref.shape[0]
    t, hd = kl_ref.shape
    lc = kx_ref.shape[0]
    ch = vl_ref.shape[2]
    cols = GQA_GROUP * tq
    qs = jnp.concatenate([q_ref[:, g * hd:(g + 1) * hd] for g in range(GQA_GROUP)], axis=0)

    def sum8(x):
        return jnp.sum(x.reshape(-1, 8, cols), axis=0)

    def colmax(x):
        return jnp.max(jnp.max(x.reshape(-1, 8, cols), axis=0), axis=0, keepdims=True)

    sx = _dot_t(kx_ref[...], qs)
    m0 = colmax(sx)
    p = jnp.exp2(sx - m0)
    m_ref[...] = m0
    l_ref[...] = sum8(p)
    acc_ref[...] = _dot(jnp.concatenate([vx_ref[c] for c in range(lc // ch)], axis=1), p.astype(BF16))

    @pl.when(j < n_q_lat)
    def _():
        n_blocks = t // nb

        def produce(i):
            s = _dot_t(kl_ref[i * nb:(i + 1) * nb, :], qs)
            s_ref[i % 2] = s
            return colmax(s)

        m = m_ref[...]
        block_max = produce(0)
        for i in range(n_blocks):
            m_new = jnp.maximum(m, block_max)
            if i + 1 < n_blocks:
                block_max = produce(i + 1)
            alpha = jnp.exp2(m - m_new)
            p = jnp.exp2(s_ref[i % 2] - m_new)
            vt = jnp.concatenate([vl_ref[i * (nb // ch) + c] for c in range(nb // ch)], axis=1)
            l_ref[...] = l_ref[...] * alpha + sum8(p)
            acc_ref[...] = acc_ref[...] * alpha + _dot(vt, p.astype(BF16))
            m = m_new

    o_t = acc_ref[...] * (1.0 / jnp.sum(l_ref[...], axis=0, keepdims=True))
    for g in range(GQA_GROUP):
        o_ref[:, g * hd:(g + 1) * hd] = o_t[:, g * tq:(g + 1) * tq].T.astype(o_ref.dtype)


def _dense_attn_unshifted_kernel(q_ref, kl_ref, kx_ref, vl_ref, vx_ref, o_ref, *, n_q_lat, nb):
    j = pl.program_id(2)
    tq = q_ref.shape[0]
    t, hd = kl_ref.shape
    lc = kx_ref.shape[0]
    ch = vl_ref.shape[2]
    cols = GQA_GROUP * tq
    qs = jnp.concatenate([q_ref[:, g * hd:(g + 1) * hd] for g in range(GQA_GROUP)], axis=0)

    def block(k, vts):
        p = jnp.exp2(_dot_t(k, qs))
        return jnp.sum(p.reshape(-1, 8, cols), axis=0), _dot(jnp.concatenate(vts, axis=1), p.astype(BF16))

    def finish(l8, acc):
        o_t = acc * (1.0 / jnp.sum(l8, axis=0, keepdims=True))
        for g in range(GQA_GROUP):
            o_ref[:, g * hd:(g + 1) * hd] = o_t[:, g * tq:(g + 1) * tq].T.astype(o_ref.dtype)

    @pl.when(j < n_q_lat)
    def _():
        l8, acc = block(kx_ref[...], [vx_ref[c] for c in range(lc // ch)])
        for i in range(t // nb):
            dl, da = block(kl_ref[i * nb:(i + 1) * nb, :], [vl_ref[i * (nb // ch) + c] for c in range(nb // ch)])
            l8, acc = l8 + dl, acc + da
        finish(l8, acc)

    @pl.when(j >= n_q_lat)
    def _():
        finish(*block(kx_ref[...], [vx_ref[c] for c in range(lc // ch)]))


def dense_attn(lay, q, k, vt, hd, with_ctx_queries, shifted):
    nt, qw = q.shape
    tq = 128 if shifted else 256
    ch = vt.shape[2]
    assert lay.t % tq == 0 and lay.lc % tq == 0 and lay.n_lat % lay.lc == 0 and lay.t % ch == 0 and lay.lc % ch == 0
    n_kv = k.shape[1] // hd
    gw = GQA_GROUP * hd
    nql, nqc = lay.t // tq, lay.lc // tq
    ctx_q0 = lay.n_lat // tq
    ctx_k0 = lay.n_lat // lay.lc
    n_rows = nt if with_ctx_queries else lay.n_lat

    def qblock(b, kv, j):
        return (jnp.where(j < nql, b * nql + j, ctx_q0 + b * nqc + (j - nql)), kv)

    qspec = pl.BlockSpec((tq, gw), qblock)
    cols = GQA_GROUP * tq
    nb = next((n for n in (4096, 1024) if lay.t % n == 0), ch) if not shifted else (1024 if lay.t % 1024 == 0 else ch)
    assert nb % ch == 0 and lay.t % nb == 0
    if shifted:
        body = _dense_attn_kernel
        scratch = [pltpu.VMEM((2, nb, cols), F32), pltpu.VMEM((1, cols), F32), pltpu.VMEM((8, cols), F32),
                   pltpu.VMEM((hd, cols), F32)]
    else:
        body = _dense_attn_unshifted_kernel
        scratch = []
    return pl.pallas_call(
        functools.partial(body, n_q_lat=nql, nb=nb),
        out_shape=jax.ShapeDtypeStruct((n_rows, qw), BF16),
        grid=(lay.b, n_kv, nql + (nqc if with_ctx_queries else 0)),
        in_specs=[
            qspec,
            pl.BlockSpec((lay.t, hd), lambda b, kv, j: (b, kv)),
            pl.BlockSpec((lay.lc, hd), lambda b, kv, j: (ctx_k0 + b, kv)),
            pl.BlockSpec((lay.t // ch, hd, ch), lambda b, kv, j: (b, kv, 0)),
            pl.BlockSpec((lay.lc // ch, hd, ch), lambda b, kv, j: (ctx_k0 + b, kv, 0)),
        ],
        out_specs=qspec,
        scratch_shapes=scratch,
        compiler_params=pltpu.CompilerParams(vmem_limit_bytes=V7X_VMEM_LIMIT),
        name="dense_attn" if shifted else "dense_attn_unshifted",
    )(q, k, k, vt, vt)


def _split_pairs(w, hd, dup=False):
    d = w.shape[0]
    pairs = w.reshape(d, -1, hd // 2, 2)
    if dup:
        pairs = jnp.concatenate([pairs[:, :, None], pairs[:, :, None]], axis=2).reshape(d, -1, hd // 2, 2)
    per_group = LANES // hd
    groups = pairs.reshape(d, -1, per_group, hd // 2, 2)
    return jnp.moveaxis(groups, 4, 2).reshape(d, -1)


def _interleave_chunks(a, b, chunk):
    lead = a.shape[:-1]
    both = jnp.stack([a.reshape(lead + (-1, chunk)), b.reshape(lead + (-1, chunk))], axis=-2)
    return both.reshape(lead + (2 * a.shape[-1],))


def _swap_halves(w):
    halves = w.reshape(w.shape[0], -1, 2, LANES // 2)
    return jnp.concatenate([halves[:, :, 1:], halves[:, :, :1]], axis=2).reshape(w.shape)


def _block_diag(w):
    h, n, _ = w.shape
    eye = jnp.eye(h, dtype=w.dtype)
    return (eye[:, None, :, None] * w[:, :, None, :]).reshape(h * n, h * n)


def _rope_tables(t, hd, ident_rows):
    n_freq = hd // 4
    freq = ROPE_THETA ** (-jnp.arange(n_freq, dtype=F32) / n_freq)
    pos = jnp.arange(t)
    row = (pos // GRID_W).astype(F32)
    col = (pos % GRID_W).astype(F32)
    ang = jnp.concatenate([row[:, None] * freq, col[:, None] * freq], axis=-1)
    reps = LANES // hd
    cos = jnp.tile(jnp.cos(ang), (1, reps))
    sin = jnp.tile(jnp.sin(ang), (1, reps))
    cos = jnp.concatenate([cos, jnp.ones((ident_rows, LANES // 2), F32)], axis=0)
    sin = jnp.concatenate([sin, jnp.zeros((ident_rows, LANES // 2), F32)], axis=0)
    return jnp.concatenate([cos, cos], axis=1), jnp.concatenate([-sin, sin], axis=1)


def kernel(x, c, ctx, c_ctx, w_ada, b_ada, norm_pre, norm_post, ffn_w_gate, ffn_w_up, ffn_w_down, even_w_in,
           even_conv_w, even_conv_b, lru_w_a, lru_b_a, lru_w_x, lru_b_x, lru_lambda, attn_sink, even_w_out, odd_w_in,
           odd_q_norm, odd_k_norm, odd_w_out):
    b, t, d = x.shape
    lc = ctx.shape[1]
    depth = w_ada.shape[0]
    lay = _Layout(b, t, lc, d)
    assert b + 1 <= MOD_ROWS and w_ada.shape[2] == N_MOD * d

    lw = even_conv_w.shape[2]
    win_heads = attn_sink.shape[1]
    win_hd = (d - lw) // win_heads
    win_kv = win_heads // GQA_GROUP
    glb_hd = odd_q_norm.shape[1]
    glb_heads = d // glb_hd
    glb_kv = glb_heads // GQA_GROUP
    qw_e, kw_e = win_heads * win_hd, win_kv * win_hd
    qw_o, kw_o = glb_heads * glb_hd, glb_kv * glb_hd

    x_all = (x.reshape(b * t, d), ctx.reshape(b * lc, d))
    c_rows = jnp.concatenate([c, c_ctx[None, :], jnp.zeros((MOD_ROWS - b - 1, d), F32)], axis=0)
    mod = ada_mod(c_rows, w_ada, b_ada).reshape(depth, MOD_ROWS, N_MOD, d)

    tm = lay.tile(IN_TILES)
    cos_e, sin_e = _rope_tables(t, win_hd, tm)
    cos_o, sin_o = _rope_tables(t, glb_hd, tm)

    wgu16, wd16 = _interleave_chunks(ffn_w_gate, ffn_w_up, FFN_CHUNK).astype(BF16), ffn_w_down.astype(BF16)

    for l in range(depth):
        last = l == depth - 1
        i = l // 2
        mod_l = mod[l]
        x_all = ffn_sublayer(lay, x_all, lay.nt, mod_l, 0, norm_pre[l, 0], norm_post[l, 0], wgu16, wd16, (l, 0))
        rows_out = lay.n_lat if last else lay.nt
        if l % 2 == 0:
            w = even_w_in[i]
            o1, o2, o3, o4 = 2 * lw, 2 * lw + qw_e, 2 * lw + qw_e + kw_e, 2 * lw + qw_e + 2 * kw_e
            wq = w[:, o1:o2]
            wk = w[:, o2:o3]
            w_cat = jnp.concatenate([
                w[:, :o1], _split_pairs(wq, win_hd), _split_pairs(wk, win_hd, dup=True), w[:, o3:o4]], axis=1).astype(BF16)
            xa, gg, q, kd, vt = even_in(lay, x_all, mod_l, norm_pre[l, 1], w_cat, cos_e, sin_e, lw, qw_e, 2 * kw_e, kw_e,
                                        win_hd ** -0.5 * LOG2_E)
            spl_args = []
            for direction in range(2):
                w_gate = jnp.concatenate([_block_diag(lru_w_a[i, direction]), _block_diag(lru_w_x[i, direction])], axis=1).astype(BF16)
                b_gate = jnp.concatenate([lru_b_a[i, direction], lru_b_x[i, direction]])
                spl_args.append((w_gate, b_gate, lru_lambda[i, direction]))
            hf = lru_scan(lay, xa, even_conv_w[i], even_conv_b[i], *spl_args[0], reverse=False)
            ya = lru_scan(lay, xa, even_conv_w[i], even_conv_b[i], *spl_args[1], reverse=True, hf=hf, gg=gg)
            sink_rows = jnp.broadcast_to(attn_sink[i][:, None], (win_heads, 128)).astype(F32)
            ob = win_attn(lay, q, kd, vt, sink_rows, win_kv, win_hd)
            mix = (norm_post[l, 1], even_w_out[i].astype(BF16), [ya, ob])
        else:
            w = odd_w_in[i]
            wq, wk, wv = w[:, :qw_o], w[:, qw_o:qw_o + kw_o], w[:, qw_o + kw_o:]
            wq, wk = _split_pairs(wq, glb_hd), _split_pairs(wk, glb_hd)
            w_cat = jnp.concatenate([wq, _swap_halves(wq), wk, _swap_halves(wk), wv], axis=1).astype(BF16)
            gq, gk = (_split_pairs(g[None, :], glb_hd) for g in (odd_q_norm[i], odd_k_norm[i]))
            gq, gk = (jnp.concatenate([g, jnp.roll(g, LANES // 2, axis=1)], axis=0) for g in (gq, gk))
            q, k, vt = odd_in(lay, x_all, mod_l, norm_pre[l, 1], w_cat, cos_o, sin_o, gq, gk, qw_o, kw_o, glb_hd)
            bound = LOG2_E * glb_hd ** 0.5 * jnp.max(jnp.abs(odd_q_norm[i])) * jnp.max(jnp.abs(odd_k_norm[i]))
            o = lax.cond(
                bound <= MAX_UNSHIFTED_SCORE,
                functools.partial(dense_attn, lay, hd=glb_hd, with_ctx_queries=not last, shifted=False),
                functools.partial(dense_attn, lay, hd=glb_hd, with_ctx_queries=not last, shifted=True),
                q, k, vt)
            mix = (norm_post[l, 1], odd_w_out[i].astype(BF16), [o])
        x_all = ffn_sublayer(lay, x_all, rows_out, mod_l, 6, norm_pre[l, 2], norm_post[l, 2], wgu16, wd16, (l, 1),
                             mix=mix)
    return x_all[:b * t].reshape(b, t, d)
```
